```python
import math
import jax
import jax.numpy as jnp
from jax import lax
import numpy as np

D_MODEL = 1024
BATCH = 4
SEQ = 8192
DEPTH = 2

GRID_W = 64
CTX_LEN = 256
CHUNK = GRID_W
N_EVEN = (DEPTH + 1) // 2
N_ODD = DEPTH // 2
EPS = 1e-6

GDN_HEADS = 4
GDN_DK = 128
GDN_DV = 128
GDN_QK = GDN_HEADS * GDN_DK
GDN_V = GDN_HEADS * GDN_DV
SHORT_CONV = 3

GLA_HEADS = 4
GLA_DK = 64
GLA_DV = 128
GLA_QK = GLA_HEADS * GLA_DK
GLA_V = GLA_HEADS * GLA_DV
GLA_GATE_RANK = 16
GLA_GATE_TAU = 16.0

AB_SIZES = (GDN_QK, GDN_QK, GDN_V, GDN_V, 2 * GDN_HEADS, 2 * GDN_HEADS, GLA_QK, GLA_QK, GLA_V, GLA_V, 2 * GLA_GATE_RANK)
AB_SPLITS = tuple(sum(AB_SIZES[: i + 1]) for i in range(len(AB_SIZES) - 1))
AB_WIDTH = sum(AB_SIZES)
MIX_WIDTH = GDN_V + GLA_V

HY_ORDER = 2
HY_SHORT = 3
HY_EMB = 33
HY_BANDS = (HY_EMB - 1) // 2
HY_HIDDEN = 64
HY_MIN_DECAY = math.log(1e-2) / 1.5
HY_MAX_DECAY = math.log(1e-2) / 0.3
HY_SHIFT = 0.05

D_FF = -(-8 * D_MODEL // (3 * 256)) * 256

kernel_name = 'hybrid_gdn_gla_hyena_prefix_dit'


def rmsnorm(x, gain):
    xf = x.astype(jnp.float32)
    y = xf * lax.rsqrt(jnp.mean(xf * xf, axis=-1, keepdims=True) + EPS)
    return (y * gain.astype(jnp.float32)).astype(x.dtype)


def modulate(x, gain, shift, scale):
    return rmsnorm(x, gain) * (1 + scale) + shift


def head_rmsnorm(o, gain):
    return o * lax.rsqrt(jnp.mean(o * o, axis=-1, keepdims=True) + EPS) * gain.astype(jnp.float32)


def l2norm(u):
    return u * lax.rsqrt(jnp.sum(u * u, axis=-1, keepdims=True) + EPS)


def short_conv(x, w):
    k, l = w.shape[0], x.shape[1]
    xp = jnp.pad(x, ((0, 0), (k // 2, k // 2), (0, 0)))
    return sum(xp[:, j:j + l] * w[j] for j in range(k))


def swiglu(u, w1, w3, w2):
    return (jax.nn.silu(u @ w1) * (u @ w3)) @ w2


def to_chunks(t):
    b, l, hh = t.shape[:3]
    rows = l // GRID_W
    t = t.reshape((b, rows, CHUNK, hh) + t.shape[3:])
    return jnp.moveaxis(t, 3, 1)


def from_chunks(t):
    b, hh, n, cl, d = t.shape
    return jnp.moveaxis(t, 1, 3).reshape(b, n * cl, hh, d)


def run_gdn(q, k, v, g, beta, s0, emit):
    q, k, v, g, beta = map(to_chunks, (q, k, v, g, beta))
    dv = v.shape[-1]
    lower = jnp.tril(jnp.ones((CHUNK, CHUNK), dtype=bool))
    strict = jnp.tril(jnp.ones((CHUNK, CHUNK), dtype=bool), -1)
    gc = jnp.cumsum(g, axis=-1)
    decay = jnp.exp(jnp.where(lower, gc[..., :, None] - gc[..., None, :], -jnp.inf))
    kb = k * beta[..., None]
    lmat = jnp.where(strict, jnp.einsum('bhncd,bhnsd->bhncs', kb, k) * decay, 0.0) + jnp.eye(CHUNK, dtype=k.dtype)
    rhs = jnp.concatenate([v * beta[..., None], kb * jnp.exp(gc)[..., None]], axis=-1)
    sol = lax.linalg.triangular_solve(lmat, rhs, left_side=True, lower=True, unit_diagonal=True)
    u0, kcd = sol[..., :dv], sol[..., dv:]
    kd = k * jnp.exp(gc[..., -1:] - gc)[..., None]
    gl = jnp.exp(gc[..., -1])
    xs = (u0, kcd, kd, gl)
    if emit:
        attn = jnp.einsum('bhncd,bhnsd->bhncs', q, k) * decay
        xs = xs + (attn, q * jnp.exp(gc)[..., None])
    xs = tuple(jnp.moveaxis(t, 2, 0) for t in xs)

    def step(s, inp):
        u0_i, kcd_i, kd_i, gl_i = inp[:4]
        u = u0_i - jnp.einsum('bhcd,bhde->bhce', kcd_i, s)
        s_new = s * gl_i[..., None, None] + jnp.einsum('bhcd,bhce->bhde', kd_i, u)
        if not emit:
            return s_new, None
        attn_i, qd_i = inp[4:]
        o = jnp.einsum('bhcd,bhde->bhce', qd_i, s) + jnp.einsum('bhcs,bhse->bhce', attn_i, u)
        return s_new, o

    s_fin, out = lax.scan(step, s0, xs)
    out = from_chunks(jnp.moveaxis(out, 0, 2)) if emit else None
    return out, s_fin


def run_gla(q, k, v, log_a, s0, emit):
    q, k, v, log_a = map(to_chunks, (q, k, v, log_a))
    bc = jnp.cumsum(log_a, axis=-2)
    kd = k * jnp.exp(bc[..., -1:, :] - bc)
    gl = jnp.exp(bc[..., -1, :])
    xs = (kd, v, gl)
    if emit:
        lower = jnp.tril(jnp.ones((CHUNK, CHUNK), dtype=bool))
        ref = bc[..., CHUNK // 2:CHUNK // 2 + 1, :]
        attn = jnp.einsum('bhncd,bhnsd->bhncs', q * jnp.exp(bc - ref), k * jnp.exp(ref - bc))
        attn = jnp.where(lower, attn, 0.0)
        intra = jnp.einsum('bhncs,bhnse->bhnce', attn, v)
        xs = xs + (q * jnp.exp(bc),)
    xs = tuple(jnp.moveaxis(t, 2, 0) for t in xs)

    def step(s, inp):
        kd_i, v_i, gl_i = inp[:3]
        s_new = s * gl_i[..., :, None] + jnp.einsum('bhcd,bhce->bhde', kd_i, v_i)
        if not emit:
            return s_new, None
        o = jnp.einsum('bhcd,bhde->bhce', inp[3], s)
        return s_new, o

    s_fin, inter = lax.scan(step, s0, xs)
    out = from_chunks(jnp.moveaxis(inter, 0, 2) + intra) if emit else None
    return out, s_fin


def orient(t, d):
    return t if d == 0 else jnp.flip(t, axis=1)


def bidirectional(run, ctx_dirs, lat_dirs, s0, ctx_out):
    lat_outs, ctx_outs = [], []
    for d in range(2):
        oc, s_ctx = run(*[orient(t, d) for t in ctx_dirs[d]], s0, ctx_out)
        ol, _ = run(*[orient(t, d) for t in lat_dirs[d]], s_ctx, True)
        lat_outs.append(orient(ol, d))
        if ctx_out:
            ctx_outs.append(orient(oc, d))
    return lat_outs[0] + lat_outs[1], (ctx_outs[0] + ctx_outs[1] if ctx_out else None)


def ab_prepare(t, w_in, conv_w, a_log, dt_bias, gla_gate_w, gla_gate_b):
    b, l, _ = t.shape
    f32 = jnp.float32
    gq, gk, gv, gz, ga, gb, lq, lk, lv, lg, llr = jnp.split(t @ w_in, AB_SPLITS, axis=-1)
    qkv = jax.nn.silu(short_conv(jnp.concatenate([gq, gk, gv], axis=-1), conv_w)).astype(f32)
    gq, gk, gv = jnp.split(qkv, (GDN_QK, 2 * GDN_QK), axis=-1)

    def heads(u, h, d):
        return u.astype(f32).reshape(b, l, h, d)

    gate = jnp.einsum('bldr,drk->bldk', heads(llr, 2, GLA_GATE_RANK), gla_gate_w.astype(f32)) + gla_gate_b.astype(f32)
    return dict(
        gdn_q=l2norm(heads(gq, GDN_HEADS, GDN_DK)) * GDN_DK ** -0.5,
        gdn_k=l2norm(heads(gk, GDN_HEADS, GDN_DK)),
        gdn_v=heads(gv, GDN_HEADS, GDN_DV),
        gdn_g=-jnp.exp(a_log.astype(f32)) * jax.nn.softplus(heads(ga, 2, GDN_HEADS) + dt_bias.astype(f32)),
        gdn_beta=jax.nn.sigmoid(heads(gb, 2, GDN_HEADS)),
        gdn_z=heads(gz, GDN_HEADS, GDN_DV),
        gla_q=heads(lq, GLA_HEADS, GLA_DK) * GLA_DK ** -0.5,
        gla_k=heads(lk, GLA_HEADS, GLA_DK),
        gla_v=heads(lv, GLA_HEADS, GLA_DV),
        gla_log_a=(jax.nn.log_sigmoid(gate) / GLA_GATE_TAU).reshape(b, l, 2, GLA_HEADS, GLA_DK),
        gla_g=heads(lg, GLA_HEADS, GLA_DV),
    )


def ab_mixer(a, ac, w_in, conv_w, a_log, dt_bias, gdn_norm_w, gla_gate_w, gla_gate_b, gla_norm_w, w_out, ctx_out):
    pl = ab_prepare(a, w_in, conv_w, a_log, dt_bias, gla_gate_w, gla_gate_b)
    pc = ab_prepare(ac, w_in, conv_w, a_log, dt_bias, gla_gate_w, gla_gate_b)
    b = a.shape[0]
    s0_gdn = jnp.zeros((b, GDN_HEADS, GDN_DK, GDN_DV), jnp.float32)
    s0_gla = jnp.zeros((b, GLA_HEADS, GLA_DK, GLA_DV), jnp.float32)

    def gdn_dir(p, d):
        return (p['gdn_q'], p['gdn_k'], p['gdn_v'], p['gdn_g'][:, :, d], p['gdn_beta'][:, :, d])

    def gla_dir(p, d):
        return (p['gla_q'], p['gla_k'], p['gla_v'], p['gla_log_a'][:, :, d])

    o_gdn, oc_gdn = bidirectional(run_gdn, [gdn_dir(pc, d) for d in range(2)], [gdn_dir(pl, d) for d in range(2)], s0_gdn, ctx_out)
    o_gla, oc_gla = bidirectional(run_gla, [gla_dir(pc, d) for d in range(2)], [gla_dir(pl, d) for d in range(2)], s0_gla, ctx_out)

    def merge(og, ol, p, dtype):
        bb, l = og.shape[:2]
        y_gdn = (head_rmsnorm(og, gdn_norm_w) * jax.nn.silu(p['gdn_z'])).reshape(bb, l, GDN_V)
        y_gla = (head_rmsnorm(ol, gla_norm_w) * jax.nn.silu(p['gla_g'])).reshape(bb, l, GLA_V)
        return jnp.concatenate([y_gdn, y_gla], axis=-1).astype(dtype) @ w_out

    m = merge(o_gdn, o_gla, pl, a.dtype)
    mc = merge(oc_gdn, oc_gla, pc, ac.dtype) if ctx_out else None
    return m, mc


def hyena_filters(l, w1, b1, w2, b2, w3, b3, freq, filt_out):
    f32 = jnp.float32
    t = jnp.linspace(0.0, 1.0, l, dtype=f32)[:, None]
    w = (2.0 * math.pi / l) * jnp.arange(l, dtype=f32)[:, None]
    f = jnp.linspace(1e-4, HY_BANDS - 1, HY_BANDS, dtype=f32)[None, :]
    z = jnp.concatenate([t, jnp.cos(f * w), -jnp.sin(f * w)], axis=-1)
    fr = freq.astype(f32)
    hdn = jnp.sin(fr * (z @ w1.astype(f32) + b1.astype(f32)))
    hdn = jnp.sin(fr * (hdn @ w2.astype(f32) + b2.astype(f32)))
    hdn = jnp.sin(fr * (hdn @ w3.astype(f32) + b3.astype(f32)))
    h = (hdn @ filt_out.astype(f32)).reshape(l, 2, HY_ORDER, D_MODEL)
    deltas = jnp.abs(jnp.linspace(HY_MIN_DECAY, HY_MAX_DECAY, D_MODEL, dtype=f32))
    window = jnp.exp(-t.reshape(l, 1, 1, 1) * deltas) + HY_SHIFT
    return h * window


def long_conv(z, h_fwd, h_bwd, skip):
    l = z.shape[1]
    k = jnp.concatenate([h_fwd, jnp.zeros_like(h_fwd[:1]), jnp.flip(h_bwd[1:], axis=0)], axis=0)
    y = jnp.fft.irfft(jnp.fft.rfft(z, n=2 * l, axis=1) * jnp.fft.rfft(k, axis=0)[None], n=2 * l, axis=1)[:, :l]
    return y + z * skip.astype(jnp.float32)


def hyena_mixer(a, w_in, conv_w, w1, b1, w2, b2, w3, b3, freq, filt_out, skip, w_out):
    l = a.shape[1]
    u = short_conv(a @ w_in, conv_w).astype(jnp.float32)
    v, x1, x2 = jnp.split(u, 3, axis=-1)
    filt = hyena_filters(l, w1, b1, w2, b2, w3, b3, freq, filt_out)
    z = v
    for n, gate in enumerate((x1, x2)):
        z = gate * long_conv(z, filt[:, 0, n], filt[:, 1, n], skip[n])
    return z.astype(a.dtype) @ w_out


def setup_inputs(seed: int = 0) -> dict:
    key = jax.random.key(seed)
    ks = iter(jax.random.split(key, 40))
    f32 = jnp.float32
    D = D_MODEL

    def nrm(shape, scale):
        return jax.random.normal(next(ks), shape, f32) * scale

    dt = jnp.exp(jax.random.uniform(next(ks), (N_EVEN, 2, GDN_HEADS), f32, math.log(1e-3), math.log(1e-1)))
    return {
        'x': nrm((BATCH, SEQ, D), 1.0),
        'c': nrm((BATCH, D), 1.0),
        'ctx': nrm((BATCH, CTX_LEN, D), 1.0),
        'c_ctx': nrm((D,), 1.0),
        'mod_w': nrm((DEPTH, D, 6 * D), 0.5 * D ** -0.5),
        'mod_b': nrm((DEPTH, 6 * D), 0.02),
        'norm1_w': 1.0 + nrm((DEPTH, D), 0.02),
        'norm2_w': 1.0 + nrm((DEPTH, D), 0.02),
        'ab_w_in': nrm((N_EVEN, D, AB_WIDTH), D ** -0.5),
        'ab_conv_w': nrm((N_EVEN, SHORT_CONV, 2 * GDN_QK + GDN_V), SHORT_CONV ** -0.5),
        'gdn_a_log': jnp.log(jax.random.uniform(next(ks), (N_EVEN, 2, GDN_HEADS), f32, 1.0, 16.0)),
        'gdn_dt_bias': dt + jnp.log(-jnp.expm1(-dt)),
        'gdn_norm_w': 1.0 + nrm((N_EVEN, GDN_DV), 0.02),
        'gla_gate_w': nrm((N_EVEN, 2, GLA_GATE_RANK, GLA_QK), GLA_GATE_RANK ** -0.5),
        'gla_gate_b': nrm((N_EVEN, 2, GLA_QK), 0.1),
        'gla_norm_w': 1.0 + nrm((N_EVEN, GLA_DV), 0.02),
        'ab_w_out': nrm((N_EVEN, MIX_WIDTH, D), MIX_WIDTH ** -0.5),
        'hy_w_in': nrm((N_ODD, D, 3 * D), D ** -0.5),
        'hy_conv_w': nrm((N_ODD, HY_SHORT, 3 * D), HY_SHORT ** -0.5),
        'hy_pos_w1': nrm((N_ODD, HY_EMB, HY_HIDDEN), HY_EMB ** -0.5),
        'hy_pos_b1': nrm((N_ODD, HY_HIDDEN), 0.1),
        'hy_pos_w2': nrm((N_ODD, HY_HIDDEN, HY_HIDDEN), HY_HIDDEN ** -0.5),
        'hy_pos_b2': nrm((N_ODD, HY_HIDDEN), 0.1),
        'hy_pos_w3': nrm((N_ODD, HY_HIDDEN, HY_HIDDEN), HY_HIDDEN ** -0.5),
        'hy_pos_b3': nrm((N_ODD, HY_HIDDEN), 0.1),
        'hy_freq': 1.0 + nrm((N_ODD, HY_HIDDEN), 0.02),
        'hy_filt_out': nrm((N_ODD, HY_HIDDEN, 2 * HY_ORDER * D), 0.03 * HY_HIDDEN ** -0.5),
        'hy_skip': nrm((N_ODD, HY_ORDER, D), 0.5),
        'hy_w_out': nrm((N_ODD, D, D), D ** -0.5),
        'ffn_w1': nrm((DEPTH, D, D_FF), D ** -0.5),
        'ffn_w3': nrm((DEPTH, D, D_FF), D ** -0.5),
        'ffn_w2': nrm((DEPTH, D_FF, D), D_FF ** -0.5),
        'final_norm_w': 1.0 + nrm((D,), 0.02),
    }


def reference(x, c, ctx, c_ctx, mod_w, mod_b, norm1_w, norm2_w, ab_w_in, ab_conv_w, gdn_a_log, gdn_dt_bias,
              gdn_norm_w, gla_gate_w, gla_gate_b, gla_norm_w, ab_w_out, hy_w_in, hy_conv_w, hy_pos_w1, hy_pos_b1,
              hy_pos_w2, hy_pos_b2, hy_pos_w3, hy_pos_b3, hy_freq, hy_filt_out, hy_skip, hy_w_out,
              ffn_w1, ffn_w3, ffn_w2, final_norm_w):
    silu_c = jax.nn.silu(c)
    silu_cc = jax.nn.silu(c_ctx)
    h, hc = x, ctx
    for i in range(DEPTH):
        even = i % 2 == 0
        ctx_carry = any(j % 2 == 0 for j in range(i + 1, DEPTH))
        ctx_read = even or ctx_carry
        sh1, sc1, g1, sh2, sc2, g2 = jnp.split((silu_c @ mod_w[i] + mod_b[i])[:, None, :], 6, axis=-1)
        a = modulate(h, norm1_w[i], sh1, sc1)
        if ctx_read:
            csh1, csc1, cg1, csh2, csc2, cg2 = jnp.split(silu_cc @ mod_w[i] + mod_b[i], 6, axis=-1)
            ac = modulate(hc, norm1_w[i], csh1, csc1)
        if even:
            e = i // 2
            m, mc = ab_mixer(a, ac, ab_w_in[e], ab_conv_w[e], gdn_a_log[e], gdn_dt_bias[e], gdn_norm_w[e],
                             gla_gate_w[e], gla_gate_b[e], gla_norm_w[e], ab_w_out[e], ctx_carry)
        else:
            o = i // 2
            hy = (hy_w_in[o], hy_conv_w[o], hy_pos_w1[o], hy_pos_b1[o], hy_pos_w2[o], hy_pos_b2[o],
                  hy_pos_w3[o], hy_pos_b3[o], hy_freq[o], hy_filt_out[o], hy_skip[o], hy_w_out[o])
            m = hyena_mixer(a, *hy)
            mc = hyena_mixer(ac, *hy) if ctx_carry else None
        h = h + g1 * m
        h = h + g2 * swiglu(modulate(h, norm2_w[i], sh2, sc2), ffn_w1[i], ffn_w3[i], ffn_w2[i])
        if ctx_carry:
            hc = hc + cg1 * mc
            hc = hc + cg2 * swiglu(modulate(hc, norm2_w[i], csh2, csc2), ffn_w1[i], ffn_w3[i], ffn_w2[i])
    return rmsnorm(h, final_norm_w)
```

```python
import functools
import math

import jax
import jax.numpy as jnp
from jax import lax
from jax.experimental import pallas as pl
from jax.experimental.pallas import tpu as pltpu

F32 = jnp.float32
BF16 = jnp.bfloat16
HI = lax.Precision.HIGHEST

EPS = 1e-6
CHUNK = 64
GDN_HEADS, GDN_DK, GDN_DV = 4, 128, 128
GLA_HEADS, GLA_DK, GLA_DV = 4, 64, 128
GLA_GATE_RANK = 16
GLA_GATE_TAU = 16.0
GDN_QK = GDN_HEADS * GDN_DK
GDN_V = GDN_HEADS * GDN_DV
GLA_QK = GLA_HEADS * GLA_DK
GLA_V = GLA_HEADS * GLA_DV
HY_EMB = 33
HY_BANDS = (HY_EMB - 1) // 2
HY_MIN_DECAY = math.log(1e-2) / 1.5
HY_MAX_DECAY = math.log(1e-2) / 0.3
HY_SHIFT = 0.05

LANE = 128
SUBLANE = 8
VMEM_LIMIT_BYTES = 56 * 1024 * 1024

COL_GQ, COL_GK, COL_GV, COL_GZ = 0, 512, 1024, 1536
COL_LQ, COL_LK, COL_LV, COL_LG = 2048, 2304, 2560, 3072
COL_SMALL = 3584
AB_COLS = COL_SMALL + LANE
SM_G, SM_B, SM_LR = 0, 8, 16


def _params(*sem):
    return pltpu.CompilerParams(dimension_semantics=sem, vmem_limit_bytes=VMEM_LIMIT_BYTES)


def _resident(shape, index_map):
    return pl.BlockSpec(shape, index_map, pipeline_mode=pl.Buffered(1))


def _dot(a, b):
    return jnp.dot(a.astype(BF16), b.astype(BF16), preferred_element_type=F32)


def _dot_nt(a, b):
    return lax.dot_general(a.astype(BF16), b.astype(BF16), (((1,), (1,)), ((), ())),
                           preferred_element_type=F32)


def _dot_hi(a, b):
    return jnp.dot(a, b, preferred_element_type=F32, precision=HI)


def _sigmoid(x):
    return 1.0 / (1.0 + jnp.exp(-x))


def _silu(x):
    return x * _sigmoid(x)


def _softplus(x):
    return jnp.maximum(x, 0.0) + jnp.log(1.0 + jnp.exp(-jnp.abs(x)))


def _mod_kernel(c_ref, w_ref, b_ref, o_ref):
    o_ref[...] = _dot_hi(_silu(c_ref[...]), w_ref[...]) + b_ref[...]


def _modulation(cvec, mod_w, mod_b):
    depth, d, n = mod_w.shape
    tn = n // 4
    return pl.pallas_call(
        _mod_kernel,
        grid=(depth, n // tn),
        in_specs=[pl.BlockSpec((SUBLANE, d), lambda i, j: (0, 0)),
                  pl.BlockSpec((None, d, tn), lambda i, j: (i, 0, j)),
                  pl.BlockSpec((None, 1, tn), lambda i, j: (i, 0, j))],
        out_specs=pl.BlockSpec((None, SUBLANE, tn), lambda i, j: (i, 0, j)),
        out_shape=jax.ShapeDtypeStruct((depth, SUBLANE, n), F32),
        compiler_params=_params("parallel", "parallel"),
        name="modulation",
    )(cvec, mod_w, mod_b.reshape(depth, 1, n))


def _modulated(x, gain, shift, scale):
    ms = jnp.mean(x * x, axis=-1, keepdims=True)
    return x * lax.rsqrt(ms + EPS) * gain * (1.0 + scale) + shift


def _norm_proj_kernel(x_ref, g_ref, sh_ref, sc_ref, w_ref, o_ref):
    a = _modulated(x_ref[...], g_ref[...], sh_ref[...], sc_ref[...])
    o_ref[...] = jnp.dot(a.astype(BF16), w_ref[...], preferred_element_type=F32)


def _norm_proj(x, gain, shift, scale, w, tm, group_tiles):
    b, l, d = x.shape
    n = w.shape[1]
    ngroups = shift.shape[1]

    def mod_map(bi, i):
        return (bi, jnp.where(i < group_tiles, 0, ngroups - 1), 0, 0)

    return pl.pallas_call(
        _norm_proj_kernel,
        grid=(b, l // tm),
        in_specs=[pl.BlockSpec((None, tm, d), lambda bi, i: (bi, i, 0)),
                  _resident((1, d), lambda bi, i: (0, 0)),
                  pl.BlockSpec((None, None, 1, d), mod_map),
                  pl.BlockSpec((None, None, 1, d), mod_map),
                  _resident((d, n), lambda bi, i: (0, 0))],
        out_specs=pl.BlockSpec((None, tm, n), lambda bi, i: (bi, i, 0)),
        out_shape=jax.ShapeDtypeStruct((b, l, n), F32),
        compiler_params=_params("parallel", "parallel"),
        name="norm_proj",
    )(x, gain.reshape(1, d), shift, scale, w)


def _conv3(xm, prev_row, next_row, w, has_prev, has_next):
    t = xm.shape[0]
    rows = lax.broadcasted_iota(jnp.int32, xm.shape, 0)
    prev_row = jnp.where(has_prev, prev_row, 0.0)
    next_row = jnp.where(has_next, next_row, 0.0)
    xp = jnp.where(rows == 0, prev_row, pltpu.roll(xm, 1, 0))
    xn = jnp.where(rows == t - 1, next_row, pltpu.roll(xm, t - 1, 0))
    return xp * w[0:1, :] + xm * w[1:2, :] + xn * w[2:3, :]


def _ab_prep_kernel(xm_ref, xp_ref, xn_ref, sm_ref, cw_ref, alog_ref, dtb_ref, gw_ref, gb_ref,
                    q_ref, k_ref, v_ref, so_ref, la_ref, *, tile, seq_starts, seq_ends):
    row0 = pl.program_id(1) * tile
    has_prev = functools.reduce(jnp.logical_and, [row0 != s for s in seq_starts])
    has_next = functools.reduce(jnp.logical_and, [row0 + tile != e for e in seq_ends])
    y = _conv3(xm_ref[...], xp_ref[SUBLANE - 1:SUBLANE, :], xn_ref[0:1, :], cw_ref[...], has_prev, has_next)
    y = _silu(y)
    for h in range(GDN_HEADS):
        sl = slice(h * GDN_DK, (h + 1) * GDN_DK)
        qh = y[:, COL_GQ + h * GDN_DK:COL_GQ + (h + 1) * GDN_DK]
        kh = y[:, COL_GK + h * GDN_DK:COL_GK + (h + 1) * GDN_DK]
        q_ref[:, sl] = qh * lax.rsqrt(jnp.sum(qh * qh, axis=-1, keepdims=True) + EPS) * (GDN_DK ** -0.5)
        k_ref[:, sl] = kh * lax.rsqrt(jnp.sum(kh * kh, axis=-1, keepdims=True) + EPS)
    v_ref[...] = y[:, COL_GV:COL_GV + GDN_V]
    sm = sm_ref[...]
    g = -jnp.exp(alog_ref[...]) * _softplus(sm + dtb_ref[...])
    beta = _sigmoid(sm)
    lanes = lax.broadcasted_iota(jnp.int32, sm.shape, 1)
    so_ref[...] = jnp.where(lanes < SM_B, g, beta)
    gate = _dot(sm, gw_ref[...]) + gb_ref[...]
    la_ref[...] = (jnp.minimum(gate, 0.0) - jnp.log(1.0 + jnp.exp(-jnp.abs(gate)))) * (1.0 / GLA_GATE_TAU)


def _ab_prep(proj, conv_w, alog_row, dtb_row, gate_w, gate_b, tile, seq_starts, seq_ends):
    b, l, _ = proj.shape
    cw = GDN_QK * 2 + GDN_V
    nrow8 = l // SUBLANE
    kern = functools.partial(_ab_prep_kernel, tile=tile, seq_starts=seq_starts, seq_ends=seq_ends)
    sds = jax.ShapeDtypeStruct
    return pl.pallas_call(
        kern,
        grid=(b, l // tile),
        in_specs=[pl.BlockSpec((None, tile, cw), lambda bi, i: (bi, i, 0)),
                  pl.BlockSpec((None, SUBLANE, cw),
                               lambda bi, i: (bi, jnp.maximum(i * (tile // SUBLANE) - 1, 0), 0)),
                  pl.BlockSpec((None, SUBLANE, cw),
                               lambda bi, i: (bi, jnp.minimum((i + 1) * (tile // SUBLANE), nrow8 - 1), 0)),
                  pl.BlockSpec((None, tile, LANE), lambda bi, i: (bi, i, COL_SMALL // LANE)),
                  _resident((3, cw), lambda bi, i: (0, 0)),
                  _resident((1, LANE), lambda bi, i: (0, 0)),
                  _resident((1, LANE), lambda bi, i: (0, 0)),
                  _resident((LANE, 2 * GLA_QK), lambda bi, i: (0, 0)),
                  _resident((1, 2 * GLA_QK), lambda bi, i: (0, 0))],
        out_specs=[pl.BlockSpec((None, tile, GDN_QK), lambda bi, i: (bi, i, 0)),
                   pl.BlockSpec((None, tile, GDN_QK), lambda bi, i: (bi, i, 0)),
                   pl.BlockSpec((None, tile, GDN_V), lambda bi, i: (bi, i, 0)),
                   pl.BlockSpec((None, tile, LANE), lambda bi, i: (bi, i, 0)),
                   pl.BlockSpec((None, tile, 2 * GLA_QK), lambda bi, i: (bi, i, 0))],
        out_shape=[sds((b, l, GDN_QK), F32), sds((b, l, GDN_QK), F32), sds((b, l, GDN_V), F32),
                   sds((b, l, LANE), F32), sds((b, l, 2 * GLA_QK), F32)],
        compiler_params=_params("parallel", "parallel"),
        name="ab_prep",
    )(proj, proj, proj, proj, conv_w, alog_row, dtb_row, gate_w, gate_b)


def _ext_chunk(d, s, nctx, ntot):
    if d == 0:
        return s
    return jnp.where(s < nctx, nctx - 1 - s, ntot + nctx - 1 - s)


def _lat_chunk(d, s, nctx, ntot):
    nlat = ntot - nctx
    if d == 0:
        return jnp.maximum(s - nctx, 0)
    return jnp.minimum(ntot - 1 - s, nlat - 1)


def _tri_masks(d):
    ii = lax.broadcasted_iota(jnp.int32, (CHUNK, CHUNK), 0)
    jj = lax.broadcasted_iota(jnp.int32, (CHUNK, CHUNK), 1)
    if d == 0:
        return ii >= jj, ii > jj, ii == jj
    return ii <= jj, ii < jj, ii == jj


def _unit_tri_inverse(a, eye):
    t = eye - a
    p = _dot_hi(a, a)
    n = 2
    while True:
        t = t + _dot_hi(t, p)
        n *= 2
        if n >= CHUNK:
            return t
        p = _dot_hi(p, p)


def _gdn_direction(d, q, k, v, sm, s_ref, o_ref, write):
    incl, strict, diag = _tri_masks(d)
    eye = diag.astype(F32)
    gc = _dot_hi(incl.astype(F32), sm)
    tot = jnp.sum(sm, axis=0, keepdims=True)
    gct = gc.T
    for h in range(GDN_HEADS):
        col = SM_G + d * GDN_HEADS + h
        bcol = SM_B + d * GDN_HEADS + h
        gcol = gc[:, col:col + 1]
        grow = gct[col:col + 1, :]
        beta = sm[:, bcol:bcol + 1]
        tcol = tot[:, col:col + 1]
        qh = q[:, h * GDN_DK:(h + 1) * GDN_DK]
        kh = k[:, h * GDN_DK:(h + 1) * GDN_DK]
        vh = v[:, h * GDN_DV:(h + 1) * GDN_DV]
        decay = jnp.exp(jnp.where(incl, gcol - grow, -jnp.inf))
        kb = kh * beta
        a = jnp.where(strict, _dot_nt(kb, kh) * decay, 0.0)
        tinv = _unit_tri_inverse(a, eye)
        egc = jnp.exp(gcol)
        u0 = _dot_hi(tinv, vh * beta)
        kcd = _dot_hi(tinv, kb * egc)
        kd = kh * jnp.exp(tcol - gcol)
        s = s_ref[d, h]
        u = u0 - _dot(kcd, s)
        attn = _dot_nt(qh, kh) * decay
        o = _dot(qh * egc, s) + _dot(attn, u)
        s_ref[d, h] = s * jnp.exp(tcol) + _dot(kd.T, u)

        @pl.when(write)
        def _():
            o_ref[:, h * GDN_DV:(h + 1) * GDN_DV] = o


def _gdn_kernel(qf, kf, vf, sf, qb, kb, vb, sb, of_ref, ob_ref, s_ref, *, nctx):
    s = pl.program_id(1)

    @pl.when(s == 0)
    def _():
        s_ref[...] = jnp.zeros(s_ref.shape, F32)

    write = s >= nctx
    _gdn_direction(0, qf[...], kf[...], vf[...], sf[...], s_ref, of_ref, write)
    _gdn_direction(1, qb[...], kb[...], vb[...], sb[...], s_ref, ob_ref, write)


def _gdn_scan(q, k, v, sm, nctx):
    b, lext, _ = q.shape
    ntot = lext // CHUNK
    nlat = ntot - nctx

    def in_map(d):
        return lambda bi, s: (bi, _ext_chunk(d, s, nctx, ntot), 0)

    def out_map(d):
        return lambda bi, s: (bi, _lat_chunk(d, s, nctx, ntot), 0)

    def specs(d):
        return [pl.BlockSpec((None, CHUNK, GDN_QK), in_map(d)), pl.BlockSpec((None, CHUNK, GDN_QK), in_map(d)),
                pl.BlockSpec((None, CHUNK, GDN_V), in_map(d)), pl.BlockSpec((None, CHUNK, LANE), in_map(d))]

    sds = jax.ShapeDtypeStruct((b, nlat * CHUNK, GDN_V), F32)
    return pl.pallas_call(
        functools.partial(_gdn_kernel, nctx=nctx),
        grid=(b, ntot),
        in_specs=specs(0) + specs(1),
        out_specs=[pl.BlockSpec((None, CHUNK, GDN_V), out_map(0)), pl.BlockSpec((None, CHUNK, GDN_V), out_map(1))],
        out_shape=[sds, sds],
        scratch_shapes=[pltpu.VMEM((2, GDN_HEADS, GDN_DK, GDN_DV), F32)],
        compiler_params=_params("parallel", "arbitrary"),
        name="gdn_scan",
    )(q, k, v, sm, q, k, v, sm)


def _gla_direction(d, q, k, v, la, s_ref, o_ref, write):
    incl, _, _ = _tri_masks(d)
    bc = _dot_hi(incl.astype(F32), la)
    tot = jnp.sum(la, axis=0, keepdims=True)
    mid = CHUNK // 2 if d == 0 else CHUNK - 1 - CHUNK // 2
    ref = bc[mid:mid + 1, :]
    qs = q * (GLA_DK ** -0.5)
    qa = qs * jnp.exp(bc - ref)
    ka = k * jnp.exp(ref - bc)
    qd = qs * jnp.exp(bc)
    kdt = (k * jnp.exp(tot - bc)).T
    glc = jnp.exp(jnp.sum(la.T, axis=1, keepdims=True))
    for h in range(GLA_HEADS):
        ks = slice(h * GLA_DK, (h + 1) * GLA_DK)
        vh = v[:, h * GLA_DV:(h + 1) * GLA_DV]
        attn = jnp.where(incl, _dot_nt(qa[:, ks], ka[:, ks]), 0.0)
        s = s_ref[d, h]
        o = _dot(qd[:, ks], s) + _dot(attn, vh)
        s_ref[d, h] = s * glc[ks, :] + _dot(kdt[ks, :], vh)

        @pl.when(write)
        def _():
            o_ref[:, h * GLA_DV:(h + 1) * GLA_DV] = o


def _gla_kernel(qf, kf, vf, lf, qb, kb, vb, lb, of_ref, ob_ref, s_ref, *, nctx):
    s = pl.program_id(1)

    @pl.when(s == 0)
    def _():
        s_ref[...] = jnp.zeros(s_ref.shape, F32)

    write = s >= nctx
    _gla_direction(0, qf[...], kf[...], vf[...], lf[...], s_ref, of_ref, write)
    _gla_direction(1, qb[...], kb[...], vb[...], lb[...], s_ref, ob_ref, write)


def _gla_scan(proj, loga, nctx):
    b, lext, _ = proj.shape
    ntot = lext // CHUNK
    nlat = ntot - nctx

    def in_map(d, colblk):
        return lambda bi, s: (bi, _ext_chunk(d, s, nctx, ntot), colblk)

    def out_map(d):
        return lambda bi, s: (bi, _lat_chunk(d, s, nctx, ntot), 0)

    def specs(d):
        return [pl.BlockSpec((None, CHUNK, GLA_QK), in_map(d, COL_LQ // GLA_QK)),
                pl.BlockSpec((None, CHUNK, GLA_QK), in_map(d, COL_LK // GLA_QK)),
                pl.BlockSpec((None, CHUNK, GLA_V), in_map(d, COL_LV // GLA_V)),
                pl.BlockSpec((None, CHUNK, GLA_QK), in_map(d, d))]

    sds = jax.ShapeDtypeStruct((b, nlat * CHUNK, GLA_V), F32)
    return pl.pallas_call(
        functools.partial(_gla_kernel, nctx=nctx),
        grid=(b, ntot),
        in_specs=specs(0) + specs(1),
        out_specs=[pl.BlockSpec((None, CHUNK, GLA_V), out_map(0)), pl.BlockSpec((None, CHUNK, GLA_V), out_map(1))],
        out_shape=[sds, sds],
        scratch_shapes=[pltpu.VMEM((2, GLA_HEADS, GLA_DK, GLA_DV), F32)],
        compiler_params=_params("parallel", "arbitrary"),
        name="gla_scan",
    )(proj, proj, proj, loga, proj, proj, proj, loga)


def _head_gate(o, gain, z, heads, dv):
    parts = []
    for h in range(heads):
        sl = slice(h * dv, (h + 1) * dv)
        oh = o[:, sl]
        parts.append(oh * lax.rsqrt(jnp.mean(oh * oh, axis=-1, keepdims=True) + EPS) * gain * _silu(z[:, sl]))
    return jnp.concatenate(parts, axis=-1)


def _ab_merge_kernel(gf_ref, gb_ref, lf_ref, lb_ref, gz_ref, lg_ref, gn_ref, ln_ref, w_ref, h_ref, g1_ref, o_ref):
    yg = _head_gate(gf_ref[...] + gb_ref[...], gn_ref[...], gz_ref[...], GDN_HEADS, GDN_DV)
    yl = _head_gate(lf_ref[...] + lb_ref[...], ln_ref[...], lg_ref[...], GLA_HEADS, GLA_DV)
    m = (jnp.dot(yg.astype(BF16), w_ref[0:GDN_V, :], preferred_element_type=F32)
         + jnp.dot(yl.astype(BF16), w_ref[GDN_V:GDN_V + GLA_V, :], preferred_element_type=F32))
    o_ref[...] = h_ref[...] + g1_ref[...] * m


def _ab_merge(og_f, og_b, ol_f, ol_b, proj, gdn_norm, gla_norm, w_out, h, g1, tm, ctx_tiles):
    b, l, d = h.shape

    def lat(bi, i):
        return (bi, i, 0)

    def ext(colblk):
        return lambda bi, i: (bi, i + ctx_tiles, colblk)

    return pl.pallas_call(
        _ab_merge_kernel,
        grid=(b, l // tm),
        in_specs=[pl.BlockSpec((None, tm, GDN_V), lat), pl.BlockSpec((None, tm, GDN_V), lat),
                  pl.BlockSpec((None, tm, GLA_V), lat), pl.BlockSpec((None, tm, GLA_V), lat),
                  pl.BlockSpec((None, tm, GDN_V), ext(COL_GZ // GDN_V)),
                  pl.BlockSpec((None, tm, GLA_V), ext(COL_LG // GLA_V)),
                  _resident((1, GDN_DV), lambda bi, i: (0, 0)),
                  _resident((1, GLA_DV), lambda bi, i: (0, 0)),
                  _resident((GDN_V + GLA_V, d), lambda bi, i: (0, 0)),
                  pl.BlockSpec((None, tm, d), lat),
                  pl.BlockSpec((None, 1, d), lambda bi, i: (bi, 0, 0))],
        out_specs=pl.BlockSpec((None, tm, d), lat),
        out_shape=jax.ShapeDtypeStruct((b, l, d), F32),
        compiler_params=_params("parallel", "parallel"),
        name="ab_merge",
    )(og_f, og_b, ol_f, ol_b, proj, proj, gdn_norm, gla_norm, w_out, h, g1)


def _ffn_kernel(h_ref, gn_ref, sh_ref, sc_ref, g2_ref, w1_ref, w3_ref, w2_ref, fn_ref, o_ref, *, nsplit, final_norm):
    h = h_ref[...]
    a = _modulated(h, gn_ref[...], sh_ref[...], sc_ref[...]).astype(BF16)
    dff = w1_ref.shape[1]
    fc = dff // nsplit
    acc = jnp.zeros(h.shape, F32)
    for j in range(nsplit):
        u1 = jnp.dot(a, w1_ref[:, j * fc:(j + 1) * fc], preferred_element_type=F32)
        u3 = jnp.dot(a, w3_ref[:, j * fc:(j + 1) * fc], preferred_element_type=F32)
        acc = acc + jnp.dot((_silu(u1) * u3).astype(BF16), w2_ref[j * fc:(j + 1) * fc, :],
                            preferred_element_type=F32)
    y = h + g2_ref[...] * acc
    if final_norm:
        y = y * lax.rsqrt(jnp.mean(y * y, axis=-1, keepdims=True) + EPS) * fn_ref[...]
    o_ref[...] = y


def _ffn(h, gain, shift, scale, g2, w1, w3, w2, final_w, tm, final_norm):
    b, l, d = h.shape
    dff = w1.shape[1]

    def vec(bi, i):
        return (bi, 0, 0)

    def const(bi, i):
        return (0, 0)

    return pl.pallas_call(
        functools.partial(_ffn_kernel, nsplit=2, final_norm=final_norm),
        grid=(b, l // tm),
        in_specs=[pl.BlockSpec((None, tm, d), lambda bi, i: (bi, i, 0)),
                  _resident((1, d), const),
                  pl.BlockSpec((None, 1, d), vec), pl.BlockSpec((None, 1, d), vec), pl.BlockSpec((None, 1, d), vec),
                  _resident((d, dff), const), _resident((d, dff), const), _resident((dff, d), const),
                  _resident((1, d), const)],
        out_specs=pl.BlockSpec((None, tm, d), lambda bi, i: (bi, i, 0)),
        out_shape=jax.ShapeDtypeStruct((b, l, d), F32),
        compiler_params=_params("parallel", "parallel"),
        name="ffn",
    )(h, gain.reshape(1, d), shift, scale, g2, w1, w3, w2, final_w.reshape(1, d))


def _hy_conv_kernel(xm_ref, xp_ref, xn_ref, w_ref, o_ref, *, tile, seq_len):
    row0 = pl.program_id(2) * tile
    o_ref[...] = _conv3(xm_ref[...], xp_ref[SUBLANE - 1:SUBLANE, :], xn_ref[0:1, :], w_ref[...],
                        row0 != 0, row0 + tile != seq_len)


def _hy_conv(u, conv_w, d, tile):
    b, l, n = u.shape
    groups = n // d
    nrow8 = l // SUBLANE
    return pl.pallas_call(
        functools.partial(_hy_conv_kernel, tile=tile, seq_len=l),
        grid=(groups, b, l // tile),
        in_specs=[pl.BlockSpec((None, tile, d), lambda j, bi, i: (bi, i, j)),
                  pl.BlockSpec((None, SUBLANE, d),
                               lambda j, bi, i: (bi, jnp.maximum(i * (tile // SUBLANE) - 1, 0), j)),
                  pl.BlockSpec((None, SUBLANE, d),
                               lambda j, bi, i: (bi, jnp.minimum((i + 1) * (tile // SUBLANE), nrow8 - 1), j)),
                  pl.BlockSpec((3, d), lambda j, bi, i: (0, j))],
        out_specs=pl.BlockSpec((None, None, tile, d), lambda j, bi, i: (j, bi, i, 0)),
        out_shape=jax.ShapeDtypeStruct((groups, b, l, d), F32),
        compiler_params=_params("parallel", "parallel", "parallel"),
        name="hy_conv",
    )(u, u, u, conv_w)


def _hy_filter_kernel(z_ref, w1_ref, b1_ref, w2_ref, b2_ref, w3_ref, b3_ref, fr_ref, fo_ref, dl_ref, o_ref,
                      *, tile, seq_len, d):
    z = z_ref[...]
    fr = fr_ref[...]
    hdn = jnp.sin(fr * (_dot_hi(z, w1_ref[...]) + b1_ref[...]))
    hdn = jnp.sin(fr * (_dot_hi(hdn, w2_ref[...]) + b2_ref[...]))
    hdn = jnp.sin(fr * (_dot_hi(hdn, w3_ref[...]) + b3_ref[...]))
    filt = _dot_hi(hdn, fo_ref[...])
    window = jnp.exp(-z[:, 0:1] * dl_ref[...]) + HY_SHIFT
    rows = pl.program_id(0) * tile + lax.broadcasted_iota(jnp.int32, (tile, 1), 0)
    out = jnp.where(rows != seq_len, filt * window, 0.0)
    for n in range(o_ref.shape[0]):
        o_ref[n] = out[:, n * d:(n + 1) * d]


def _hy_filters(zext, w1, b1, w2, b2, w3, b3, freq, fo_dir, deltas, seq_len, d, tile):
    n2 = zext.shape[0]
    hid = w1.shape[1]
    orders = fo_dir.shape[2] // d
    half_tiles = seq_len // tile

    def const(i):
        return (0, 0)

    return pl.pallas_call(
        functools.partial(_hy_filter_kernel, tile=tile, seq_len=seq_len, d=d),
        grid=(n2 // tile,),
        in_specs=[pl.BlockSpec((tile, zext.shape[1]), lambda i: (i, 0)),
                  pl.BlockSpec(w1.shape, const), pl.BlockSpec((1, hid), const),
                  pl.BlockSpec(w2.shape, const), pl.BlockSpec((1, hid), const),
                  pl.BlockSpec(w3.shape, const), pl.BlockSpec((1, hid), const),
                  pl.BlockSpec((1, hid), const),
                  pl.BlockSpec((None, hid, orders * d), lambda i: (jnp.where(i < half_tiles, 0, 1), 0, 0)),
                  pl.BlockSpec((1, orders * d), const)],
        out_specs=pl.BlockSpec((orders, tile, d), lambda i: (0, i, 0)),
        out_shape=jax.ShapeDtypeStruct((orders, n2, d), F32),
        compiler_params=_params("parallel"),
        name="hy_filters",
    )(zext, w1, b1.reshape(1, hid), w2, b2.reshape(1, hid), w3, b3.reshape(1, hid), freq.reshape(1, hid),
      fo_dir, deltas)


def _fft_matrices(r):
    n = r * r
    idx = jnp.arange(r, dtype=jnp.int32)

    def cis(m, period):
        ang = (2.0 * math.pi / period) * m.astype(F32)
        return jnp.cos(ang), -jnp.sin(ang)

    fr, fi = cis((idx[:, None] * idx[None, :]) % r, r)
    half = r // 2
    f1_data = jnp.block([[fr[:, :half], -fi[:, :half]], [fi[:, :half], fr[:, :half]]])
    f1_real = jnp.concatenate([fr, fi], axis=0)
    f3 = jnp.block([[fr[:half, :], fi[:half, :]], [-fi[:half, :], fr[:half, :]]]) * (1.0 / n)
    klo = idx[:, None, None]
    khi = idx[None, :, None]
    bb = idx[None, None, :]
    gr, gi = cis((bb * klo + r * bb * khi) % n, n)
    g = jnp.concatenate([jnp.concatenate([gr, -gi], axis=2), jnp.concatenate([gi, gr], axis=2)], axis=1)
    gh = jnp.swapaxes(g, 1, 2)
    return f1_data.astype(BF16), f1_real.astype(BF16), f3.astype(BF16), g.astype(BF16), gh.astype(BF16)


def _fft_stage1_kernel(f_ref, x_ref, o_ref):
    x = x_ref[...]
    x = x.reshape(x.shape[0] * x.shape[1], x.shape[2])
    o_ref[...] = jnp.dot(f_ref[...], x.astype(BF16), preferred_element_type=F32).astype(o_ref.dtype)


def _fft_stage1(f1, xv, sel, rows_per_item, tl):
    r2, r = f1.shape
    items, a, lanes = xv.shape[-3:]
    groups = items // rows_per_item
    lead = xv.ndim - 3

    def x_map(p, j):
        return tuple(sel) + (p, 0, j)

    return pl.pallas_call(
        _fft_stage1_kernel,
        grid=(groups, lanes // tl),
        in_specs=[_resident((r2, r), lambda p, j: (0, 0)),
                  pl.BlockSpec((None,) * lead + (rows_per_item, a, tl), x_map)],
        out_specs=pl.BlockSpec((None, r2, tl), lambda p, j: (p, 0, j)),
        out_shape=jax.ShapeDtypeStruct((groups, r2, lanes), BF16),
        compiler_params=_params("parallel", "parallel"),
        name="fft_stage1",
    )(f1, xv)


def _fft_spectrum_kernel(g_ref, p_ref, o_ref):
    p = p_ref[...]
    o_ref[...] = jnp.dot(g_ref[...], p.reshape(p.shape[0] * p.shape[1], p.shape[2]), preferred_element_type=F32)


def _fft_spectrum(g, pk):
    orders, _, r, _, d = pk.shape
    return pl.pallas_call(
        _fft_spectrum_kernel,
        grid=(r, orders),
        in_specs=[pl.BlockSpec((None, 2 * r, 2 * r), lambda kk, n: (kk, 0, 0)),
                  pl.BlockSpec((None, 2, None, r, d), lambda kk, n: (n, 0, kk, 0, 0))],
        out_specs=pl.BlockSpec((None, None, 2 * r, d), lambda kk, n: (n, kk, 0, 0)),
        out_shape=jax.ShapeDtypeStruct((orders, r, 2 * r, d), F32),
        compiler_params=_params("parallel", "parallel"),
        name="fft_spectrum",
    )(g, pk)


def _fft_stage2_kernel(g_ref, gh_ref, h_ref, p_ref, o_ref):
    p = p_ref[...]
    r = p.shape[1]
    x = jnp.dot(g_ref[...], p.reshape(2 * r, p.shape[2]), preferred_element_type=F32)
    xr, xi = x[:r], x[r:]
    hr, hi = h_ref[0:r, :], h_ref[r:2 * r, :]
    y = jnp.concatenate([xr * hr - xi * hi, xr * hi + xi * hr], axis=0)
    q = jnp.dot(gh_ref[...], y.astype(BF16), preferred_element_type=F32)
    o_ref[...] = q.reshape(o_ref.shape).astype(o_ref.dtype)


def _fft_stage2(g, gh, hspec, order, p5):
    pairs, _, r, _, d = p5.shape
    return pl.pallas_call(
        _fft_stage2_kernel,
        grid=(r, pairs),
        in_specs=[pl.BlockSpec((None, 2 * r, 2 * r), lambda kk, p: (kk, 0, 0)),
                  pl.BlockSpec((None, 2 * r, 2 * r), lambda kk, p: (kk, 0, 0)),
                  pl.BlockSpec((None, None, 2 * r, d), lambda kk, p: (order, kk, 0, 0)),
                  pl.BlockSpec((None, 2, None, r, d), lambda kk, p: (p, 0, kk, 0, 0))],
        out_specs=pl.BlockSpec((None, 2, None, r, d), lambda kk, p: (p, 0, kk, 0, 0)),
        out_shape=jax.ShapeDtypeStruct(p5.shape, BF16),
        compiler_params=_params("parallel", "parallel"),
        name="fft_stage2",
    )(g, gh, hspec, p5)


def _fft_stage3_kernel(f_ref, q_ref, z_ref, gate_ref, skip_ref, o_ref):
    q = q_ref[...]
    y = jnp.dot(f_ref[...], q.reshape(q.shape[0] * q.shape[1], q.shape[2]), preferred_element_type=F32)
    y = y.reshape(o_ref.shape)
    o_ref[...] = gate_ref[...] * (y + z_ref[...] * skip_ref[...])


def _fft_stage3(f3, q4, zv, z_sel, gv, g_sel, skip_t, tl):
    r, r2 = f3.shape
    pairs = q4.shape[0]
    half, lanes = zv.shape[-2:]

    def sel_map(arr, sel):
        lead = arr.ndim - 3
        return pl.BlockSpec((None,) * lead + (2, half, tl), lambda p, j: tuple(sel) + (p, 0, j))

    return pl.pallas_call(
        _fft_stage3_kernel,
        grid=(pairs, lanes // tl),
        in_specs=[_resident((r, r2), lambda p, j: (0, 0)),
                  pl.BlockSpec((None, 2, r, tl), lambda p, j: (p, 0, 0, j)),
                  sel_map(zv, z_sel), sel_map(gv, g_sel),
                  _resident((1, tl), lambda p, j: (0, 0))],
        out_specs=pl.BlockSpec((2, half, tl), lambda p, j: (p, 0, j)),
        out_shape=jax.ShapeDtypeStruct((2 * pairs, half, lanes), F32),
        compiler_params=_params("parallel", "parallel"),
        name="fft_stage3",
    )(f3, q4, zv, gv, skip_t)


def _resid_proj_kernel(z_ref, w_ref, h_ref, g_ref, o_ref):
    m = jnp.dot(z_ref[...].astype(BF16), w_ref[...], preferred_element_type=F32)
    o_ref[...] = h_ref[...] + g_ref[...] * m


def _resid_proj(z, w, h, g1, tm):
    b, l, d = h.shape
    k = z.shape[-1]
    return pl.pallas_call(
        _resid_proj_kernel,
        grid=(b, l // tm),
        in_specs=[pl.BlockSpec((None, tm, k), lambda bi, i: (bi, i, 0)),
                  _resident((k, d), lambda bi, i: (0, 0)),
                  pl.BlockSpec((None, tm, d), lambda bi, i: (bi, i, 0)),
                  pl.BlockSpec((None, 1, d), lambda bi, i: (bi, 0, 0))],
        out_specs=pl.BlockSpec((None, tm, d), lambda bi, i: (bi, i, 0)),
        out_shape=jax.ShapeDtypeStruct((b, l, d), F32),
        compiler_params=_params("parallel", "parallel"),
        name="resid_proj",
    )(z, w, h, g1)


def _regroup_ab_weights(w_in):
    d = w_in.shape[0]
    sizes = (GDN_QK, GDN_QK, GDN_V, GDN_V, 2 * GDN_HEADS, 2 * GDN_HEADS, GLA_QK, GLA_QK, GLA_V, GLA_V,
             2 * GLA_GATE_RANK)
    offs = [0]
    for s in sizes:
        offs.append(offs[-1] + s)
    gq, gk, gv, gz, ga, gb, lq, lk, lv, lg, llr = (w_in[:, offs[i]:offs[i + 1]] for i in range(len(sizes)))
    pad = jnp.zeros((d, LANE - (ga.shape[1] + gb.shape[1] + llr.shape[1])), w_in.dtype)
    return jnp.concatenate([gq, gk, gv, gz, lq, lk, lv, lg, ga, gb, llr, pad], axis=1)


def _lane_row(vals, offset):
    flat = vals.reshape(-1).astype(F32)
    return jnp.zeros((1, LANE), F32).at[0, offset:offset + flat.shape[0]].set(flat)


def _even_layer(h, ctx, mods, norm1, norm2, w_in, conv_w, a_log, dt_bias, gdn_norm, gate_w, gate_b, gla_norm,
                w_out, w1, w3, w2, final_w, final_norm):
    b, l, d = h.shape
    lctx = ctx.shape[1]
    tm = 256
    sh1, sc1, g1, sh2, sc2, g2 = mods["lat"]
    csh1, csc1 = mods["ctx"][0], mods["ctx"][1]
    xe = jnp.concatenate([ctx, h], axis=1)
    shift = jnp.stack([jnp.broadcast_to(csh1, (b, d)), sh1], axis=1)[:, :, None, :]
    scale = jnp.stack([jnp.broadcast_to(csc1, (b, d)), sc1], axis=1)[:, :, None, :]
    proj = _norm_proj(xe, norm1, shift, scale, _regroup_ab_weights(w_in).astype(BF16), tm, lctx // tm)

    gw = jnp.zeros((LANE, 2 * GLA_QK), F32)
    for dd in range(2):
        gw = gw.at[SM_LR + dd * GLA_GATE_RANK:SM_LR + (dd + 1) * GLA_GATE_RANK,
                   dd * GLA_QK:(dd + 1) * GLA_QK].set(gate_w[dd])
    q, k, v, sm, loga = _ab_prep(proj, conv_w, _lane_row(a_log, SM_G), _lane_row(dt_bias, SM_G), gw,
                                 gate_b.reshape(1, 2 * GLA_QK), tm, (0, lctx), (lctx, lctx + l))
    nctx = lctx // CHUNK
    og_f, og_b = _gdn_scan(q, k, v, sm, nctx)
    ol_f, ol_b = _gla_scan(proj, loga, nctx)
    h = _ab_merge(og_f, og_b, ol_f, ol_b, proj, gdn_norm.reshape(1, GDN_DV), gla_norm.reshape(1, GLA_DV),
                  w_out.astype(BF16), h, g1[:, None, :], tm, lctx // tm)
    return _ffn(h, norm2, sh2[:, None, :], sc2[:, None, :], g2[:, None, :], w1.astype(BF16), w3.astype(BF16),
                w2.astype(BF16), final_w, tm, final_norm)


def _hyena_layer(h, mods, norm1, norm2, w_in, conv_w, pw1, pb1, pw2, pb2, pw3, pb3, freq, filt_out, skip, w_out,
                 w1, w3, w2, final_w, final_norm):
    b, l, d = h.shape
    tm = 256
    sh1, sc1, g1, sh2, sc2, g2 = mods["lat"]
    u = _norm_proj(h, norm1, sh1[:, None, None, :], sc1[:, None, None, :], w_in.astype(BF16), tm, 0)
    u3 = _hy_conv(u, conv_w, d, 512)

    n2 = 2 * l
    r = math.isqrt(n2)
    assert r * r == n2 and b % 2 == 0
    rows = jnp.arange(n2, dtype=jnp.int32)
    pos = jnp.where(rows < l, rows, n2 - rows) % l
    tlin = jnp.linspace(0.0, 1.0, l, dtype=F32)
    wang = (2.0 * math.pi / l) * jnp.arange(l, dtype=F32)
    bands = jnp.linspace(1e-4, HY_BANDS - 1, HY_BANDS, dtype=F32)[None, :]
    zfeat = jnp.concatenate([tlin[:, None], jnp.cos(bands * wang[:, None]), -jnp.sin(bands * wang[:, None])], axis=-1)
    zext = jnp.pad(zfeat[pos], ((0, 0), (0, LANE - HY_EMB)))
    pw1 = jnp.pad(pw1, ((0, LANE - HY_EMB), (0, 0)))
    orders = skip.shape[0]
    fo_dir = jnp.moveaxis(filt_out.reshape(filt_out.shape[0], 2, orders * d), 1, 0)
    deltas = jnp.abs(jnp.linspace(HY_MIN_DECAY, HY_MAX_DECAY, d, dtype=F32))
    kcirc = _hy_filters(zext, pw1, pb1, pw2, pb2, pw3, pb3, freq, fo_dir,
                        jnp.tile(deltas, orders)[None, :], l, d, 512)

    f1_data, f1_real, f3, g, gh = _fft_matrices(r)
    tl = 4096
    pk = _fft_stage1(f1_real, kcirc.reshape(orders, r, r * d), (), 1, tl)
    hspec = _fft_spectrum(g, pk.reshape(orders, 2, r, r, d))

    u4 = u3.reshape(3, b, r // 2, r * d)
    z_arr, z_sel = u4, (0,)
    for n in range(orders):
        p = _fft_stage1(f1_data, z_arr, z_sel, 2, tl)
        q = _fft_stage2(g, gh, hspec, n, p.reshape(b // 2, 2, r, r, d))
        z_arr = _fft_stage3(f3, q.reshape(b // 2, 2, r, r * d), z_arr, z_sel, u4, (n + 1,),
                            jnp.tile(skip[n], tl // d)[None, :], tl)
        z_sel = ()
    h = _resid_proj(z_arr.reshape(b, l, d), w_out.astype(BF16), h, g1[:, None, :], tm)
    return _ffn(h, norm2, sh2[:, None, :], sc2[:, None, :], g2[:, None, :], w1.astype(BF16), w3.astype(BF16),
                w2.astype(BF16), final_w, tm, final_norm)


def kernel(x, c, ctx, c_ctx, mod_w, mod_b, norm1_w, norm2_w, ab_w_in, ab_conv_w, gdn_a_log, gdn_dt_bias, gdn_norm_w, gla_gate_w, gla_gate_b, gla_norm_w, ab_w_out, hy_w_in, hy_conv_w, hy_pos_w1, hy_pos_b1, hy_pos_w2, hy_pos_b2, hy_pos_w3, hy_pos_b3, hy_freq, hy_filt_out, hy_skip, hy_w_out, ffn_w1, ffn_w3, ffn_w2, final_norm_w):
    b, l, d = x.shape
    depth = mod_w.shape[0]
    assert b + 1 <= SUBLANE
    cvec = jnp.concatenate([c, c_ctx[None, :], jnp.zeros((SUBLANE - b - 1, d), F32)], axis=0)
    mod = _modulation(cvec, mod_w, mod_b)
    h = x
    for i in range(depth):
        mods = {"lat": [mod[i, :b, j * d:(j + 1) * d] for j in range(6)],
                "ctx": [mod[i, b, j * d:(j + 1) * d] for j in range(6)]}
        last = i == depth - 1
        if i % 2 == 0:
            assert not any(j % 2 == 0 for j in range(i + 1, depth))
            e = i // 2
            h = _even_layer(h, ctx, mods, norm1_w[i], norm2_w[i], ab_w_in[e], ab_conv_w[e], gdn_a_log[e],
                            gdn_dt_bias[e], gdn_norm_w[e], gla_gate_w[e], gla_gate_b[e], gla_norm_w[e], ab_w_out[e],
                            ffn_w1[i], ffn_w3[i], ffn_w2[i], final_norm_w, last)
        else:
            o = i // 2
            h = _hyena_layer(h, mods, norm1_w[i], norm2_w[i], hy_w_in[o], hy_conv_w[o], hy_pos_w1[o], hy_pos_b1[o],
                             hy_pos_w2[o], hy_pos_b2[o], hy_pos_w3[o], hy_pos_b3[o], hy_freq[o], hy_filt_out[o],
                             hy_skip[o], hy_w_out[o], ffn_w1[i], ffn_w3[i], ffn_w2[i], final_norm_w, last)
    return h
```

```python
import functools
import math

import jax
import jax.numpy as jnp
from jax import lax
from jax.experimental import pallas as pl
from jax.experimental.pallas import tpu as pltpu

F32 = jnp.float32
BF16 = jnp.bfloat16
HI = lax.Precision.HIGHEST

EPS = 1e-6
CHUNK = 64
GDN_HEADS, GDN_DK, GDN_DV = 4, 128, 128
GLA_HEADS, GLA_DK, GLA_DV = 4, 64, 128
GLA_GATE_RANK = 16
GLA_GATE_TAU = 16.0
GDN_QK = GDN_HEADS * GDN_DK
GDN_V = GDN_HEADS * GDN_DV
GLA_QK = GLA_HEADS * GLA_DK
GLA_V = GLA_HEADS * GLA_DV
HY_EMB = 33
HY_BANDS = (HY_EMB - 1) // 2
HY_MIN_DECAY = math.log(1e-2) / 1.5
HY_MAX_DECAY = math.log(1e-2) / 0.3
HY_SHIFT = 0.05

LANE = 128
SUBLANE = 8
VMEM_LIMIT_BYTES = 56 * 1024 * 1024

COL_GQ, COL_GK, COL_GV, COL_GZ = 0, 512, 1024, 1536
COL_LQ, COL_LK, COL_LV, COL_LG = 2048, 2304, 2560, 3072
COL_SMALL = 3584
AB_COLS = COL_SMALL + LANE
SM_G, SM_B, SM_LR = 0, 8, 16


def _params(*sem):
    return pltpu.CompilerParams(dimension_semantics=sem, vmem_limit_bytes=VMEM_LIMIT_BYTES)


def _resident(shape, index_map):
    return pl.BlockSpec(shape, index_map, pipeline_mode=pl.Buffered(1))


def _dot(a, b):
    return jnp.dot(a.astype(BF16), b.astype(BF16), preferred_element_type=F32)


def _dot_nt(a, b):
    return lax.dot_general(a.astype(BF16), b.astype(BF16), (((1,), (1,)), ((), ())),
                           preferred_element_type=F32)


def _dot_hi(a, b):
    return jnp.dot(a, b, preferred_element_type=F32, precision=HI)


def _sigmoid(x):
    return 1.0 / (1.0 + jnp.exp(-x))


def _silu(x):
    return x * _sigmoid(x)


def _softplus(x):
    return jnp.maximum(x, 0.0) + jnp.log(1.0 + jnp.exp(-jnp.abs(x)))


def _mod_kernel(c_ref, w_ref, b_ref, o_ref):
    o_ref[...] = _dot_hi(_silu(c_ref[...]), w_ref[...]) + b_ref[...]


def _modulation(cvec, mod_w, mod_b):
    depth, d, n = mod_w.shape
    tn = n // 4
    return pl.pallas_call(
        _mod_kernel,
        grid=(depth, n // tn),
        in_specs=[pl.BlockSpec((SUBLANE, d), lambda i, j: (0, 0)),
                  pl.BlockSpec((None, d, tn), lambda i, j: (i, 0, j)),
                  pl.BlockSpec((None, 1, tn), lambda i, j: (i, 0, j))],
        out_specs=pl.BlockSpec((None, SUBLANE, tn), lambda i, j: (i, 0, j)),
        out_shape=jax.ShapeDtypeStruct((depth, SUBLANE, n), F32),
        compiler_params=_params("parallel", "parallel"),
        name="modulation",
    )(cvec, mod_w, mod_b.reshape(depth, 1, n))


def _modulated(x, gain, shift, scale):
    ms = jnp.mean(x * x, axis=-1, keepdims=True)
    return x * lax.rsqrt(ms + EPS) * gain * (1.0 + scale) + shift


def _norm_proj_kernel(x_ref, g_ref, sh_ref, sc_ref, w_ref, o_ref):
    a = _modulated(x_ref[...], g_ref[...], sh_ref[...], sc_ref[...])
    o_ref[...] = jnp.dot(a.astype(BF16), w_ref[...], preferred_element_type=F32)


def _norm_proj(x, gain, shift, scale, w, tm, group_tiles):
    b, l, d = x.shape
    n = w.shape[1]
    ngroups = shift.shape[1]

    def mod_map(bi, i):
        return (bi, jnp.where(i < group_tiles, 0, ngroups - 1), 0, 0)

    return pl.pallas_call(
        _norm_proj_kernel,
        grid=(b, l // tm),
        in_specs=[pl.BlockSpec((None, tm, d), lambda bi, i: (bi, i, 0)),
                  _resident((1, d), lambda bi, i: (0, 0)),
                  pl.BlockSpec((None, None, 1, d), mod_map),
                  pl.BlockSpec((None, None, 1, d), mod_map),
                  _resident((d, n), lambda bi, i: (0, 0))],
        out_specs=pl.BlockSpec((None, tm, n), lambda bi, i: (bi, i, 0)),
        out_shape=jax.ShapeDtypeStruct((b, l, n), F32),
        compiler_params=_params("parallel", "parallel"),
        name="norm_proj",
    )(x, gain.reshape(1, d), shift, scale, w)


def _conv3(xm, prev_row, next_row, w, has_prev, has_next):
    t = xm.shape[0]
    rows = lax.broadcasted_iota(jnp.int32, xm.shape, 0)
    prev_row = jnp.where(has_prev, prev_row, 0.0)
    next_row = jnp.where(has_next, next_row, 0.0)
    xp = jnp.where(rows == 0, prev_row, pltpu.roll(xm, 1, 0))
    xn = jnp.where(rows == t - 1, next_row, pltpu.roll(xm, t - 1, 0))
    return xp * w[0:1, :] + xm * w[1:2, :] + xn * w[2:3, :]


def _ab_prep_kernel(xm_ref, xp_ref, xn_ref, sm_ref, cw_ref, alog_ref, dtb_ref, gw_ref, gb_ref,
                    q_ref, k_ref, v_ref, so_ref, la_ref, *, tile, seq_starts, seq_ends):
    row0 = pl.program_id(1) * tile
    has_prev = functools.reduce(jnp.logical_and, [row0 != s for s in seq_starts])
    has_next = functools.reduce(jnp.logical_and, [row0 + tile != e for e in seq_ends])
    y = _conv3(xm_ref[...], xp_ref[SUBLANE - 1:SUBLANE, :], xn_ref[0:1, :], cw_ref[...], has_prev, has_next)
    y = _silu(y)
    for h in range(GDN_HEADS):
        sl = slice(h * GDN_DK, (h + 1) * GDN_DK)
        qh = y[:, COL_GQ + h * GDN_DK:COL_GQ + (h + 1) * GDN_DK]
        kh = y[:, COL_GK + h * GDN_DK:COL_GK + (h + 1) * GDN_DK]
        q_ref[:, sl] = qh * lax.rsqrt(jnp.sum(qh * qh, axis=-1, keepdims=True) + EPS) * (GDN_DK ** -0.5)
        k_ref[:, sl] = kh * lax.rsqrt(jnp.sum(kh * kh, axis=-1, keepdims=True) + EPS)
    v_ref[...] = y[:, COL_GV:COL_GV + GDN_V]
    sm = sm_ref[...]
    g = -jnp.exp(alog_ref[...]) * _softplus(sm + dtb_ref[...])
    beta = _sigmoid(sm)
    lanes = lax.broadcasted_iota(jnp.int32, sm.shape, 1)
    so_ref[...] = jnp.where(lanes < SM_B, g, beta)
    gate = _dot(sm, gw_ref[...]) + gb_ref[...]
    la_ref[...] = (jnp.minimum(gate, 0.0) - jnp.log(1.0 + jnp.exp(-jnp.abs(gate)))) * (1.0 / GLA_GATE_TAU)


def _ab_prep(proj, conv_w, alog_row, dtb_row, gate_w, gate_b, tile, seq_starts, seq_ends):
    b, l, _ = proj.shape
    cw = GDN_QK * 2 + GDN_V
    nrow8 = l // SUBLANE
    kern = functools.partial(_ab_prep_kernel, tile=tile, seq_starts=seq_starts, seq_ends=seq_ends)
    sds = jax.ShapeDtypeStruct
    return pl.pallas_call(
        kern,
        grid=(b, l // tile),
        in_specs=[pl.BlockSpec((None, tile, cw), lambda bi, i: (bi, i, 0)),
                  pl.BlockSpec((None, SUBLANE, cw),
                               lambda bi, i: (bi, jnp.maximum(i * (tile // SUBLANE) - 1, 0), 0)),
                  pl.BlockSpec((None, SUBLANE, cw),
                               lambda bi, i: (bi, jnp.minimum((i + 1) * (tile // SUBLANE), nrow8 - 1), 0)),
                  pl.BlockSpec((None, tile, LANE), lambda bi, i: (bi, i, COL_SMALL // LANE)),
                  _resident((3, cw), lambda bi, i: (0, 0)),
                  _resident((1, LANE), lambda bi, i: (0, 0)),
                  _resident((1, LANE), lambda bi, i: (0, 0)),
                  _resident((LANE, 2 * GLA_QK), lambda bi, i: (0, 0)),
                  _resident((1, 2 * GLA_QK), lambda bi, i: (0, 0))],
        out_specs=[pl.BlockSpec((None, tile, GDN_QK), lambda bi, i: (bi, i, 0)),
                   pl.BlockSpec((None, tile, GDN_QK), lambda bi, i: (bi, i, 0)),
                   pl.BlockSpec((None, tile, GDN_V), lambda bi, i: (bi, i, 0)),
                   pl.BlockSpec((None, tile, LANE), lambda bi, i: (bi, i, 0)),
                   pl.BlockSpec((None, tile, 2 * GLA_QK), lambda bi, i: (bi, i, 0))],
        out_shape=[sds((b, l, GDN_QK), F32), sds((b, l, GDN_QK), F32), sds((b, l, GDN_V), F32),
                   sds((b, l, LANE), F32), sds((b, l, 2 * GLA_QK), F32)],
        compiler_params=_params("parallel", "parallel"),
        name="ab_prep",
    )(proj, proj, proj, proj, conv_w, alog_row, dtb_row, gate_w, gate_b)


def _ext_chunk(d, s, nctx, ntot):
    if d == 0:
        return s
    return jnp.where(s < nctx, nctx - 1 - s, ntot + nctx - 1 - s)


def _lat_chunk(d, s, nctx, ntot):
    nlat = ntot - nctx
    if d == 0:
        return jnp.maximum(s - nctx, 0)
    return jnp.minimum(ntot - 1 - s, nlat - 1)


def _tri_masks(d):
    ii = lax.broadcasted_iota(jnp.int32, (CHUNK, CHUNK), 0)
    jj = lax.broadcasted_iota(jnp.int32, (CHUNK, CHUNK), 1)
    if d == 0:
        return ii >= jj, ii > jj, ii == jj
    return ii <= jj, ii < jj, ii == jj


def _cumsum_rows(incl, x):
    m = incl.astype(BF16)
    x1 = x.astype(BF16)
    r1 = x - x1.astype(F32)
    x2 = r1.astype(BF16)
    x3 = (r1 - x2.astype(F32)).astype(BF16)
    dot = functools.partial(jnp.dot, preferred_element_type=F32)
    return dot(m, x1) + dot(m, x2) + dot(m, x3)


def _merge_masks(d):
    ii = lax.broadcasted_iota(jnp.int32, (CHUNK, CHUNK), 0)
    jj = lax.broadcasted_iota(jnp.int32, (CHUNK, CHUNK), 1)
    lo, hi = (jj, ii) if d == 0 else (ii, jj)
    masks = {}
    s = 1
    while s < CHUNK:
        masks[s] = jnp.logical_and(ii // (2 * s) == jj // (2 * s),
                                   jnp.logical_and((hi // s) % 2 == 1, (lo // s) % 2 == 0))
        s *= 2
    return masks


def _gdn_pre_kernel(q_ref, k_ref, v_ref, sm_ref, u0_ref, cq_ref, akd_ref, gl_ref, *, cpb):
    tri = [_tri_masks(d) for d in range(2)]
    merge = [_merge_masks(d) for d in range(2)]
    insts = [(d, h) for d in range(2) for h in range(GDN_HEADS)]
    for c in range(cpb):
        rows = slice(c * CHUNK, (c + 1) * CHUNK)
        q = q_ref[rows, :]
        k = k_ref[rows, :]
        v = v_ref[rows, :]
        sm = sm_ref[rows, :]
        pref = _cumsum_rows(tri[0][0], sm)
        tot = jnp.sum(sm, axis=0, keepdims=True)
        lanes = lax.broadcasted_iota(jnp.int32, sm.shape, 1)
        bwd_lane = jnp.logical_and(lanes >= SM_G + GDN_HEADS, lanes < SM_G + 2 * GDN_HEADS)
        gc = jnp.where(bwd_lane, tot - pref + sm, pref)
        gct = gc.T
        totc = jnp.sum(sm.T, axis=1, keepdims=True)
        gl_ref[c] = jnp.exp(jnp.broadcast_to(totc[0:SUBLANE, :], (SUBLANE, LANE)))
        egc = jnp.exp(gc)
        etg = jnp.exp(tot - gc)
        kh = [k[:, h * GDN_DK:(h + 1) * GDN_DK] for h in range(GDN_HEADS)]
        qh = [q[:, h * GDN_DK:(h + 1) * GDN_DK] for h in range(GDN_HEADS)]
        vh = [v[:, h * GDN_DV:(h + 1) * GDN_DV] for h in range(GDN_HEADS)]
        gq = [_dot_nt(jnp.concatenate([kh[h], qh[h]], axis=0), kh[h]) for h in range(GDN_HEADS)]
        a, t, beta, ecol = {}, {}, {}, {}
        for d, h in insts:
            incl, strict, diag = tri[d]
            col = SM_G + d * GDN_HEADS + h
            decay = jnp.exp(jnp.where(incl, gc[:, col:col + 1] - gct[col:col + 1, :], -jnp.inf))
            beta[d, h] = sm[:, SM_B + col:SM_B + col + 1]
            ecol[d, h] = egc[:, col:col + 1]
            a[d, h] = jnp.where(strict, gq[h][0:CHUNK] * decay, 0.0) * beta[d, h]
            akd_ref[c, d, h, 0:CHUNK, :] = (gq[h][CHUNK:2 * CHUNK] * decay).astype(BF16)
            akd_ref[c, d, h, CHUNK:CHUNK + GDN_DK, :] = (kh[h] * etg[:, col:col + 1]).T.astype(BF16)
            t[d, h] = diag.astype(F32) - jnp.where(merge[d][1], a[d, h], 0.0)
        s = 2
        while s < CHUNK:
            x = {i: _dot(jnp.where(merge[i[0]][s], a[i], 0.0), t[i]) for i in insts}
            t = {i: t[i] - _dot(t[i], x[i]) for i in insts}
            s *= 2
        for d, h in insts:
            rhs = jnp.concatenate([vh[h] * beta[d, h], kh[h] * (beta[d, h] * ecol[d, h])], axis=1)
            sol = _dot(t[d, h], rhs)
            u0_ref[c, d, h] = sol[:, 0:GDN_DV]
            cq_ref[c, d, h, 0:CHUNK, :] = sol[:, GDN_DV:GDN_DV + GDN_DK].astype(BF16)
            cq_ref[c, d, h, CHUNK:2 * CHUNK, :] = (qh[h] * ecol[d, h]).astype(BF16)


def _gdn_pre(q, k, v, sm, cpb):
    b, lext, _ = q.shape
    ntot = lext // CHUNK
    rows = cpb * CHUNK
    lead = (b, ntot, 2, GDN_HEADS)

    def tok(bi, i):
        return (bi, i, 0)

    def blk(*tail):
        return pl.BlockSpec((None, cpb, 2, GDN_HEADS) + tail, lambda bi, i: (bi, i, 0, 0, 0, 0))

    sds = jax.ShapeDtypeStruct
    return pl.pallas_call(
        functools.partial(_gdn_pre_kernel, cpb=cpb),
        grid=(b, ntot // cpb),
        in_specs=[pl.BlockSpec((None, rows, GDN_QK), tok), pl.BlockSpec((None, rows, GDN_QK), tok),
                  pl.BlockSpec((None, rows, GDN_V), tok), pl.BlockSpec((None, rows, LANE), tok)],
        out_specs=[blk(CHUNK, GDN_DV), blk(2 * CHUNK, GDN_DK), blk(CHUNK + GDN_DK, CHUNK),
                   pl.BlockSpec((None, cpb, SUBLANE, LANE), lambda bi, i: (bi, i, 0, 0))],
        out_shape=[sds(lead + (CHUNK, GDN_DV), F32), sds(lead + (2 * CHUNK, GDN_DK), BF16),
                   sds(lead + (CHUNK + GDN_DK, CHUNK), BF16), sds((b, ntot, SUBLANE, LANE), F32)],
        compiler_params=_params("parallel", "parallel"),
        name="gdn_pre",
    )(q, k, v, sm)


def _scan_kernel(u0f, cqf, akf, ggf, mf, qdf, lgf, u0b, cqb, akb, ggb, mb, qdb, lgb,
                 gof_ref, gob_ref, lof_ref, lob_ref, sg_ref, sl_ref, *, nctx):
    step = pl.program_id(1)

    @pl.when(step == 0)
    def _():
        sg_ref[...] = jnp.zeros(sg_ref.shape, F32)
        sl_ref[...] = jnp.zeros(sl_ref.shape, F32)

    refs = ((u0f, cqf, akf, ggf, mf, qdf, lgf, gof_ref, lof_ref), (u0b, cqb, akb, ggb, mb, qdb, lgb, gob_ref, lob_ref))
    insts = [(d, h) for d in range(2) for h in range(GDN_HEADS)]
    dot = functools.partial(jnp.dot, preferred_element_type=F32)
    s = {i: sg_ref[i[0], i[1]] for i in insts}
    sl = {i: sl_ref[i[0], i[1]] for i in insts}
    cqs = {(d, h): dot(refs[d][1][h], s[d, h].astype(BF16)) for d, h in insts}
    ol = {(d, h): dot(refs[d][5][h], sl[d, h].astype(BF16)) for d, h in insts}
    u = {(d, h): refs[d][0][h] - cqs[d, h][0:CHUNK] for d, h in insts}
    r = {(d, h): dot(refs[d][2][h], u[d, h].astype(BF16)) for d, h in insts}
    for d, h in insts:
        lane = d * GDN_HEADS + h
        sg_ref[d, h] = s[d, h] * refs[d][3][lane:lane + 1, :] + r[d, h][CHUNK:CHUNK + GDN_DK]
        sl_ref[d, h] = sl[d, h] * refs[d][6][:, lane:lane + 1] + refs[d][4][h]

    @pl.when(step >= nctx)
    def _():
        for d, h in insts:
            refs[d][7][:, h * GDN_DV:(h + 1) * GDN_DV] = cqs[d, h][CHUNK:2 * CHUNK] + r[d, h][0:CHUNK]
            refs[d][8][:, h * GLA_DV:(h + 1) * GLA_DV] = ol[d, h]


def _scan(u0, cq, akd, ggl, m, qd, lgl, nctx):
    b, ntot = u0.shape[:2]
    nlat = ntot - nctx

    def specs(d):
        def chunk(bi, s):
            return (bi, _ext_chunk(d, s, nctx, ntot))

        def big(*tail):
            return pl.BlockSpec((None, None, None, GDN_HEADS) + tail, lambda bi, s: chunk(bi, s) + (d, 0, 0, 0))

        def small(*tail):
            return pl.BlockSpec((None, None) + tail, lambda bi, s: chunk(bi, s) + (0, 0))

        return [big(CHUNK, GDN_DV), big(2 * CHUNK, GDN_DK), big(CHUNK + GDN_DK, CHUNK), small(SUBLANE, LANE),
                big(GLA_DK, GLA_DV), big(CHUNK, GLA_DK), small(GLA_DK, LANE)]

    def out_spec(d, width):
        return pl.BlockSpec((None, CHUNK, width), lambda bi, s: (bi, _lat_chunk(d, s, nctx, ntot), 0))

    sds = jax.ShapeDtypeStruct
    return pl.pallas_call(
        functools.partial(_scan_kernel, nctx=nctx),
        grid=(b, ntot),
        in_specs=specs(0) + specs(1),
        out_specs=[out_spec(0, GDN_V), out_spec(1, GDN_V), out_spec(0, GLA_V), out_spec(1, GLA_V)],
        out_shape=[sds((b, nlat * CHUNK, GDN_V), F32)] * 2 + [sds((b, nlat * CHUNK, GLA_V), F32)] * 2,
        scratch_shapes=[pltpu.VMEM((2, GDN_HEADS, GDN_DK, GDN_DV), F32),
                        pltpu.VMEM((2, GLA_HEADS, GLA_DK, GLA_DV), F32)],
        compiler_params=_params("parallel", "arbitrary"),
        name="scan",
    )(u0, cq, akd, ggl, m, qd, lgl, u0, cq, akd, ggl, m, qd, lgl)


def _gla_pre_kernel(q_ref, k_ref, v_ref, la_ref, m_ref, qd_ref, gl_ref, intra_ref, *, cpb):
    tri = [_tri_masks(d) for d in range(2)]
    for c in range(cpb):
        rows = slice(c * CHUNK, (c + 1) * CHUNK)
        q = q_ref[rows, :] * (GLA_DK ** -0.5)
        k = k_ref[rows, :]
        v = v_ref[rows, :]
        la = la_ref[rows, :]
        pref = _cumsum_rows(tri[0][0], la)
        tot = jnp.sum(la, axis=0, keepdims=True)
        lanes = lax.broadcasted_iota(jnp.int32, la.shape, 1)
        bwd_lane = lanes >= GLA_QK
        bc = jnp.where(bwd_lane, tot - pref + la, pref)
        mid = CHUNK // 2
        ref = jnp.where(bwd_lane[0:1, :], bc[CHUNK - 1 - mid:CHUNK - mid, :], bc[mid:mid + 1, :])
        q2 = jnp.concatenate([q, q], axis=1)
        k2 = jnp.concatenate([k, k], axis=1)
        qa = q2 * jnp.exp(bc - ref)
        ka = k2 * jnp.exp(ref - bc)
        qd = (q2 * jnp.exp(bc)).astype(BF16)
        kdt = (k2 * jnp.exp(tot - bc)).T
        glcol = jnp.exp(jnp.sum(la.T, axis=1, keepdims=True))
        out_lanes = lax.broadcasted_iota(jnp.int32, (GLA_DK, LANE), 1)
        gl = jnp.zeros((GLA_DK, LANE), F32)
        for h in range(GLA_HEADS):
            vh = v[:, h * GLA_DV:(h + 1) * GLA_DV]
            intra = None
            for d in range(2):
                cs = slice(d * GLA_QK + h * GLA_DK, d * GLA_QK + (h + 1) * GLA_DK)
                attn = jnp.where(tri[d][0], _dot_nt(qa[:, cs], ka[:, cs]), 0.0)
                im = _dot(jnp.concatenate([attn, kdt[cs, :]], axis=0), vh)
                intra = im[0:CHUNK] if intra is None else intra + im[0:CHUNK]
                m_ref[c, d, h] = im[CHUNK:CHUNK + GLA_DK]
                qd_ref[c, d, h] = qd[:, cs]
                gl = jnp.where(out_lanes == d * GLA_HEADS + h, glcol[cs, :], gl)
            intra_ref[rows, h * GLA_DV:(h + 1) * GLA_DV] = intra
        gl_ref[c] = gl


def _gla_pre(proj, loga, cpb):
    b, lext, _ = proj.shape
    ntot = lext // CHUNK
    rows = cpb * CHUNK
    lead = (b, ntot, 2, GLA_HEADS)

    def col(colblk):
        return lambda bi, i: (bi, i, colblk)

    def blk(*tail):
        return pl.BlockSpec((None, cpb, 2, GLA_HEADS) + tail, lambda bi, i: (bi, i, 0, 0, 0, 0))

    sds = jax.ShapeDtypeStruct
    return pl.pallas_call(
        functools.partial(_gla_pre_kernel, cpb=cpb),
        grid=(b, ntot // cpb),
        in_specs=[pl.BlockSpec((None, rows, GLA_QK), col(COL_LQ // GLA_QK)),
                  pl.BlockSpec((None, rows, GLA_QK), col(COL_LK // GLA_QK)),
                  pl.BlockSpec((None, rows, GLA_V), col(COL_LV // GLA_V)),
                  pl.BlockSpec((None, rows, 2 * GLA_QK), col(0))],
        out_specs=[blk(GLA_DK, GLA_DV), blk(CHUNK, GLA_DK),
                   pl.BlockSpec((None, cpb, GLA_DK, LANE), lambda bi, i: (bi, i, 0, 0)),
                   pl.BlockSpec((None, rows, GLA_V), col(0))],
        out_shape=[sds(lead + (GLA_DK, GLA_DV), F32), sds(lead + (CHUNK, GLA_DK), BF16),
                   sds((b, ntot, GLA_DK, LANE), F32), sds((b, lext, GLA_V), F32)],
        compiler_params=_params("parallel", "parallel"),
        name="gla_pre",
    )(proj, proj, proj, loga)


def _head_gate(o, gain, z, heads, dv):
    parts = []
    for h in range(heads):
        sl = slice(h * dv, (h + 1) * dv)
        oh = o[:, sl]
        parts.append(oh * lax.rsqrt(jnp.mean(oh * oh, axis=-1, keepdims=True) + EPS) * gain * _silu(z[:, sl]))
    return jnp.concatenate(parts, axis=-1)


def _ab_merge_kernel(gf_ref, gb_ref, lf_ref, lb_ref, li_ref, gz_ref, lg_ref, gn_ref, ln_ref, w_ref, h_ref, g1_ref,
                     o_ref):
    yg = _head_gate(gf_ref[...] + gb_ref[...], gn_ref[...], gz_ref[...], GDN_HEADS, GDN_DV)
    yl = _head_gate(lf_ref[...] + lb_ref[...] + li_ref[...], ln_ref[...], lg_ref[...], GLA_HEADS, GLA_DV)
    m = (jnp.dot(yg.astype(BF16), w_ref[0:GDN_V, :], preferred_element_type=F32)
         + jnp.dot(yl.astype(BF16), w_ref[GDN_V:GDN_V + GLA_V, :], preferred_element_type=F32))
    o_ref[...] = h_ref[...] + g1_ref[...] * m


def _ab_merge(og_f, og_b, ol_f, ol_b, ol_intra, proj, gdn_norm, gla_norm, w_out, h, g1, tm, ctx_tiles):
    b, l, d = h.shape

    def lat(bi, i):
        return (bi, i, 0)

    def ext(colblk):
        return lambda bi, i: (bi, i + ctx_tiles, colblk)

    return pl.pallas_call(
        _ab_merge_kernel,
        grid=(b, l // tm),
        in_specs=[pl.BlockSpec((None, tm, GDN_V), lat), pl.BlockSpec((None, tm, GDN_V), lat),
                  pl.BlockSpec((None, tm, GLA_V), lat), pl.BlockSpec((None, tm, GLA_V), lat),
                  pl.BlockSpec((None, tm, GLA_V), ext(0)),
                  pl.BlockSpec((None, tm, GDN_V), ext(COL_GZ // GDN_V)),
                  pl.BlockSpec((None, tm, GLA_V), ext(COL_LG // GLA_V)),
                  _resident((1, GDN_DV), lambda bi, i: (0, 0)),
                  _resident((1, GLA_DV), lambda bi, i: (0, 0)),
                  _resident((GDN_V + GLA_V, d), lambda bi, i: (0, 0)),
                  pl.BlockSpec((None, tm, d), lat),
                  pl.BlockSpec((None, 1, d), lambda bi, i: (bi, 0, 0))],
        out_specs=pl.BlockSpec((None, tm, d), lat),
        out_shape=jax.ShapeDtypeStruct((b, l, d), F32),
        compiler_params=_params("parallel", "parallel"),
        name="ab_merge",
    )(og_f, og_b, ol_f, ol_b, ol_intra, proj, proj, gdn_norm, gla_norm, w_out, h, g1)


def _ffn_kernel(h_ref, gn_ref, sh_ref, sc_ref, g2_ref, w1_ref, w3_ref, w2_ref, fn_ref, o_ref, *, nsplit, final_norm):
    h = h_ref[...]
    a = _modulated(h, gn_ref[...], sh_ref[...], sc_ref[...]).astype(BF16)
    dff = w1_ref.shape[1]
    fc = dff // nsplit
    acc = jnp.zeros(h.shape, F32)
    for j in range(nsplit):
        u1 = jnp.dot(a, w1_ref[:, j * fc:(j + 1) * fc], preferred_element_type=F32)
        u3 = jnp.dot(a, w3_ref[:, j * fc:(j + 1) * fc], preferred_element_type=F32)
        acc = acc + jnp.dot((_silu(u1) * u3).astype(BF16), w2_ref[j * fc:(j + 1) * fc, :],
                            preferred_element_type=F32)
    y = h + g2_ref[...] * acc
    if final_norm:
        y = y * lax.rsqrt(jnp.mean(y * y, axis=-1, keepdims=True) + EPS) * fn_ref[...]
    o_ref[...] = y


def _ffn(h, gain, shift, scale, g2, w1, w3, w2, final_w, tm, final_norm):
    b, l, d = h.shape
    dff = w1.shape[1]

    def vec(bi, i):
        return (bi, 0, 0)

    def const(bi, i):
        return (0, 0)

    return pl.pallas_call(
        functools.partial(_ffn_kernel, nsplit=2, final_norm=final_norm),
        grid=(b, l // tm),
        in_specs=[pl.BlockSpec((None, tm, d), lambda bi, i: (bi, i, 0)),
                  _resident((1, d), const),
                  pl.BlockSpec((None, 1, d), vec), pl.BlockSpec((None, 1, d), vec), pl.BlockSpec((None, 1, d), vec),
                  _resident((d, dff), const), _resident((d, dff), const), _resident((dff, d), const),
                  _resident((1, d), const)],
        out_specs=pl.BlockSpec((None, tm, d), lambda bi, i: (bi, i, 0)),
        out_shape=jax.ShapeDtypeStruct((b, l, d), F32),
        compiler_params=_params("parallel", "parallel"),
        name="ffn",
    )(h, gain.reshape(1, d), shift, scale, g2, w1, w3, w2, final_w.reshape(1, d))


def _hy_conv_kernel(xm_ref, xp_ref, xn_ref, w_ref, o_ref, *, tile, seq_len):
    row0 = pl.program_id(2) * tile
    o_ref[...] = _conv3(xm_ref[...], xp_ref[SUBLANE - 1:SUBLANE, :], xn_ref[0:1, :], w_ref[...],
                        row0 != 0, row0 + tile != seq_len)


def _hy_conv(u, conv_w, d, tile):
    b, l, n = u.shape
    groups = n // d
    nrow8 = l // SUBLANE
    return pl.pallas_call(
        functools.partial(_hy_conv_kernel, tile=tile, seq_len=l),
        grid=(groups, b, l // tile),
        in_specs=[pl.BlockSpec((None, tile, d), lambda j, bi, i: (bi, i, j)),
                  pl.BlockSpec((None, SUBLANE, d),
                               lambda j, bi, i: (bi, jnp.maximum(i * (tile // SUBLANE) - 1, 0), j)),
                  pl.BlockSpec((None, SUBLANE, d),
                               lambda j, bi, i: (bi, jnp.minimum((i + 1) * (tile // SUBLANE), nrow8 - 1), j)),
                  pl.BlockSpec((3, d), lambda j, bi, i: (0, j))],
        out_specs=pl.BlockSpec((None, None, tile, d), lambda j, bi, i: (j, bi, i, 0)),
        out_shape=jax.ShapeDtypeStruct((groups, b, l, d), F32),
        compiler_params=_params("parallel", "parallel", "parallel"),
        name="hy_conv",
    )(u, u, u, conv_w)


def _hy_filter_kernel(z_ref, w1_ref, b1_ref, w2_ref, b2_ref, w3_ref, b3_ref, fr_ref, fo_ref, dl_ref, o_ref,
                      *, tile, seq_len, d):
    z = z_ref[...]
    fr = fr_ref[...]
    hdn = jnp.sin(fr * (_dot_hi(z, w1_ref[...]) + b1_ref[...]))
    hdn = jnp.sin(fr * (_dot_hi(hdn, w2_ref[...]) + b2_ref[...]))
    hdn = jnp.sin(fr * (_dot_hi(hdn, w3_ref[...]) + b3_ref[...]))
    filt = _dot_hi(hdn, fo_ref[...])
    window = jnp.exp(-z[:, 0:1] * dl_ref[...]) + HY_SHIFT
    rows = pl.program_id(0) * tile + lax.broadcasted_iota(jnp.int32, (tile, 1), 0)
    out = jnp.where(rows != seq_len, filt * window, 0.0)
    for n in range(o_ref.shape[0]):
        o_ref[n] = out[:, n * d:(n + 1) * d]


def _hy_filters(zext, w1, b1, w2, b2, w3, b3, freq, fo_dir, deltas, seq_len, d, tile):
    n2 = zext.shape[0]
    hid = w1.shape[1]
    orders = fo_dir.shape[2] // d
    half_tiles = seq_len // tile

    def const(i):
        return (0, 0)

    return pl.pallas_call(
        functools.partial(_hy_filter_kernel, tile=tile, seq_len=seq_len, d=d),
        grid=(n2 // tile,),
        in_specs=[pl.BlockSpec((tile, zext.shape[1]), lambda i: (i, 0)),
                  pl.BlockSpec(w1.shape, const), pl.BlockSpec((1, hid), const),
                  pl.BlockSpec(w2.shape, const), pl.BlockSpec((1, hid), const),
                  pl.BlockSpec(w3.shape, const), pl.BlockSpec((1, hid), const),
                  pl.BlockSpec((1, hid), const),
                  pl.BlockSpec((None, hid, orders * d), lambda i: (jnp.where(i < half_tiles, 0, 1), 0, 0)),
                  pl.BlockSpec((1, orders * d), const)],
        out_specs=pl.BlockSpec((orders, tile, d), lambda i: (0, i, 0)),
        out_shape=jax.ShapeDtypeStruct((orders, n2, d), F32),
        compiler_params=_params("parallel"),
        name="hy_filters",
    )(zext, w1, b1.reshape(1, hid), w2, b2.reshape(1, hid), w3, b3.reshape(1, hid), freq.reshape(1, hid),
      fo_dir, deltas)


def _fft_matrices(r):
    n = r * r
    idx = jnp.arange(r, dtype=jnp.int32)

    def cis(m, period):
        ang = (2.0 * math.pi / period) * m.astype(F32)
        return jnp.cos(ang), -jnp.sin(ang)

    fr, fi = cis((idx[:, None] * idx[None, :]) % r, r)
    half = r // 2
    f1_data = jnp.block([[fr[:, :half], -fi[:, :half]], [fi[:, :half], fr[:, :half]]])
    f1_real = jnp.concatenate([fr, fi], axis=0)
    f3 = jnp.block([[fr[:half, :], fi[:half, :]], [-fi[:half, :], fr[:half, :]]]) * (1.0 / n)
    klo = idx[:, None, None]
    khi = idx[None, :, None]
    bb = idx[None, None, :]
    gr, gi = cis((bb * klo + r * bb * khi) % n, n)
    g = jnp.concatenate([jnp.concatenate([gr, -gi], axis=2), jnp.concatenate([gi, gr], axis=2)], axis=1)
    gh = jnp.swapaxes(g, 1, 2)
    return f1_data.astype(BF16), f1_real.astype(BF16), f3.astype(BF16), g.astype(BF16), gh.astype(BF16)


def _fft_stage1_kernel(f_ref, x_ref, o_ref):
    x = x_ref[...]
    x = x.reshape(x.shape[0] * x.shape[1], x.shape[2])
    o_ref[...] = jnp.dot(f_ref[...], x.astype(BF16), preferred_element_type=F32).astype(o_ref.dtype)


def _fft_stage1(f1, xv, sel, rows_per_item, tl):
    r2, r = f1.shape
    items, a, lanes = xv.shape[-3:]
    groups = items // rows_per_item
    lead = xv.ndim - 3

    def x_map(p, j):
        return tuple(sel) + (p, 0, j)

    return pl.pallas_call(
        _fft_stage1_kernel,
        grid=(groups, lanes // tl),
        in_specs=[_resident((r2, r), lambda p, j: (0, 0)),
                  pl.BlockSpec((None,) * lead + (rows_per_item, a, tl), x_map)],
        out_specs=pl.BlockSpec((None, r2, tl), lambda p, j: (p, 0, j)),
        out_shape=jax.ShapeDtypeStruct((groups, r2, lanes), BF16),
        compiler_params=_params("parallel", "parallel"),
        name="fft_stage1",
    )(f1, xv)


def _fft_spectrum_kernel(g_ref, p_ref, o_ref):
    p = p_ref[...]
    o_ref[...] = jnp.dot(g_ref[...], p.reshape(p.shape[0] * p.shape[1], p.shape[2]), preferred_element_type=F32)


def _fft_spectrum(g, pk):
    orders, _, r, _, d = pk.shape
    return pl.pallas_call(
        _fft_spectrum_kernel,
        grid=(r, orders),
        in_specs=[pl.BlockSpec((None, 2 * r, 2 * r), lambda kk, n: (kk, 0, 0)),
                  pl.BlockSpec((None, 2, None, r, d), lambda kk, n: (n, 0, kk, 0, 0))],
        out_specs=pl.BlockSpec((None, None, 2 * r, d), lambda kk, n: (n, kk, 0, 0)),
        out_shape=jax.ShapeDtypeStruct((orders, r, 2 * r, d), F32),
        compiler_params=_params("parallel", "parallel"),
        name="fft_spectrum",
    )(g, pk)


def _fft_stage2_kernel(g_ref, gh_ref, h_ref, p_ref, o_ref):
    p = p_ref[...]
    r = p.shape[1]
    x = jnp.dot(g_ref[...], p.reshape(2 * r, p.shape[2]), preferred_element_type=F32)
    xr, xi = x[:r], x[r:]
    hr, hi = h_ref[0:r, :], h_ref[r:2 * r, :]
    y = jnp.concatenate([xr * hr - xi * hi, xr * hi + xi * hr], axis=0)
    q = jnp.dot(gh_ref[...], y.astype(BF16), preferred_element_type=F32)
    o_ref[...] = q.reshape(o_ref.shape).astype(o_ref.dtype)


def _fft_stage2(g, gh, hspec, order, p5):
    pairs, _, r, _, d = p5.shape
    return pl.pallas_call(
        _fft_stage2_kernel,
        grid=(r, pairs),
        in_specs=[pl.BlockSpec((None, 2 * r, 2 * r), lambda kk, p: (kk, 0, 0)),
                  pl.BlockSpec((None, 2 * r, 2 * r), lambda kk, p: (kk, 0, 0)),
                  pl.BlockSpec((None, None, 2 * r, d), lambda kk, p: (order, kk, 0, 0)),
                  pl.BlockSpec((None, 2, None, r, d), lambda kk, p: (p, 0, kk, 0, 0))],
        out_specs=pl.BlockSpec((None, 2, None, r, d), lambda kk, p: (p, 0, kk, 0, 0)),
        out_shape=jax.ShapeDtypeStruct(p5.shape, BF16),
        compiler_params=_params("parallel", "parallel"),
        name="fft_stage2",
    )(g, gh, hspec, p5)


def _fft_stage3_kernel(f_ref, q_ref, z_ref, gate_ref, skip_ref, o_ref):
    q = q_ref[...]
    y = jnp.dot(f_ref[...], q.reshape(q.shape[0] * q.shape[1], q.shape[2]), preferred_element_type=F32)
    y = y.reshape(o_ref.shape)
    o_ref[...] = gate_ref[...] * (y + z_ref[...] * skip_ref[...])


def _fft_stage3(f3, q4, zv, z_sel, gv, g_sel, skip_t, tl):
    r, r2 = f3.shape
    pairs = q4.shape[0]
    half, lanes = zv.shape[-2:]

    def sel_map(arr, sel):
        lead = arr.ndim - 3
        return pl.BlockSpec((None,) * lead + (2, half, tl), lambda p, j: tuple(sel) + (p, 0, j))

    return pl.pallas_call(
        _fft_stage3_kernel,
        grid=(pairs, lanes // tl),
        in_specs=[_resident((r, r2), lambda p, j: (0, 0)),
                  pl.BlockSpec((None, 2, r, tl), lambda p, j: (p, 0, 0, j)),
                  sel_map(zv, z_sel), sel_map(gv, g_sel),
                  _resident((1, tl), lambda p, j: (0, 0))],
        out_specs=pl.BlockSpec((2, half, tl), lambda p, j: (p, 0, j)),
        out_shape=jax.ShapeDtypeStruct((2 * pairs, half, lanes), F32),
        compiler_params=_params("parallel", "parallel"),
        name="fft_stage3",
    )(f3, q4, zv, gv, skip_t)


def _resid_proj_kernel(z_ref, w_ref, h_ref, g_ref, o_ref):
    m = jnp.dot(z_ref[...].astype(BF16), w_ref[...], preferred_element_type=F32)
    o_ref[...] = h_ref[...] + g_ref[...] * m


def _resid_proj(z, w, h, g1, tm):
    b, l, d = h.shape
    k = z.shape[-1]
    return pl.pallas_call(
        _resid_proj_kernel,
        grid=(b, l // tm),
        in_specs=[pl.BlockSpec((None, tm, k), lambda bi, i: (bi, i, 0)),
                  _resident((k, d), lambda bi, i: (0, 0)),
                  pl.BlockSpec((None, tm, d), lambda bi, i: (bi, i, 0)),
                  pl.BlockSpec((None, 1, d), lambda bi, i: (bi, 0, 0))],
        out_specs=pl.BlockSpec((None, tm, d), lambda bi, i: (bi, i, 0)),
        out_shape=jax.ShapeDtypeStruct((b, l, d), F32),
        compiler_params=_params("parallel", "parallel"),
        name="resid_proj",
    )(z, w, h, g1)


def _regroup_ab_weights(w_in):
    d = w_in.shape[0]
    sizes = (GDN_QK, GDN_QK, GDN_V, GDN_V, 2 * GDN_HEADS, 2 * GDN_HEADS, GLA_QK, GLA_QK, GLA_V, GLA_V,
             2 * GLA_GATE_RANK)
    offs = [0]
    for s in sizes:
        offs.append(offs[-1] + s)
    gq, gk, gv, gz, ga, gb, lq, lk, lv, lg, llr = (w_in[:, offs[i]:offs[i + 1]] for i in range(len(sizes)))
    pad = jnp.zeros((d, LANE - (ga.shape[1] + gb.shape[1] + llr.shape[1])), w_in.dtype)
    return jnp.concatenate([gq, gk, gv, gz, lq, lk, lv, lg, ga, gb, llr, pad], axis=1)


def _lane_row(vals, offset):
    flat = vals.reshape(-1).astype(F32)
    return jnp.zeros((1, LANE), F32).at[0, offset:offset + flat.shape[0]].set(flat)


def _even_layer(h, ctx, mods, norm1, norm2, w_in, conv_w, a_log, dt_bias, gdn_norm, gate_w, gate_b, gla_norm,
                w_out, w1, w3, w2, final_w, final_norm):
    b, l, d = h.shape
    lctx = ctx.shape[1]
    tm = 256
    sh1, sc1, g1, sh2, sc2, g2 = mods["lat"]
    csh1, csc1 = mods["ctx"][0], mods["ctx"][1]
    xe = jnp.concatenate([ctx, h], axis=1)
    shift = jnp.stack([jnp.broadcast_to(csh1, (b, d)), sh1], axis=1)[:, :, None, :]
    scale = jnp.stack([jnp.broadcast_to(csc1, (b, d)), sc1], axis=1)[:, :, None, :]
    proj = _norm_proj(xe, norm1, shift, scale, _regroup_ab_weights(w_in).astype(BF16), tm, lctx // tm)

    gw = jnp.zeros((LANE, 2 * GLA_QK), F32)
    for dd in range(2):
        gw = gw.at[SM_LR + dd * GLA_GATE_RANK:SM_LR + (dd + 1) * GLA_GATE_RANK,
                   dd * GLA_QK:(dd + 1) * GLA_QK].set(gate_w[dd])
    q, k, v, sm, loga = _ab_prep(proj, conv_w, _lane_row(a_log, SM_G), _lane_row(dt_bias, SM_G), gw,
                                 gate_b.reshape(1, 2 * GLA_QK), tm, (0, lctx), (lctx, lctx + l))
    nctx = lctx // CHUNK
    u0, cq, akd, ggl = _gdn_pre(q, k, v, sm, 2)
    m, qd, lgl, ol_intra = _gla_pre(proj, loga, 2)
    og_f, og_b, ol_f, ol_b = _scan(u0, cq, akd, ggl, m, qd, lgl, nctx)
    h = _ab_merge(og_f, og_b, ol_f, ol_b, ol_intra, proj, gdn_norm.reshape(1, GDN_DV), gla_norm.reshape(1, GLA_DV),
                  w_out.astype(BF16), h, g1[:, None, :], tm, lctx // tm)
    return _ffn(h, norm2, sh2[:, None, :], sc2[:, None, :], g2[:, None, :], w1.astype(BF16), w3.astype(BF16),
                w2.astype(BF16), final_w, tm, final_norm)


def _hyena_layer(h, mods, norm1, norm2, w_in, conv_w, pw1, pb1, pw2, pb2, pw3, pb3, freq, filt_out, skip, w_out,
                 w1, w3, w2, final_w, final_norm):
    b, l, d = h.shape
    tm = 256
    sh1, sc1, g1, sh2, sc2, g2 = mods["lat"]
    u = _norm_proj(h, norm1, sh1[:, None, None, :], sc1[:, None, None, :], w_in.astype(BF16), tm, 0)
    u3 = _hy_conv(u, conv_w, d, 512)

    n2 = 2 * l
    r = math.isqrt(n2)
    assert r * r == n2 and b % 2 == 0
    rows = jnp.arange(n2, dtype=jnp.int32)
    pos = jnp.where(rows < l, rows, n2 - rows) % l
    tlin = jnp.linspace(0.0, 1.0, l, dtype=F32)
    wang = (2.0 * math.pi / l) * jnp.arange(l, dtype=F32)
    bands = jnp.linspace(1e-4, HY_BANDS - 1, HY_BANDS, dtype=F32)[None, :]
    zfeat = jnp.concatenate([tlin[:, None], jnp.cos(bands * wang[:, None]), -jnp.sin(bands * wang[:, None])], axis=-1)
    zext = jnp.pad(zfeat[pos], ((0, 0), (0, LANE - HY_EMB)))
    pw1 = jnp.pad(pw1, ((0, LANE - HY_EMB), (0, 0)))
    orders = skip.shape[0]
    fo_dir = jnp.moveaxis(filt_out.reshape(filt_out.shape[0], 2, orders * d), 1, 0)
    deltas = jnp.abs(jnp.linspace(HY_MIN_DECAY, HY_MAX_DECAY, d, dtype=F32))
    kcirc = _hy_filters(zext, pw1, pb1, pw2, pb2, pw3, pb3, freq, fo_dir,
                        jnp.tile(deltas, orders)[None, :], l, d, 512)

    f1_data, f1_real, f3, g, gh = _fft_matrices(r)
    tl = 4096
    pk = _fft_stage1(f1_real, kcirc.reshape(orders, r, r * d), (), 1, tl)
    hspec = _fft_spectrum(g, pk.reshape(orders, 2, r, r, d))

    u4 = u3.reshape(3, b, r // 2, r * d)
    z_arr, z_sel = u4, (0,)
    for n in range(orders):
        p = _fft_stage1(f1_data, z_arr, z_sel, 2, tl)
        q = _fft_stage2(g, gh, hspec, n, p.reshape(b // 2, 2, r, r, d))
        z_arr = _fft_stage3(f3, q.reshape(b // 2, 2, r, r * d), z_arr, z_sel, u4, (n + 1,),
                            jnp.tile(skip[n], tl // d)[None, :], tl)
        z_sel = ()
    h = _resid_proj(z_arr.reshape(b, l, d), w_out.astype(BF16), h, g1[:, None, :], tm)
    return _ffn(h, norm2, sh2[:, None, :], sc2[:, None, :], g2[:, None, :], w1.astype(BF16), w3.astype(BF16),
                w2.astype(BF16), final_w, tm, final_norm)


def kernel(x, c, ctx, c_ctx, mod_w, mod_b, norm1_w, norm2_w, ab_w_in, ab_conv_w, gdn_a_log, gdn_dt_bias, gdn_norm_w, gla_gate_w, gla_gate_b, gla_norm_w, ab_w_out, hy_w_in, hy_conv_w, hy_pos_w1, hy_pos_b1, hy_pos_w2, hy_pos_b2, hy_pos_w3, hy_pos_b3, hy_freq, hy_filt_out, hy_skip, hy_w_out, ffn_w1, ffn_w3, ffn_w2, final_norm_w):
    b, l, d = x.shape
    depth = mod_w.shape[0]
    assert b + 1 <= SUBLANE
    cvec = jnp.concatenate([c, c_ctx[None, :], jnp.zeros((SUBLANE - b - 1, d), F32)], axis=0)
    mod = _modulation(cvec, mod_w, mod_b)
    h = x
    for i in range(depth):
        mods = {"lat": [mod[i, :b, j * d:(j + 1) * d] for j in range(6)],
                "ctx": [mod[i, b, j * d:(j + 1) * d] for j in range(6)]}
        last = i == depth - 1
        if i % 2 == 0:
            assert not any(j % 2 == 0 for j in range(i + 1, depth))
            e = i // 2
            h = _even_layer(h, ctx, mods, norm1_w[i], norm2_w[i], ab_w_in[e], ab_conv_w[e], gdn_a_log[e],
                            gdn_dt_bias[e], gdn_norm_w[e], gla_gate_w[e], gla_gate_b[e], gla_norm_w[e], ab_w_out[e],
                            ffn_w1[i], ffn_w3[i], ffn_w2[i], final_norm_w, last)
        else:
            o = i // 2
            h = _hyena_layer(h, mods, norm1_w[i], norm2_w[i], hy_w_in[o], hy_conv_w[o], hy_pos_w1[o], hy_pos_b1[o],
                             hy_pos_w2[o], hy_pos_b2[o], hy_pos_w3[o], hy_pos_b3[o], hy_freq[o], hy_filt_out[o],
                             hy_skip[o], hy_w_out[o], ffn_w1[i], ffn_w3[i], ffn_w2[i], final_norm_w, last)
    return h
```

```python
import functools
import math

import jax
import jax.numpy as jnp
from jax import lax
from jax.experimental import pallas as pl
from jax.experimental.pallas import tpu as pltpu

F32 = jnp.float32
BF16 = jnp.bfloat16
HI = lax.Precision.HIGHEST

EPS = 1e-6
CHUNK = 64
GDN_HEADS, GDN_DK, GDN_DV = 4, 128, 128
GLA_HEADS, GLA_DK, GLA_DV = 4, 64, 128
GLA_GATE_RANK = 16
GLA_GATE_TAU = 16.0
GDN_QK = GDN_HEADS * GDN_DK
GDN_V = GDN_HEADS * GDN_DV
GLA_QK = GLA_HEADS * GLA_DK
GLA_V = GLA_HEADS * GLA_DV
HY_EMB = 33
HY_BANDS = (HY_EMB - 1) // 2
HY_MIN_DECAY = math.log(1e-2) / 1.5
HY_MAX_DECAY = math.log(1e-2) / 0.3
HY_SHIFT = 0.05

LANE = 128
SUBLANE = 8
VMEM_LIMIT_BYTES = 56 * 1024 * 1024

COL_GQ, COL_GK, COL_GV, COL_GZ = 0, 512, 1024, 1536
COL_LQ, COL_LK, COL_LV, COL_LG = 2048, 2304, 2560, 3072
COL_SMALL = 3584
AB_COLS = COL_SMALL + LANE
SM_G, SM_B, SM_LR = 0, 8, 16


def _params(*sem):
    return pltpu.CompilerParams(dimension_semantics=sem, vmem_limit_bytes=VMEM_LIMIT_BYTES)


def _resident(shape, index_map):
    return pl.BlockSpec(shape, index_map, pipeline_mode=pl.Buffered(1))


def _dot(a, b):
    return jnp.dot(a.astype(BF16), b.astype(BF16), preferred_element_type=F32)


def _dot_nt(a, b):
    return lax.dot_general(a.astype(BF16), b.astype(BF16), (((1,), (1,)), ((), ())),
                           preferred_element_type=F32)


def _dot_hi(a, b):
    return jnp.dot(a, b, preferred_element_type=F32, precision=HI)


def _sigmoid(x):
    return 1.0 / (1.0 + jnp.exp(-x))


def _silu(x):
    return x * _sigmoid(x)


def _softplus(x):
    return jnp.maximum(x, 0.0) + jnp.log(1.0 + jnp.exp(-jnp.abs(x)))


def _mod_kernel(c_ref, w_ref, b_ref, o_ref):
    o_ref[...] = _dot_hi(_silu(c_ref[...]), w_ref[...]) + b_ref[...]


def _modulation(cvec, mod_w, mod_b):
    depth, d, n = mod_w.shape
    tn = n // 4
    return pl.pallas_call(
        _mod_kernel,
        grid=(depth, n // tn),
        in_specs=[pl.BlockSpec((SUBLANE, d), lambda i, j: (0, 0)),
                  pl.BlockSpec((None, d, tn), lambda i, j: (i, 0, j)),
                  pl.BlockSpec((None, 1, tn), lambda i, j: (i, 0, j))],
        out_specs=pl.BlockSpec((None, SUBLANE, tn), lambda i, j: (i, 0, j)),
        out_shape=jax.ShapeDtypeStruct((depth, SUBLANE, n), F32),
        compiler_params=_params("parallel", "parallel"),
        name="modulation",
    )(cvec, mod_w, mod_b.reshape(depth, 1, n))


def _modulated(x, gain, shift, scale):
    ms = jnp.mean(x * x, axis=-1, keepdims=True)
    return x * lax.rsqrt(ms + EPS) * gain * (1.0 + scale) + shift


def _norm_proj_kernel(x_ref, g_ref, sh_ref, sc_ref, w_ref, o_ref):
    a = _modulated(x_ref[...], g_ref[...], sh_ref[...], sc_ref[...])
    o_ref[...] = jnp.dot(a.astype(BF16), w_ref[...], preferred_element_type=F32)


def _norm_proj(x, gain, shift, scale, w, tm, group_tiles):
    b, l, d = x.shape
    n = w.shape[1]
    ngroups = shift.shape[1]

    def mod_map(bi, i):
        return (bi, jnp.where(i < group_tiles, 0, ngroups - 1), 0, 0)

    return pl.pallas_call(
        _norm_proj_kernel,
        grid=(b, l // tm),
        in_specs=[pl.BlockSpec((None, tm, d), lambda bi, i: (bi, i, 0)),
                  _resident((1, d), lambda bi, i: (0, 0)),
                  pl.BlockSpec((None, None, 1, d), mod_map),
                  pl.BlockSpec((None, None, 1, d), mod_map),
                  _resident((d, n), lambda bi, i: (0, 0))],
        out_specs=pl.BlockSpec((None, tm, n), lambda bi, i: (bi, i, 0)),
        out_shape=jax.ShapeDtypeStruct((b, l, n), F32),
        compiler_params=_params("parallel", "parallel"),
        name="norm_proj",
    )(x, gain.reshape(1, d), shift, scale, w)


def _conv3(xm, prev_row, next_row, w, has_prev, has_next):
    t = xm.shape[0]
    rows = lax.broadcasted_iota(jnp.int32, xm.shape, 0)
    prev_row = jnp.where(has_prev, prev_row, 0.0)
    next_row = jnp.where(has_next, next_row, 0.0)
    xp = jnp.where(rows == 0, prev_row, pltpu.roll(xm, 1, 0))
    xn = jnp.where(rows == t - 1, next_row, pltpu.roll(xm, t - 1, 0))
    return xp * w[0:1, :] + xm * w[1:2, :] + xn * w[2:3, :]


def _ab_prep_kernel(xm_ref, xp_ref, xn_ref, sm_ref, cw_ref, alog_ref, dtb_ref, gw_ref, gb_ref,
                    q_ref, k_ref, v_ref, so_ref, la_ref, *, tile, seq_starts, seq_ends):
    row0 = pl.program_id(1) * tile
    has_prev = functools.reduce(jnp.logical_and, [row0 != s for s in seq_starts])
    has_next = functools.reduce(jnp.logical_and, [row0 + tile != e for e in seq_ends])
    y = _conv3(xm_ref[...], xp_ref[SUBLANE - 1:SUBLANE, :], xn_ref[0:1, :], cw_ref[...], has_prev, has_next)
    y = _silu(y)
    for h in range(GDN_HEADS):
        sl = slice(h * GDN_DK, (h + 1) * GDN_DK)
        qh = y[:, COL_GQ + h * GDN_DK:COL_GQ + (h + 1) * GDN_DK]
        kh = y[:, COL_GK + h * GDN_DK:COL_GK + (h + 1) * GDN_DK]
        q_ref[:, sl] = qh * lax.rsqrt(jnp.sum(qh * qh, axis=-1, keepdims=True) + EPS) * (GDN_DK ** -0.5)
        k_ref[:, sl] = kh * lax.rsqrt(jnp.sum(kh * kh, axis=-1, keepdims=True) + EPS)
    v_ref[...] = y[:, COL_GV:COL_GV + GDN_V]
    sm = sm_ref[...]
    g = -jnp.exp(alog_ref[...]) * _softplus(sm + dtb_ref[...])
    beta = _sigmoid(sm)
    lanes = lax.broadcasted_iota(jnp.int32, sm.shape, 1)
    so_ref[...] = jnp.where(lanes < SM_B, g, beta)
    gate = _dot(sm, gw_ref[...]) + gb_ref[...]
    la_ref[...] = (jnp.minimum(gate, 0.0) - jnp.log(1.0 + jnp.exp(-jnp.abs(gate)))) * (1.0 / GLA_GATE_TAU)


def _ab_prep(proj, conv_w, alog_row, dtb_row, gate_w, gate_b, tile, seq_starts, seq_ends):
    b, l, _ = proj.shape
    cw = GDN_QK * 2 + GDN_V
    nrow8 = l // SUBLANE
    kern = functools.partial(_ab_prep_kernel, tile=tile, seq_starts=seq_starts, seq_ends=seq_ends)
    sds = jax.ShapeDtypeStruct
    return pl.pallas_call(
        kern,
        grid=(b, l // tile),
        in_specs=[pl.BlockSpec((None, tile, cw), lambda bi, i: (bi, i, 0)),
                  pl.BlockSpec((None, SUBLANE, cw),
                               lambda bi, i: (bi, jnp.maximum(i * (tile // SUBLANE) - 1, 0), 0)),
                  pl.BlockSpec((None, SUBLANE, cw),
                               lambda bi, i: (bi, jnp.minimum((i + 1) * (tile // SUBLANE), nrow8 - 1), 0)),
                  pl.BlockSpec((None, tile, LANE), lambda bi, i: (bi, i, COL_SMALL // LANE)),
                  _resident((3, cw), lambda bi, i: (0, 0)),
                  _resident((1, LANE), lambda bi, i: (0, 0)),
                  _resident((1, LANE), lambda bi, i: (0, 0)),
                  _resident((LANE, 2 * GLA_QK), lambda bi, i: (0, 0)),
                  _resident((1, 2 * GLA_QK), lambda bi, i: (0, 0))],
        out_specs=[pl.BlockSpec((None, tile, GDN_QK), lambda bi, i: (bi, i, 0)),
                   pl.BlockSpec((None, tile, GDN_QK), lambda bi, i: (bi, i, 0)),
                   pl.BlockSpec((None, tile, GDN_V), lambda bi, i: (bi, i, 0)),
                   pl.BlockSpec((None, tile, LANE), lambda bi, i: (bi, i, 0)),
                   pl.BlockSpec((None, tile, 2 * GLA_QK), lambda bi, i: (bi, i, 0))],
        out_shape=[sds((b, l, GDN_QK), F32), sds((b, l, GDN_QK), F32), sds((b, l, GDN_V), F32),
                   sds((b, l, LANE), F32), sds((b, l, 2 * GLA_QK), F32)],
        compiler_params=_params("parallel", "parallel"),
        name="ab_prep",
    )(proj, proj, proj, proj, conv_w, alog_row, dtb_row, gate_w, gate_b)


def _ext_chunk(d, s, nctx, ntot):
    if d == 0:
        return s
    return jnp.where(s < nctx, nctx - 1 - s, ntot + nctx - 1 - s)


def _lat_chunk(d, s, nctx, ntot):
    nlat = ntot - nctx
    if d == 0:
        return jnp.maximum(s - nctx, 0)
    return jnp.minimum(ntot - 1 - s, nlat - 1)


def _tri_masks(d):
    ii = lax.broadcasted_iota(jnp.int32, (CHUNK, CHUNK), 0)
    jj = lax.broadcasted_iota(jnp.int32, (CHUNK, CHUNK), 1)
    if d == 0:
        return ii >= jj, ii > jj, ii == jj
    return ii <= jj, ii < jj, ii == jj


def _cumsum_rows(incl, x):
    m = incl.astype(BF16)
    x1 = x.astype(BF16)
    r1 = x - x1.astype(F32)
    x2 = r1.astype(BF16)
    x3 = (r1 - x2.astype(F32)).astype(BF16)
    dot = functools.partial(jnp.dot, preferred_element_type=F32)
    return dot(m, x1) + dot(m, x2) + dot(m, x3)


def _merge_masks(d):
    ii = lax.broadcasted_iota(jnp.int32, (CHUNK, CHUNK), 0)
    jj = lax.broadcasted_iota(jnp.int32, (CHUNK, CHUNK), 1)
    lo, hi = (jj, ii) if d == 0 else (ii, jj)
    masks = {}
    s = 1
    while s < CHUNK:
        masks[s] = jnp.logical_and(ii // (2 * s) == jj // (2 * s),
                                   jnp.logical_and((hi // s) % 2 == 1, (lo // s) % 2 == 0))
        s *= 2
    return masks


def _gdn_pre_kernel(q_ref, k_ref, v_ref, sm_ref, u0_ref, cq_ref, akd_ref, gl_ref, *, cpb):
    tri = [_tri_masks(d) for d in range(2)]
    merge = [_merge_masks(d) for d in range(2)]
    insts = [(d, h) for d in range(2) for h in range(GDN_HEADS)]
    for c in range(cpb):
        rows = slice(c * CHUNK, (c + 1) * CHUNK)
        q = q_ref[rows, :]
        k = k_ref[rows, :]
        v = v_ref[rows, :]
        sm = sm_ref[rows, :]
        pref = _cumsum_rows(tri[0][0], sm)
        tot = jnp.sum(sm, axis=0, keepdims=True)
        lanes = lax.broadcasted_iota(jnp.int32, sm.shape, 1)
        bwd_lane = jnp.logical_and(lanes >= SM_G + GDN_HEADS, lanes < SM_G + 2 * GDN_HEADS)
        gc = jnp.where(bwd_lane, tot - pref + sm, pref)
        gct = gc.T
        totc = jnp.sum(sm.T, axis=1, keepdims=True)
        gl_ref[c] = jnp.exp(jnp.broadcast_to(totc[0:SUBLANE, :], (SUBLANE, LANE)))
        egc = jnp.exp(gc)
        etg = jnp.exp(tot - gc)
        kh = [k[:, h * GDN_DK:(h + 1) * GDN_DK] for h in range(GDN_HEADS)]
        qh = [q[:, h * GDN_DK:(h + 1) * GDN_DK] for h in range(GDN_HEADS)]
        vh = [v[:, h * GDN_DV:(h + 1) * GDN_DV] for h in range(GDN_HEADS)]
        gq = [_dot_nt(jnp.concatenate([kh[h], qh[h]], axis=0), kh[h]) for h in range(GDN_HEADS)]
        a, t, beta, ecol = {}, {}, {}, {}
        for d, h in insts:
            incl, strict, diag = tri[d]
            col = SM_G + d * GDN_HEADS + h
            decay = jnp.exp(jnp.where(incl, gc[:, col:col + 1] - gct[col:col + 1, :], -jnp.inf))
            beta[d, h] = sm[:, SM_B + col:SM_B + col + 1]
            ecol[d, h] = egc[:, col:col + 1]
            a[d, h] = jnp.where(strict, gq[h][0:CHUNK] * decay, 0.0) * beta[d, h]
            akd_ref[c, d, h, 0:CHUNK, :] = (gq[h][CHUNK:2 * CHUNK] * decay).astype(BF16)
            akd_ref[c, d, h, CHUNK:CHUNK + GDN_DK, :] = (kh[h] * etg[:, col:col + 1]).T.astype(BF16)
            t[d, h] = diag.astype(F32) - jnp.where(merge[d][1], a[d, h], 0.0)
        s = 2
        while s < CHUNK:
            x = {i: _dot(jnp.where(merge[i[0]][s], a[i], 0.0), t[i]) for i in insts}
            t = {i: t[i] - _dot(t[i], x[i]) for i in insts}
            s *= 2
        for d, h in insts:
            rhs = jnp.concatenate([vh[h] * beta[d, h], kh[h] * (beta[d, h] * ecol[d, h])], axis=1)
            sol = _dot(t[d, h], rhs)
            u0_ref[c, d, h] = sol[:, 0:GDN_DV]
            cq_ref[c, d, h, 0:CHUNK, :] = sol[:, GDN_DV:GDN_DV + GDN_DK].astype(BF16)
            cq_ref[c, d, h, CHUNK:2 * CHUNK, :] = (qh[h] * ecol[d, h]).astype(BF16)


def _gdn_pre(q, k, v, sm, cpb):
    b, lext, _ = q.shape
    ntot = lext // CHUNK
    rows = cpb * CHUNK
    lead = (b, ntot, 2, GDN_HEADS)

    def tok(bi, i):
        return (bi, i, 0)

    def blk(*tail):
        return pl.BlockSpec((None, cpb, 2, GDN_HEADS) + tail, lambda bi, i: (bi, i, 0, 0, 0, 0))

    sds = jax.ShapeDtypeStruct
    return pl.pallas_call(
        functools.partial(_gdn_pre_kernel, cpb=cpb),
        grid=(b, ntot // cpb),
        in_specs=[pl.BlockSpec((None, rows, GDN_QK), tok), pl.BlockSpec((None, rows, GDN_QK), tok),
                  pl.BlockSpec((None, rows, GDN_V), tok), pl.BlockSpec((None, rows, LANE), tok)],
        out_specs=[blk(CHUNK, GDN_DV), blk(2 * CHUNK, GDN_DK), blk(CHUNK + GDN_DK, CHUNK),
                   pl.BlockSpec((None, cpb, SUBLANE, LANE), lambda bi, i: (bi, i, 0, 0))],
        out_shape=[sds(lead + (CHUNK, GDN_DV), F32), sds(lead + (2 * CHUNK, GDN_DK), BF16),
                   sds(lead + (CHUNK + GDN_DK, CHUNK), BF16), sds((b, ntot, SUBLANE, LANE), F32)],
        compiler_params=_params("parallel", "parallel"),
        name="gdn_pre",
    )(q, k, v, sm)


def _scan_kernel(u0f, cqf, akf, ggf, mf, qdf, lgf, u0b, cqb, akb, ggb, mb, qdb, lgb,
                 gof_ref, gob_ref, lof_ref, lob_ref, sg_ref, sl_ref, *, nctx):
    step = pl.program_id(1)

    @pl.when(step == 0)
    def _():
        sg_ref[...] = jnp.zeros(sg_ref.shape, F32)
        sl_ref[...] = jnp.zeros(sl_ref.shape, F32)

    refs = ((u0f, cqf, akf, ggf, mf, qdf, lgf, gof_ref, lof_ref), (u0b, cqb, akb, ggb, mb, qdb, lgb, gob_ref, lob_ref))
    insts = [(d, h) for d in range(2) for h in range(GDN_HEADS)]
    dot = functools.partial(jnp.dot, preferred_element_type=F32)
    cps = u0f.shape[0]
    s = {i: sg_ref[i[0], i[1]] for i in insts}
    sl = {i: sl_ref[i[0], i[1]] for i in insts}
    for ci in range(cps):
        cd = (ci, cps - 1 - ci)
        cqs = {(d, h): dot(refs[d][1][cd[d], h], s[d, h].astype(BF16)) for d, h in insts}
        ol = {(d, h): dot(refs[d][5][cd[d], h], sl[d, h].astype(BF16)) for d, h in insts}
        u = {(d, h): refs[d][0][cd[d], h] - cqs[d, h][0:CHUNK] for d, h in insts}
        r = {(d, h): dot(refs[d][2][cd[d], h], u[d, h].astype(BF16)) for d, h in insts}
        for d, h in insts:
            lane = d * GDN_HEADS + h
            s[d, h] = s[d, h] * refs[d][3][cd[d], lane:lane + 1, :] + r[d, h][CHUNK:CHUNK + GDN_DK]
            sl[d, h] = sl[d, h] * refs[d][6][cd[d], :, lane:lane + 1] + refs[d][4][cd[d], h]

        @pl.when(step >= nctx)
        def _():
            for d, h in insts:
                rows = slice(cd[d] * CHUNK, (cd[d] + 1) * CHUNK)
                refs[d][7][rows, h * GDN_DV:(h + 1) * GDN_DV] = cqs[d, h][CHUNK:2 * CHUNK] + r[d, h][0:CHUNK]
                refs[d][8][rows, h * GLA_DV:(h + 1) * GLA_DV] = ol[d, h]

    for d, h in insts:
        sg_ref[d, h] = s[d, h]
        sl_ref[d, h] = sl[d, h]


def _scan(u0, cq, akd, ggl, m, qd, lgl, nctx, cps):
    b, nchunks = u0.shape[:2]
    assert nctx % cps == 0 and nchunks % cps == 0
    nlat = nchunks - nctx
    ntot, nctx = nchunks // cps, nctx // cps

    def specs(d):
        def chunk(bi, s):
            return (bi, _ext_chunk(d, s, nctx, ntot))

        def big(*tail):
            return pl.BlockSpec((None, cps, None, GDN_HEADS) + tail, lambda bi, s: chunk(bi, s) + (d, 0, 0, 0))

        def small(*tail):
            return pl.BlockSpec((None, cps) + tail, lambda bi, s: chunk(bi, s) + (0, 0))

        return [big(CHUNK, GDN_DV), big(2 * CHUNK, GDN_DK), big(CHUNK + GDN_DK, CHUNK), small(SUBLANE, LANE),
                big(GLA_DK, GLA_DV), big(CHUNK, GLA_DK), small(GLA_DK, LANE)]

    def out_spec(d, width):
        return pl.BlockSpec((None, cps * CHUNK, width), lambda bi, s: (bi, _lat_chunk(d, s, nctx, ntot), 0))

    sds = jax.ShapeDtypeStruct
    return pl.pallas_call(
        functools.partial(_scan_kernel, nctx=nctx),
        grid=(b, ntot),
        in_specs=specs(0) + specs(1),
        out_specs=[out_spec(0, GDN_V), out_spec(1, GDN_V), out_spec(0, GLA_V), out_spec(1, GLA_V)],
        out_shape=[sds((b, nlat * CHUNK, GDN_V), F32)] * 2 + [sds((b, nlat * CHUNK, GLA_V), F32)] * 2,
        scratch_shapes=[pltpu.VMEM((2, GDN_HEADS, GDN_DK, GDN_DV), F32),
                        pltpu.VMEM((2, GLA_HEADS, GLA_DK, GLA_DV), F32)],
        compiler_params=_params("parallel", "arbitrary"),
        name="scan",
    )(u0, cq, akd, ggl, m, qd, lgl, u0, cq, akd, ggl, m, qd, lgl)


def _gla_pre_kernel(q_ref, k_ref, v_ref, la_ref, m_ref, qd_ref, gl_ref, intra_ref, *, cpb):
    tri = [_tri_masks(d) for d in range(2)]
    for c in range(cpb):
        rows = slice(c * CHUNK, (c + 1) * CHUNK)
        q = q_ref[rows, :] * (GLA_DK ** -0.5)
        k = k_ref[rows, :]
        v = v_ref[rows, :]
        la = la_ref[rows, :]
        pref = _cumsum_rows(tri[0][0], la)
        tot = jnp.sum(la, axis=0, keepdims=True)
        lanes = lax.broadcasted_iota(jnp.int32, la.shape, 1)
        bwd_lane = lanes >= GLA_QK
        bc = jnp.where(bwd_lane, tot - pref + la, pref)
        mid = CHUNK // 2
        ref = jnp.where(bwd_lane[0:1, :], bc[CHUNK - 1 - mid:CHUNK - mid, :], bc[mid:mid + 1, :])
        q2 = jnp.concatenate([q, q], axis=1)
        k2 = jnp.concatenate([k, k], axis=1)
        qa = q2 * jnp.exp(bc - ref)
        ka = k2 * jnp.exp(ref - bc)
        qd = (q2 * jnp.exp(bc)).astype(BF16)
        kdt = (k2 * jnp.exp(tot - bc)).T
        glcol = jnp.exp(jnp.sum(la.T, axis=1, keepdims=True))
        out_lanes = lax.broadcasted_iota(jnp.int32, (GLA_DK, LANE), 1)
        gl = jnp.zeros((GLA_DK, LANE), F32)
        for h in range(GLA_HEADS):
            vh = v[:, h * GLA_DV:(h + 1) * GLA_DV]
            intra = None
            for d in range(2):
                cs = slice(d * GLA_QK + h * GLA_DK, d * GLA_QK + (h + 1) * GLA_DK)
                attn = jnp.where(tri[d][0], _dot_nt(qa[:, cs], ka[:, cs]), 0.0)
                im = _dot(jnp.concatenate([attn, kdt[cs, :]], axis=0), vh)
                intra = im[0:CHUNK] if intra is None else intra + im[0:CHUNK]
                m_ref[c, d, h] = im[CHUNK:CHUNK + GLA_DK]
                qd_ref[c, d, h] = qd[:, cs]
                gl = jnp.where(out_lanes == d * GLA_HEADS + h, glcol[cs, :], gl)
            intra_ref[rows, h * GLA_DV:(h + 1) * GLA_DV] = intra
        gl_ref[c] = gl


def _gla_pre(proj, loga, cpb):
    b, lext, _ = proj.shape
    ntot = lext // CHUNK
    rows = cpb * CHUNK
    lead = (b, ntot, 2, GLA_HEADS)

    def col(colblk):
        return lambda bi, i: (bi, i, colblk)

    def blk(*tail):
        return pl.BlockSpec((None, cpb, 2, GLA_HEADS) + tail, lambda bi, i: (bi, i, 0, 0, 0, 0))

    sds = jax.ShapeDtypeStruct
    return pl.pallas_call(
        functools.partial(_gla_pre_kernel, cpb=cpb),
        grid=(b, ntot // cpb),
        in_specs=[pl.BlockSpec((None, rows, GLA_QK), col(COL_LQ // GLA_QK)),
                  pl.BlockSpec((None, rows, GLA_QK), col(COL_LK // GLA_QK)),
                  pl.BlockSpec((None, rows, GLA_V), col(COL_LV // GLA_V)),
                  pl.BlockSpec((None, rows, 2 * GLA_QK), col(0))],
        out_specs=[blk(GLA_DK, GLA_DV), blk(CHUNK, GLA_DK),
                   pl.BlockSpec((None, cpb, GLA_DK, LANE), lambda bi, i: (bi, i, 0, 0)),
                   pl.BlockSpec((None, rows, GLA_V), col(0))],
        out_shape=[sds(lead + (GLA_DK, GLA_DV), F32), sds(lead + (CHUNK, GLA_DK), BF16),
                   sds((b, ntot, GLA_DK, LANE), F32), sds((b, lext, GLA_V), F32)],
        compiler_params=_params("parallel", "parallel"),
        name="gla_pre",
    )(proj, proj, proj, loga)


def _head_gate(o, gain, z, heads, dv):
    parts = []
    for h in range(heads):
        sl = slice(h * dv, (h + 1) * dv)
        oh = o[:, sl]
        parts.append(oh * lax.rsqrt(jnp.mean(oh * oh, axis=-1, keepdims=True) + EPS) * gain * _silu(z[:, sl]))
    return jnp.concatenate(parts, axis=-1)


def _ab_merge_kernel(gf_ref, gb_ref, lf_ref, lb_ref, li_ref, gz_ref, lg_ref, gn_ref, ln_ref, w_ref, h_ref, g1_ref,
                     o_ref):
    yg = _head_gate(gf_ref[...] + gb_ref[...], gn_ref[...], gz_ref[...], GDN_HEADS, GDN_DV)
    yl = _head_gate(lf_ref[...] + lb_ref[...] + li_ref[...], ln_ref[...], lg_ref[...], GLA_HEADS, GLA_DV)
    m = (jnp.dot(yg.astype(BF16), w_ref[0:GDN_V, :], preferred_element_type=F32)
         + jnp.dot(yl.astype(BF16), w_ref[GDN_V:GDN_V + GLA_V, :], preferred_element_type=F32))
    o_ref[...] = h_ref[...] + g1_ref[...] * m


def _ab_merge(og_f, og_b, ol_f, ol_b, ol_intra, proj, gdn_norm, gla_norm, w_out, h, g1, tm, ctx_tiles):
    b, l, d = h.shape

    def lat(bi, i):
        return (bi, i, 0)

    def ext(colblk):
        return lambda bi, i: (bi, i + ctx_tiles, colblk)

    return pl.pallas_call(
        _ab_merge_kernel,
        grid=(b, l // tm),
        in_specs=[pl.BlockSpec((None, tm, GDN_V), lat), pl.BlockSpec((None, tm, GDN_V), lat),
                  pl.BlockSpec((None, tm, GLA_V), lat), pl.BlockSpec((None, tm, GLA_V), lat),
                  pl.BlockSpec((None, tm, GLA_V), ext(0)),
                  pl.BlockSpec((None, tm, GDN_V), ext(COL_GZ // GDN_V)),
                  pl.BlockSpec((None, tm, GLA_V), ext(COL_LG // GLA_V)),
                  _resident((1, GDN_DV), lambda bi, i: (0, 0)),
                  _resident((1, GLA_DV), lambda bi, i: (0, 0)),
                  _resident((GDN_V + GLA_V, d), lambda bi, i: (0, 0)),
                  pl.BlockSpec((None, tm, d), lat),
                  pl.BlockSpec((None, 1, d), lambda bi, i: (bi, 0, 0))],
        out_specs=pl.BlockSpec((None, tm, d), lat),
        out_shape=jax.ShapeDtypeStruct((b, l, d), F32),
        compiler_params=_params("parallel", "parallel"),
        name="ab_merge",
    )(og_f, og_b, ol_f, ol_b, ol_intra, proj, proj, gdn_norm, gla_norm, w_out, h, g1)


def _ffn_kernel(h_ref, gn_ref, sh_ref, sc_ref, g2_ref, w1_ref, w3_ref, w2_ref, fn_ref, o_ref, *, nsplit, final_norm):
    h = h_ref[...]
    a = _modulated(h, gn_ref[...], sh_ref[...], sc_ref[...]).astype(BF16)
    dff = w1_ref.shape[1]
    fc = dff // nsplit
    acc = jnp.zeros(h.shape, F32)
    for j in range(nsplit):
        u1 = jnp.dot(a, w1_ref[:, j * fc:(j + 1) * fc], preferred_element_type=F32)
        u3 = jnp.dot(a, w3_ref[:, j * fc:(j + 1) * fc], preferred_element_type=F32)
        acc = acc + jnp.dot((_silu(u1) * u3).astype(BF16), w2_ref[j * fc:(j + 1) * fc, :],
                            preferred_element_type=F32)
    y = h + g2_ref[...] * acc
    if final_norm:
        y = y * lax.rsqrt(jnp.mean(y * y, axis=-1, keepdims=True) + EPS) * fn_ref[...]
    o_ref[...] = y


def _ffn(h, gain, shift, scale, g2, w1, w3, w2, final_w, tm, final_norm):
    b, l, d = h.shape
    dff = w1.shape[1]

    def vec(bi, i):
        return (bi, 0, 0)

    def const(bi, i):
        return (0, 0)

    return pl.pallas_call(
        functools.partial(_ffn_kernel, nsplit=2, final_norm=final_norm),
        grid=(b, l // tm),
        in_specs=[pl.BlockSpec((None, tm, d), lambda bi, i: (bi, i, 0)),
                  _resident((1, d), const),
                  pl.BlockSpec((None, 1, d), vec), pl.BlockSpec((None, 1, d), vec), pl.BlockSpec((None, 1, d), vec),
                  _resident((d, dff), const), _resident((d, dff), const), _resident((dff, d), const),
                  _resident((1, d), const)],
        out_specs=pl.BlockSpec((None, tm, d), lambda bi, i: (bi, i, 0)),
        out_shape=jax.ShapeDtypeStruct((b, l, d), F32),
        compiler_params=_params("parallel", "parallel"),
        name="ffn",
    )(h, gain.reshape(1, d), shift, scale, g2, w1, w3, w2, final_w.reshape(1, d))


def _hy_conv_kernel(xm_ref, xp_ref, xn_ref, w_ref, o_ref, *, tile, seq_len):
    row0 = pl.program_id(2) * tile
    o_ref[...] = _conv3(xm_ref[...], xp_ref[SUBLANE - 1:SUBLANE, :], xn_ref[0:1, :], w_ref[...],
                        row0 != 0, row0 + tile != seq_len)


def _hy_conv(u, conv_w, d, tile):
    b, l, n = u.shape
    groups = n // d
    nrow8 = l // SUBLANE
    return pl.pallas_call(
        functools.partial(_hy_conv_kernel, tile=tile, seq_len=l),
        grid=(groups, b, l // tile),
        in_specs=[pl.BlockSpec((None, tile, d), lambda j, bi, i: (bi, i, j)),
                  pl.BlockSpec((None, SUBLANE, d),
                               lambda j, bi, i: (bi, jnp.maximum(i * (tile // SUBLANE) - 1, 0), j)),
                  pl.BlockSpec((None, SUBLANE, d),
                               lambda j, bi, i: (bi, jnp.minimum((i + 1) * (tile // SUBLANE), nrow8 - 1), j)),
                  pl.BlockSpec((3, d), lambda j, bi, i: (0, j))],
        out_specs=pl.BlockSpec((None, None, tile, d), lambda j, bi, i: (j, bi, i, 0)),
        out_shape=jax.ShapeDtypeStruct((groups, b, l, d), F32),
        compiler_params=_params("parallel", "parallel", "parallel"),
        name="hy_conv",
    )(u, u, u, conv_w)


def _hy_filter_kernel(z_ref, w1_ref, b1_ref, w2_ref, b2_ref, w3_ref, b3_ref, fr_ref, fo_ref, dl_ref, o_ref,
                      *, tile, seq_len, d):
    z = z_ref[...]
    fr = fr_ref[...]
    hdn = jnp.sin(fr * (_dot_hi(z, w1_ref[...]) + b1_ref[...]))
    hdn = jnp.sin(fr * (_dot_hi(hdn, w2_ref[...]) + b2_ref[...]))
    hdn = jnp.sin(fr * (_dot_hi(hdn, w3_ref[...]) + b3_ref[...]))
    filt = _dot(hdn, fo_ref[...])
    window = jnp.exp(-z[:, 0:1] * dl_ref[...]) + HY_SHIFT
    rows = pl.program_id(0) * tile + lax.broadcasted_iota(jnp.int32, (tile, 1), 0)
    out = jnp.where(rows != seq_len, filt * window, 0.0)
    for n in range(o_ref.shape[0]):
        o_ref[n] = out[:, n * d:(n + 1) * d]


def _hy_filters(zext, w1, b1, w2, b2, w3, b3, freq, fo_dir, deltas, seq_len, d, tile):
    n2 = zext.shape[0]
    hid = w1.shape[1]
    orders = fo_dir.shape[2] // d
    half_tiles = seq_len // tile

    def const(i):
        return (0, 0)

    return pl.pallas_call(
        functools.partial(_hy_filter_kernel, tile=tile, seq_len=seq_len, d=d),
        grid=(n2 // tile,),
        in_specs=[pl.BlockSpec((tile, zext.shape[1]), lambda i: (i, 0)),
                  pl.BlockSpec(w1.shape, const), pl.BlockSpec((1, hid), const),
                  pl.BlockSpec(w2.shape, const), pl.BlockSpec((1, hid), const),
                  pl.BlockSpec(w3.shape, const), pl.BlockSpec((1, hid), const),
                  pl.BlockSpec((1, hid), const),
                  pl.BlockSpec((None, hid, orders * d), lambda i: (jnp.where(i < half_tiles, 0, 1), 0, 0)),
                  pl.BlockSpec((1, orders * d), const)],
        out_specs=pl.BlockSpec((orders, tile, d), lambda i: (0, i, 0)),
        out_shape=jax.ShapeDtypeStruct((orders, n2, d), F32),
        compiler_params=_params("parallel"),
        name="hy_filters",
    )(zext, w1, b1.reshape(1, hid), w2, b2.reshape(1, hid), w3, b3.reshape(1, hid), freq.reshape(1, hid),
      fo_dir, deltas)


def _fft_matrices(r):
    n = r * r
    idx = jnp.arange(r, dtype=jnp.int32)

    def cis(m, period):
        ang = (2.0 * math.pi / period) * m.astype(F32)
        return jnp.cos(ang), -jnp.sin(ang)

    fr, fi = cis((idx[:, None] * idx[None, :]) % r, r)
    half = r // 2
    f1_data = jnp.block([[fr[:, :half], -fi[:, :half]], [fi[:, :half], fr[:, :half]]])
    f1_real = jnp.concatenate([fr, fi], axis=0)
    f3 = jnp.block([[fr[:half, :], fi[:half, :]], [-fi[:half, :], fr[:half, :]]]) * (1.0 / n)
    klo = idx[:, None, None]
    khi = idx[None, :, None]
    bb = idx[None, None, :]
    gr, gi = cis((bb * klo + r * bb * khi) % n, n)
    g = jnp.concatenate([jnp.concatenate([gr, -gi], axis=2), jnp.concatenate([gi, gr], axis=2)], axis=1)
    gh = jnp.swapaxes(g, 1, 2)
    return f1_data.astype(BF16), f1_real.astype(BF16), f3.astype(BF16), g.astype(BF16), gh.astype(BF16)


FFT_BG = 16
FFT_DT = 256


def _slab(ref, j):
    return jnp.concatenate([ref[i, :, j, :] for i in range(ref.shape[0])], axis=0)


def _spread_rows(scr, j, bg, val):
    for lg in range(val.shape[1] // LANE):
        scr[lg, pl.ds(j, val.shape[0], stride=bg), :] = val[:, lg * LANE:(lg + 1) * LANE]


def _flush(scr, o_ref):
    for lg in range(scr.shape[0]):
        o_ref[..., lg * LANE:(lg + 1) * LANE] = scr[lg].reshape(o_ref.shape[:-1] + (LANE,)).astype(o_ref.dtype)


def _time_block(arr, sel, n):
    lead = arr.ndim - 4
    return pl.BlockSpec((None,) * lead + (n, arr.shape[-3], FFT_BG, FFT_DT),
                        lambda p, g, dd: tuple(sel) + (p, 0, g, dd))


def _fft_stage1_kernel(f_ref, x_ref, o_ref, ps):
    bg = x_ref.shape[2]
    for j in range(bg):
        p = jnp.dot(f_ref[...], _slab(x_ref, j).astype(BF16), preferred_element_type=F32)
        _spread_rows(ps, j, bg, p)
    _flush(ps, o_ref)


def _fft_stage1(f1, x5, sel, n):
    r2, _ = f1.shape
    r = r2 // 2
    items, _, rb, d = x5.shape[-4:]
    groups = items // n
    return pl.pallas_call(
        _fft_stage1_kernel,
        grid=(groups, rb // FFT_BG, d // FFT_DT),
        in_specs=[_resident(f1.shape, lambda p, g, dd: (0, 0)), _time_block(x5, sel, n)],
        out_specs=pl.BlockSpec((None, 2, r, FFT_BG, FFT_DT), lambda p, g, dd: (p, 0, 0, g, dd)),
        out_shape=jax.ShapeDtypeStruct((groups, 2, r, rb, d), BF16),
        scratch_shapes=[pltpu.VMEM((FFT_DT // LANE, r2 * FFT_BG, LANE), F32)],
        compiler_params=_params("parallel", "parallel", "parallel"),
        name="fft_stage1",
    )(f1, x5)


def _fft_spectrum_kernel(g_ref, p_ref, o_ref):
    for n in range(p_ref.shape[0]):
        p = p_ref[n]
        o_ref[n] = jnp.dot(g_ref[...], p.reshape(p.shape[0] * p.shape[1], p.shape[2]), preferred_element_type=F32)


def _fft_spectrum(g, pk):
    orders, _, r, _, d = pk.shape
    return pl.pallas_call(
        _fft_spectrum_kernel,
        grid=(r,),
        in_specs=[pl.BlockSpec((None, 2 * r, 2 * r), lambda kk: (kk, 0, 0)),
                  pl.BlockSpec((orders, 2, None, r, d), lambda kk: (0, 0, kk, 0, 0))],
        out_specs=pl.BlockSpec((orders, None, 2 * r, d), lambda kk: (0, kk, 0, 0)),
        out_shape=jax.ShapeDtypeStruct((orders, r, 2 * r, d), F32),
        compiler_params=_params("parallel"),
        name="fft_spectrum",
    )(g, pk)


def _fft_stage2_kernel(g_ref, gh_ref, h_ref, p_ref, o_ref):
    r = p_ref.shape[2]
    hr, hi = h_ref[0:r, :], h_ref[r:2 * r, :]
    for n in range(p_ref.shape[0]):
        p = p_ref[n]
        x = jnp.dot(g_ref[...], p.reshape(2 * r, p.shape[2]), preferred_element_type=F32)
        xr, xi = x[:r], x[r:]
        y = jnp.concatenate([xr * hr - xi * hi, xr * hi + xi * hr], axis=0)
        q = jnp.dot(gh_ref[...], y.astype(BF16), preferred_element_type=F32)
        o_ref[n] = q.reshape(o_ref.shape[1:]).astype(o_ref.dtype)


def _fft_stage2(g, gh, hspec, order, p5):
    pairs, _, r, _, d = p5.shape
    return pl.pallas_call(
        _fft_stage2_kernel,
        grid=(r,),
        in_specs=[pl.BlockSpec((None, 2 * r, 2 * r), lambda kk: (kk, 0, 0)),
                  pl.BlockSpec((None, 2 * r, 2 * r), lambda kk: (kk, 0, 0)),
                  pl.BlockSpec((None, None, 2 * r, d), lambda kk: (order, kk, 0, 0)),
                  pl.BlockSpec((pairs, 2, None, r, d), lambda kk: (0, 0, kk, 0, 0))],
        out_specs=pl.BlockSpec((pairs, 2, None, r, d), lambda kk: (0, 0, kk, 0, 0)),
        out_shape=jax.ShapeDtypeStruct(p5.shape, BF16),
        compiler_params=_params("parallel"),
        name="fft_stage2",
    )(g, gh, hspec, p5)


def _fft_stage3_kernel(f3_ref, f1_ref, q_ref, z_ref, gate_ref, skip_ref, *rest, chain):
    if chain:
        zo_ref, po_ref, qs, zs, ps = rest
    else:
        zo_ref, qs, zs = rest
    bg = z_ref.shape[2]
    rows = q_ref.shape[0] * q_ref.shape[1]
    for lg in range(qs.shape[0]):
        qs[lg] = q_ref[..., lg * LANE:(lg + 1) * LANE].astype(F32).reshape(rows * bg, LANE)
    skip = skip_ref[...]
    for j in range(bg):
        qj = jnp.concatenate([qs[lg, pl.ds(j, rows, stride=bg), :] for lg in range(qs.shape[0])], axis=1)
        y = jnp.dot(f3_ref[...], qj.astype(BF16), preferred_element_type=F32)
        zn = _slab(gate_ref, j) * (y + _slab(z_ref, j) * skip)
        _spread_rows(zs, j, bg, zn)
        if chain:
            _spread_rows(ps, j, bg, jnp.dot(f1_ref[...], zn.astype(BF16), preferred_element_type=F32))
    _flush(zs, zo_ref)
    if chain:
        _flush(ps, po_ref)


def _fft_stage3(f3, f1, q5, z5, z_sel, g5, g_sel, skip, chain):
    pairs, _, r, rb, d = q5.shape
    a = z5.shape[-3]
    sds = jax.ShapeDtypeStruct
    out_specs = [pl.BlockSpec((2, a, FFT_BG, FFT_DT), lambda p, g, dd: (p, 0, g, dd))]
    out_shape = [sds((2 * pairs, a, rb, d), F32)]
    lanes = FFT_DT // LANE
    scratch = [pltpu.VMEM((lanes, 2 * r * FFT_BG, LANE), F32), pltpu.VMEM((lanes, 2 * a * FFT_BG, LANE), F32)]
    if chain:
        out_specs.append(pl.BlockSpec((None, 2, r, FFT_BG, FFT_DT), lambda p, g, dd: (p, 0, 0, g, dd)))
        out_shape.append(sds(q5.shape, BF16))
        scratch.append(pltpu.VMEM((lanes, 2 * r * FFT_BG, LANE), F32))
    return pl.pallas_call(
        functools.partial(_fft_stage3_kernel, chain=chain),
        grid=(pairs, rb // FFT_BG, d // FFT_DT),
        in_specs=[_resident(f3.shape, lambda p, g, dd: (0, 0)), _resident(f1.shape, lambda p, g, dd: (0, 0)),
                  pl.BlockSpec((None, 2, r, FFT_BG, FFT_DT), lambda p, g, dd: (p, 0, 0, g, dd)),
                  _time_block(z5, z_sel, 2), _time_block(g5, g_sel, 2),
                  pl.BlockSpec((1, FFT_DT), lambda p, g, dd: (0, dd))],
        out_specs=out_specs,
        out_shape=out_shape,
        scratch_shapes=scratch,
        compiler_params=_params("parallel", "parallel", "parallel"),
        name="fft_stage3",
    )(f3, f1, q5, z5, g5, skip)


def _resid_proj_kernel(z_ref, w_ref, h_ref, g_ref, o_ref):
    m = jnp.dot(z_ref[...].astype(BF16), w_ref[...], preferred_element_type=F32)
    o_ref[...] = h_ref[...] + g_ref[...] * m


def _resid_proj(z, w, h, g1, tm):
    b, l, d = h.shape
    k = z.shape[-1]
    return pl.pallas_call(
        _resid_proj_kernel,
        grid=(b, l // tm),
        in_specs=[pl.BlockSpec((None, tm, k), lambda bi, i: (bi, i, 0)),
                  _resident((k, d), lambda bi, i: (0, 0)),
                  pl.BlockSpec((None, tm, d), lambda bi, i: (bi, i, 0)),
                  pl.BlockSpec((None, 1, d), lambda bi, i: (bi, 0, 0))],
        out_specs=pl.BlockSpec((None, tm, d), lambda bi, i: (bi, i, 0)),
        out_shape=jax.ShapeDtypeStruct((b, l, d), F32),
        compiler_params=_params("parallel", "parallel"),
        name="resid_proj",
    )(z, w, h, g1)


def _regroup_ab_weights(w_in):
    d = w_in.shape[0]
    sizes = (GDN_QK, GDN_QK, GDN_V, GDN_V, 2 * GDN_HEADS, 2 * GDN_HEADS, GLA_QK, GLA_QK, GLA_V, GLA_V,
             2 * GLA_GATE_RANK)
    offs = [0]
    for s in sizes:
        offs.append(offs[-1] + s)
    gq, gk, gv, gz, ga, gb, lq, lk, lv, lg, llr = (w_in[:, offs[i]:offs[i + 1]] for i in range(len(sizes)))
    pad = jnp.zeros((d, LANE - (ga.shape[1] + gb.shape[1] + llr.shape[1])), w_in.dtype)
    return jnp.concatenate([gq, gk, gv, gz, lq, lk, lv, lg, ga, gb, llr, pad], axis=1)


def _lane_row(vals, offset):
    flat = vals.reshape(-1).astype(F32)
    return jnp.zeros((1, LANE), F32).at[0, offset:offset + flat.shape[0]].set(flat)


def _even_layer(h, ctx, mods, norm1, norm2, w_in, conv_w, a_log, dt_bias, gdn_norm, gate_w, gate_b, gla_norm,
                w_out, w1, w3, w2, final_w, final_norm):
    b, l, d = h.shape
    lctx = ctx.shape[1]
    tm = 256
    sh1, sc1, g1, sh2, sc2, g2 = mods["lat"]
    csh1, csc1 = mods["ctx"][0], mods["ctx"][1]
    xe = jnp.concatenate([ctx, h], axis=1)
    shift = jnp.stack([jnp.broadcast_to(csh1, (b, d)), sh1], axis=1)[:, :, None, :]
    scale = jnp.stack([jnp.broadcast_to(csc1, (b, d)), sc1], axis=1)[:, :, None, :]
    proj = _norm_proj(xe, norm1, shift, scale, _regroup_ab_weights(w_in).astype(BF16), tm, lctx // tm)

    gw = jnp.zeros((LANE, 2 * GLA_QK), F32)
    for dd in range(2):
        gw = gw.at[SM_LR + dd * GLA_GATE_RANK:SM_LR + (dd + 1) * GLA_GATE_RANK,
                   dd * GLA_QK:(dd + 1) * GLA_QK].set(gate_w[dd])
    q, k, v, sm, loga = _ab_prep(proj, conv_w, _lane_row(a_log, SM_G), _lane_row(dt_bias, SM_G), gw,
                                 gate_b.reshape(1, 2 * GLA_QK), tm, (0, lctx), (lctx, lctx + l))
    nctx = lctx // CHUNK
    u0, cq, akd, ggl = _gdn_pre(q, k, v, sm, 2)
    m, qd, lgl, ol_intra = _gla_pre(proj, loga, 2)
    og_f, og_b, ol_f, ol_b = _scan(u0, cq, akd, ggl, m, qd, lgl, nctx, 4)
    h = _ab_merge(og_f, og_b, ol_f, ol_b, ol_intra, proj, gdn_norm.reshape(1, GDN_DV), gla_norm.reshape(1, GLA_DV),
                  w_out.astype(BF16), h, g1[:, None, :], tm, lctx // tm)
    return _ffn(h, norm2, sh2[:, None, :], sc2[:, None, :], g2[:, None, :], w1.astype(BF16), w3.astype(BF16),
                w2.astype(BF16), final_w, tm, final_norm)


def _hyena_layer(h, mods, norm1, norm2, w_in, conv_w, pw1, pb1, pw2, pb2, pw3, pb3, freq, filt_out, skip, w_out,
                 w1, w3, w2, final_w, final_norm):
    b, l, d = h.shape
    tm = 256
    sh1, sc1, g1, sh2, sc2, g2 = mods["lat"]
    u = _norm_proj(h, norm1, sh1[:, None, None, :], sc1[:, None, None, :], w_in.astype(BF16), tm, 0)
    u3 = _hy_conv(u, conv_w, d, 512)

    n2 = 2 * l
    r = math.isqrt(n2)
    assert r * r == n2 and b % 2 == 0
    rows = jnp.arange(n2, dtype=jnp.int32)
    pos = jnp.where(rows < l, rows, n2 - rows) % l
    tlin = jnp.linspace(0.0, 1.0, l, dtype=F32)
    wang = (2.0 * math.pi / l) * jnp.arange(l, dtype=F32)
    bands = jnp.linspace(1e-4, HY_BANDS - 1, HY_BANDS, dtype=F32)[None, :]
    zfeat = jnp.concatenate([tlin[:, None], jnp.cos(bands * wang[:, None]), -jnp.sin(bands * wang[:, None])], axis=-1)
    zext = jnp.pad(zfeat[pos], ((0, 0), (0, LANE - HY_EMB)))
    pw1 = jnp.pad(pw1, ((0, LANE - HY_EMB), (0, 0)))
    orders = skip.shape[0]
    fo_dir = jnp.moveaxis(filt_out.reshape(filt_out.shape[0], 2, orders * d), 1, 0)
    deltas = jnp.abs(jnp.linspace(HY_MIN_DECAY, HY_MAX_DECAY, d, dtype=F32))
    kcirc = _hy_filters(zext, pw1, pb1, pw2, pb2, pw3, pb3, freq, fo_dir,
                        jnp.tile(deltas, orders)[None, :], l, d, 512)

    f1_data, f1_real, f3, g, gh = _fft_matrices(r)
    hspec = _fft_spectrum(g, _fft_stage1(f1_real, kcirc.reshape(orders, r, r, d), (), 1))

    u5 = u3.reshape(3, b, r // 2, r, d)
    z_arr, z_sel = u5, (0,)
    p = _fft_stage1(f1_data, z_arr, z_sel, 2)
    for n in range(orders):
        q = _fft_stage2(g, gh, hspec, n, p)
        chain = n + 1 < orders
        res = _fft_stage3(f3, f1_data, q, z_arr, z_sel, u5, (n + 1,), skip[n][None, :], chain)
        z_arr, z_sel = res[0], ()
        if chain:
            p = res[1]
    h = _resid_proj(z_arr.reshape(b, l, d), w_out.astype(BF16), h, g1[:, None, :], tm)
    return _ffn(h, norm2, sh2[:, None, :], sc2[:, None, :], g2[:, None, :], w1.astype(BF16), w3.astype(BF16),
                w2.astype(BF16), final_w, tm, final_norm)


def kernel(x, c, ctx, c_ctx, mod_w, mod_b, norm1_w, norm2_w, ab_w_in, ab_conv_w, gdn_a_log, gdn_dt_bias, gdn_norm_w, gla_gate_w, gla_gate_b, gla_norm_w, ab_w_out, hy_w_in, hy_conv_w, hy_pos_w1, hy_pos_b1, hy_pos_w2, hy_pos_b2, hy_pos_w3, hy_pos_b3, hy_freq, hy_filt_out, hy_skip, hy_w_out, ffn_w1, ffn_w3, ffn_w2, final_norm_w):
    b, l, d = x.shape
    depth = mod_w.shape[0]
    assert b + 1 <= SUBLANE
    cvec = jnp.concatenate([c, c_ctx[None, :], jnp.zeros((SUBLANE - b - 1, d), F32)], axis=0)
    mod = _modulation(cvec, mod_w, mod_b)
    h = x
    for i in range(depth):
        mods = {"lat": [mod[i, :b, j * d:(j + 1) * d] for j in range(6)],
                "ctx": [mod[i, b, j * d:(j + 1) * d] for j in range(6)]}
        last = i == depth - 1
        if i % 2 == 0:
            assert not any(j % 2 == 0 for j in range(i + 1, depth))
            e = i // 2
            h = _even_layer(h, ctx, mods, norm1_w[i], norm2_w[i], ab_w_in[e], ab_conv_w[e], gdn_a_log[e],
                            gdn_dt_bias[e], gdn_norm_w[e], gla_gate_w[e], gla_gate_b[e], gla_norm_w[e], ab_w_out[e],
                            ffn_w1[i], ffn_w3[i], ffn_w2[i], final_norm_w, last)
        else:
            o = i // 2
            h = _hyena_layer(h, mods, norm1_w[i], norm2_w[i], hy_w_in[o], hy_conv_w[o], hy_pos_w1[o], hy_pos_b1[o],
                             hy_pos_w2[o], hy_pos_b2[o], hy_pos_w3[o], hy_pos_b3[o], hy_freq[o], hy_filt_out[o],
                             hy_skip[o], hy_w_out[o], ffn_w1[i], ffn_w3[i], ffn_w2[i], final_norm_w, last)
    return h
```

```python
import functools
import math

import jax
import jax.numpy as jnp
from jax import lax
from jax.experimental import pallas as pl
from jax.experimental.pallas import tpu as pltpu

F32 = jnp.float32
BF16 = jnp.bfloat16
HI = lax.Precision.HIGHEST

EPS = 1e-6
CHUNK = 64
GDN_HEADS, GDN_DK, GDN_DV = 4, 128, 128
GLA_HEADS, GLA_DK, GLA_DV = 4, 64, 128
GLA_GATE_RANK = 16
GLA_GATE_TAU = 16.0
GDN_QK = GDN_HEADS * GDN_DK
GDN_V = GDN_HEADS * GDN_DV
GLA_QK = GLA_HEADS * GLA_DK
GLA_V = GLA_HEADS * GLA_DV
HY_EMB = 33
HY_BANDS = (HY_EMB - 1) // 2
HY_MIN_DECAY = math.log(1e-2) / 1.5
HY_MAX_DECAY = math.log(1e-2) / 0.3
HY_SHIFT = 0.05

LANE = 128
SUBLANE = 8
VMEM_LIMIT_BYTES = 56 * 1024 * 1024

COL_GQ, COL_GK, COL_GV, COL_GZ = 0, 512, 1024, 1536
COL_LQ, COL_LK, COL_LV, COL_LG = 2048, 2304, 2560, 3072
COL_SMALL = 3584
SM_G, SM_B, SM_LR = 0, 8, 16


def _params(*sem):
    return pltpu.CompilerParams(dimension_semantics=sem, vmem_limit_bytes=VMEM_LIMIT_BYTES)


def _resident(shape, index_map):
    return pl.BlockSpec(shape, index_map, pipeline_mode=pl.Buffered(1))


def _dot(a, b):
    return jnp.dot(a.astype(BF16), b.astype(BF16), preferred_element_type=F32)


def _dot_nt(a, b):
    return lax.dot_general(a.astype(BF16), b.astype(BF16), (((1,), (1,)), ((), ())),
                           preferred_element_type=F32)


def _dot_hi(a, b):
    return jnp.dot(a, b, preferred_element_type=F32, precision=HI)


def _sigmoid(x):
    return 1.0 / (1.0 + jnp.exp(-x))


def _silu(x):
    return x * _sigmoid(x)


def _softplus(x):
    return jnp.maximum(x, 0.0) + jnp.log(1.0 + jnp.exp(-jnp.abs(x)))


def _mod_kernel(c_ref, w_ref, b_ref, o_ref):
    o_ref[...] = _dot_hi(_silu(c_ref[...]), w_ref[...]) + b_ref[...]


def _modulation(cvec, mod_w, mod_b):
    depth, d, n = mod_w.shape
    tn = n // 4
    return pl.pallas_call(
        _mod_kernel,
        grid=(depth, n // tn),
        in_specs=[pl.BlockSpec((SUBLANE, d), lambda i, j: (0, 0)),
                  pl.BlockSpec((None, d, tn), lambda i, j: (i, 0, j)),
                  pl.BlockSpec((None, 1, tn), lambda i, j: (i, 0, j))],
        out_specs=pl.BlockSpec((None, SUBLANE, tn), lambda i, j: (i, 0, j)),
        out_shape=jax.ShapeDtypeStruct((depth, SUBLANE, n), F32),
        compiler_params=_params("parallel", "parallel"),
        name="modulation",
    )(cvec, mod_w, mod_b.reshape(depth, 1, n))


CONV_HALO = SUBLANE


def _modulated(x, gain, shift, scale):
    ms = jnp.mean(x * x, axis=-1, keepdims=True)
    return x * lax.rsqrt(ms + EPS) * gain * (1.0 + scale) + shift


def _window_start(i, tm, win, seq_len):
    return SUBLANE * jnp.clip(i * (tm // SUBLANE) - CONV_HALO // SUBLANE, 0, (seq_len - win) // SUBLANE)


def _window_spec(tm, win, seq_len, d):
    return pl.BlockSpec((pl.Element(1), pl.Element(win), pl.Element(d)),
                        lambda bi, i: (bi, _window_start(i, tm, win, seq_len), 0))


def _conv3_rows(u, w, gidx, starts, ends):
    rows = u.shape[0]
    no_prev = functools.reduce(jnp.logical_or, [gidx == r for r in starts])
    no_next = functools.reduce(jnp.logical_or, [gidx == r for r in ends])
    prev = jnp.where(no_prev, 0.0, pltpu.roll(u, 1, 0))
    nxt = jnp.where(no_next, 0.0, pltpu.roll(u, rows - 1, 0))
    return prev * w[0:1, :] + u * w[1:2, :] + nxt * w[2:3, :]


def _hy_front_kernel(x_ref, g_ref, sh_ref, sc_ref, w_ref, cw_ref, o_ref, scr, *, tm, seq_len):
    i = pl.program_id(1)
    win = x_ref.shape[1]
    ws = _window_start(i, tm, win, seq_len)
    off = pl.multiple_of(i * tm - ws, SUBLANE)
    a = _modulated(x_ref[0], g_ref[...], sh_ref[...], sc_ref[...])
    u = jnp.dot(a.astype(BF16), w_ref[...], preferred_element_type=F32)
    gidx = ws + lax.broadcasted_iota(jnp.int32, (win, 1), 0)
    scr[...] = _conv3_rows(u, cw_ref[...], gidx, (0,), (seq_len - 1,))
    d = o_ref.shape[-1]
    for g in range(o_ref.shape[0]):
        o_ref[g] = scr[pl.ds(off, tm), g * d:(g + 1) * d]


def _hy_front(x, gain, shift, scale, w, conv_w, tm):
    b, l, d = x.shape
    n = w.shape[1]
    groups = n // d
    win = tm + 2 * CONV_HALO

    def vec(bi, i):
        return (bi, 0, 0)

    return pl.pallas_call(
        functools.partial(_hy_front_kernel, tm=tm, seq_len=l),
        grid=(b, l // tm),
        in_specs=[_window_spec(tm, win, l, d), _resident((1, d), lambda bi, i: (0, 0)),
                  pl.BlockSpec((None, 1, d), vec), pl.BlockSpec((None, 1, d), vec),
                  _resident((d, n), lambda bi, i: (0, 0)), _resident((3, n), lambda bi, i: (0, 0))],
        out_specs=pl.BlockSpec((groups, None, tm, d), lambda bi, i: (0, bi, i, 0)),
        out_shape=jax.ShapeDtypeStruct((groups, b, l, d), F32),
        scratch_shapes=[pltpu.VMEM((win, n), F32)],
        compiler_params=_params("parallel", "parallel"),
        name="hy_front",
    )(x, gain.reshape(1, d), shift, scale, w, conv_w)


def _ab_front_kernel(x_ref, g_ref, sh_ref, sc_ref, w_ref, cw_ref, alog_ref, dtb_ref, gw_ref, gb_ref,
                     q_ref, k_ref, v_ref, so_ref, la_ref, rest_ref, scr, *, tm, lctx, lext):
    i = pl.program_id(1)
    win = x_ref.shape[1]
    ws = _window_start(i, tm, win, lext)
    off = pl.multiple_of(i * tm - ws, SUBLANE)
    gidx = ws + lax.broadcasted_iota(jnp.int32, (win, 1), 0)
    is_ctx = gidx < lctx
    shift = jnp.where(is_ctx, sh_ref[0], sh_ref[1])
    scale = jnp.where(is_ctx, sc_ref[0], sc_ref[1])
    a = _modulated(x_ref[0], g_ref[...], shift, scale)
    u = jnp.dot(a.astype(BF16), w_ref[...], preferred_element_type=F32)
    ncv = cw_ref.shape[1]
    scr[:, 0:ncv] = _conv3_rows(u[:, 0:ncv], cw_ref[...], gidx, (0, lctx), (lctx - 1, lext - 1))
    scr[:, ncv:] = u[:, ncv:]
    rows = pl.ds(off, tm)
    for h in range(GDN_HEADS):
        sl = slice(h * GDN_DK, (h + 1) * GDN_DK)
        qh = _silu(scr[rows, COL_GQ + h * GDN_DK:COL_GQ + (h + 1) * GDN_DK])
        kh = _silu(scr[rows, COL_GK + h * GDN_DK:COL_GK + (h + 1) * GDN_DK])
        q_ref[:, sl] = qh * lax.rsqrt(jnp.sum(qh * qh, axis=-1, keepdims=True) + EPS) * (GDN_DK ** -0.5)
        k_ref[:, sl] = kh * lax.rsqrt(jnp.sum(kh * kh, axis=-1, keepdims=True) + EPS)
    v_ref[...] = _silu(scr[rows, COL_GV:COL_GV + GDN_V])
    rest_ref[...] = scr[rows, COL_GZ:COL_SMALL]
    sm = scr[rows, COL_SMALL:COL_SMALL + LANE]
    g = -jnp.exp(alog_ref[...]) * _softplus(sm + dtb_ref[...])
    beta = _sigmoid(sm)
    lanes = lax.broadcasted_iota(jnp.int32, sm.shape, 1)
    so_ref[...] = jnp.where(lanes < SM_B, g, beta)
    gate = _dot(sm, gw_ref[...]) + gb_ref[...]
    la_ref[...] = (jnp.minimum(gate, 0.0) - jnp.log(1.0 + jnp.exp(-jnp.abs(gate)))) * (1.0 / GLA_GATE_TAU)


def _ab_front(xe, gain, shift, scale, w, conv_w, alog_row, dtb_row, gate_w, gate_b, tm, lctx):
    b, lext, d = xe.shape
    n = w.shape[1]
    win = tm + 2 * CONV_HALO
    nrest = COL_SMALL - COL_GZ

    def const(bi, i):
        return (0, 0)

    def tok(width):
        return pl.BlockSpec((None, tm, width), lambda bi, i: (bi, i, 0))

    sds = jax.ShapeDtypeStruct
    return pl.pallas_call(
        functools.partial(_ab_front_kernel, tm=tm, lctx=lctx, lext=lext),
        grid=(b, lext // tm),
        in_specs=[_window_spec(tm, win, lext, d), _resident((1, d), const),
                  pl.BlockSpec((None, 2, 1, d), lambda bi, i: (bi, 0, 0, 0)),
                  pl.BlockSpec((None, 2, 1, d), lambda bi, i: (bi, 0, 0, 0)),
                  _resident((d, n), const), _resident(conv_w.shape, const),
                  _resident((1, LANE), const), _resident((1, LANE), const),
                  _resident((LANE, 2 * GLA_QK), const), _resident((1, 2 * GLA_QK), const)],
        out_specs=[tok(GDN_QK), tok(GDN_QK), tok(GDN_V), tok(LANE), tok(2 * GLA_QK), tok(nrest)],
        out_shape=[sds((b, lext, GDN_QK), F32), sds((b, lext, GDN_QK), F32), sds((b, lext, GDN_V), F32),
                   sds((b, lext, LANE), F32), sds((b, lext, 2 * GLA_QK), F32), sds((b, lext, nrest), F32)],
        scratch_shapes=[pltpu.VMEM((win, n), F32)],
        compiler_params=_params("parallel", "parallel"),
        name="ab_front",
    )(xe, gain.reshape(1, d), shift, scale, w, conv_w, alog_row, dtb_row, gate_w, gate_b)


def _ext_chunk(d, s, nctx, ntot):
    if d == 0:
        return s
    return jnp.where(s < nctx, nctx - 1 - s, ntot + nctx - 1 - s)


def _lat_chunk(d, s, nctx, ntot):
    nlat = ntot - nctx
    if d == 0:
        return jnp.maximum(s - nctx, 0)
    return jnp.minimum(ntot - 1 - s, nlat - 1)


def _tri_masks(d):
    ii = lax.broadcasted_iota(jnp.int32, (CHUNK, CHUNK), 0)
    jj = lax.broadcasted_iota(jnp.int32, (CHUNK, CHUNK), 1)
    if d == 0:
        return ii >= jj, ii > jj, ii == jj
    return ii <= jj, ii < jj, ii == jj


def _cumsum_rows(incl, x):
    m = incl.astype(BF16)
    x1 = x.astype(BF16)
    r1 = x - x1.astype(F32)
    x2 = r1.astype(BF16)
    x3 = (r1 - x2.astype(F32)).astype(BF16)
    dot = functools.partial(jnp.dot, preferred_element_type=F32)
    return dot(m, x1) + dot(m, x2) + dot(m, x3)


def _merge_masks(d):
    ii = lax.broadcasted_iota(jnp.int32, (CHUNK, CHUNK), 0)
    jj = lax.broadcasted_iota(jnp.int32, (CHUNK, CHUNK), 1)
    lo, hi = (jj, ii) if d == 0 else (ii, jj)
    masks = {}
    s = 1
    while s < CHUNK:
        masks[s] = jnp.logical_and(ii // (2 * s) == jj // (2 * s),
                                   jnp.logical_and((hi // s) % 2 == 1, (lo // s) % 2 == 0))
        s *= 2
    return masks


def _gdn_pre_kernel(q_ref, k_ref, v_ref, sm_ref, u0_ref, cq_ref, akd_ref, gl_ref, *, cpb):
    tri = [_tri_masks(d) for d in range(2)]
    merge = [_merge_masks(d) for d in range(2)]
    insts = [(d, h) for d in range(2) for h in range(GDN_HEADS)]
    for c in range(cpb):
        rows = slice(c * CHUNK, (c + 1) * CHUNK)
        q = q_ref[rows, :]
        k = k_ref[rows, :]
        v = v_ref[rows, :]
        sm = sm_ref[rows, :]
        pref = _cumsum_rows(tri[0][0], sm)
        tot = jnp.sum(sm, axis=0, keepdims=True)
        lanes = lax.broadcasted_iota(jnp.int32, sm.shape, 1)
        bwd_lane = jnp.logical_and(lanes >= SM_G + GDN_HEADS, lanes < SM_G + 2 * GDN_HEADS)
        gc = jnp.where(bwd_lane, tot - pref + sm, pref)
        gct = gc.T
        totc = jnp.sum(sm.T, axis=1, keepdims=True)
        gl_ref[c] = jnp.exp(jnp.broadcast_to(totc[0:SUBLANE, :], (SUBLANE, LANE)))
        egc = jnp.exp(gc)
        etg = jnp.exp(tot - gc)
        kh = [k[:, h * GDN_DK:(h + 1) * GDN_DK] for h in range(GDN_HEADS)]
        qh = [q[:, h * GDN_DK:(h + 1) * GDN_DK] for h in range(GDN_HEADS)]
        vh = [v[:, h * GDN_DV:(h + 1) * GDN_DV] for h in range(GDN_HEADS)]
        gq = [_dot_nt(jnp.concatenate([kh[h], qh[h]], axis=0), kh[h]) for h in range(GDN_HEADS)]
        a, t, beta, ecol = {}, {}, {}, {}
        for d, h in insts:
            incl, strict, diag = tri[d]
            col = SM_G + d * GDN_HEADS + h
            decay = jnp.exp(jnp.where(incl, gc[:, col:col + 1] - gct[col:col + 1, :], -jnp.inf))
            beta[d, h] = sm[:, SM_B + col:SM_B + col + 1]
            ecol[d, h] = egc[:, col:col + 1]
            a[d, h] = jnp.where(strict, gq[h][0:CHUNK] * decay, 0.0) * beta[d, h]
            akd_ref[c, d, h, 0:CHUNK, :] = (gq[h][CHUNK:2 * CHUNK] * decay).astype(BF16)
            akd_ref[c, d, h, CHUNK:CHUNK + GDN_DK, :] = (kh[h] * etg[:, col:col + 1]).T.astype(BF16)
            t[d, h] = diag.astype(F32) - jnp.where(merge[d][1], a[d, h], 0.0)
        s = 2
        while s < CHUNK:
            x = {i: _dot(jnp.where(merge[i[0]][s], a[i], 0.0), t[i]) for i in insts}
            t = {i: t[i] - _dot(t[i], x[i]) for i in insts}
            s *= 2
        for d, h in insts:
            rhs = jnp.concatenate([vh[h] * beta[d, h], kh[h] * (beta[d, h] * ecol[d, h])], axis=1)
            sol = _dot(t[d, h], rhs)
            u0_ref[c, d, h] = sol[:, 0:GDN_DV]
            cq_ref[c, d, h, 0:CHUNK, :] = sol[:, GDN_DV:GDN_DV + GDN_DK].astype(BF16)
            cq_ref[c, d, h, CHUNK:2 * CHUNK, :] = (qh[h] * ecol[d, h]).astype(BF16)


def _gdn_pre(q, k, v, sm, cpb):
    b, lext, _ = q.shape
    ntot = lext // CHUNK
    rows = cpb * CHUNK
    lead = (b, ntot, 2, GDN_HEADS)

    def tok(bi, i):
        return (bi, i, 0)

    def blk(*tail):
        return pl.BlockSpec((None, cpb, 2, GDN_HEADS) + tail, lambda bi, i: (bi, i, 0, 0, 0, 0))

    sds = jax.ShapeDtypeStruct
    return pl.pallas_call(
        functools.partial(_gdn_pre_kernel, cpb=cpb),
        grid=(b, ntot // cpb),
        in_specs=[pl.BlockSpec((None, rows, GDN_QK), tok), pl.BlockSpec((None, rows, GDN_QK), tok),
                  pl.BlockSpec((None, rows, GDN_V), tok), pl.BlockSpec((None, rows, LANE), tok)],
        out_specs=[blk(CHUNK, GDN_DV), blk(2 * CHUNK, GDN_DK), blk(CHUNK + GDN_DK, CHUNK),
                   pl.BlockSpec((None, cpb, SUBLANE, LANE), lambda bi, i: (bi, i, 0, 0))],
        out_shape=[sds(lead + (CHUNK, GDN_DV), F32), sds(lead + (2 * CHUNK, GDN_DK), BF16),
                   sds(lead + (CHUNK + GDN_DK, CHUNK), BF16), sds((b, ntot, SUBLANE, LANE), F32)],
        compiler_params=_params("parallel", "parallel"),
        name="gdn_pre",
    )(q, k, v, sm)


def _scan_kernel(u0f, cqf, akf, ggf, mf, qdf, lgf, u0b, cqb, akb, ggb, mb, qdb, lgb,
                 gof_ref, gob_ref, lof_ref, lob_ref, sg_ref, sl_ref, *, nctx):
    step = pl.program_id(1)

    @pl.when(step == 0)
    def _():
        sg_ref[...] = jnp.zeros(sg_ref.shape, F32)
        sl_ref[...] = jnp.zeros(sl_ref.shape, F32)

    refs = ((u0f, cqf, akf, ggf, mf, qdf, lgf, gof_ref, lof_ref), (u0b, cqb, akb, ggb, mb, qdb, lgb, gob_ref, lob_ref))
    insts = [(d, h) for d in range(2) for h in range(GDN_HEADS)]
    dot = functools.partial(jnp.dot, preferred_element_type=F32)
    cps = u0f.shape[0]
    s = {i: sg_ref[i[0], i[1]] for i in insts}
    sl = {i: sl_ref[i[0], i[1]] for i in insts}
    for ci in range(cps):
        cd = (ci, cps - 1 - ci)
        cqs = {(d, h): dot(refs[d][1][cd[d], h], s[d, h].astype(BF16)) for d, h in insts}
        ol = {(d, h): dot(refs[d][5][cd[d], h], sl[d, h].astype(BF16)) for d, h in insts}
        u = {(d, h): refs[d][0][cd[d], h] - cqs[d, h][0:CHUNK] for d, h in insts}
        r = {(d, h): dot(refs[d][2][cd[d], h], u[d, h].astype(BF16)) for d, h in insts}
        for d, h in insts:
            lane = d * GDN_HEADS + h
            s[d, h] = s[d, h] * refs[d][3][cd[d], lane:lane + 1, :] + r[d, h][CHUNK:CHUNK + GDN_DK]
            sl[d, h] = sl[d, h] * refs[d][6][cd[d], :, lane:lane + 1] + refs[d][4][cd[d], h]

        @pl.when(step >= nctx)
        def _():
            for d, h in insts:
                rows = slice(cd[d] * CHUNK, (cd[d] + 1) * CHUNK)
                refs[d][7][rows, h * GDN_DV:(h + 1) * GDN_DV] = cqs[d, h][CHUNK:2 * CHUNK] + r[d, h][0:CHUNK]
                refs[d][8][rows, h * GLA_DV:(h + 1) * GLA_DV] = ol[d, h]

    for d, h in insts:
        sg_ref[d, h] = s[d, h]
        sl_ref[d, h] = sl[d, h]


def _scan(u0, cq, akd, ggl, m, qd, lgl, nctx, cps):
    b, nchunks = u0.shape[:2]
    assert nctx % cps == 0 and nchunks % cps == 0
    nlat = nchunks - nctx
    ntot, nctx = nchunks // cps, nctx // cps

    def specs(d):
        def chunk(bi, s):
            return (bi, _ext_chunk(d, s, nctx, ntot))

        def big(*tail):
            return pl.BlockSpec((None, cps, None, GDN_HEADS) + tail, lambda bi, s: chunk(bi, s) + (d, 0, 0, 0))

        def small(*tail):
            return pl.BlockSpec((None, cps) + tail, lambda bi, s: chunk(bi, s) + (0, 0))

        return [big(CHUNK, GDN_DV), big(2 * CHUNK, GDN_DK), big(CHUNK + GDN_DK, CHUNK), small(SUBLANE, LANE),
                big(GLA_DK, GLA_DV), big(CHUNK, GLA_DK), small(GLA_DK, LANE)]

    def out_spec(d, width):
        return pl.BlockSpec((None, cps * CHUNK, width), lambda bi, s: (bi, _lat_chunk(d, s, nctx, ntot), 0))

    sds = jax.ShapeDtypeStruct
    return pl.pallas_call(
        functools.partial(_scan_kernel, nctx=nctx),
        grid=(b, ntot),
        in_specs=specs(0) + specs(1),
        out_specs=[out_spec(0, GDN_V), out_spec(1, GDN_V), out_spec(0, GLA_V), out_spec(1, GLA_V)],
        out_shape=[sds((b, nlat * CHUNK, GDN_V), F32)] * 2 + [sds((b, nlat * CHUNK, GLA_V), F32)] * 2,
        scratch_shapes=[pltpu.VMEM((2, GDN_HEADS, GDN_DK, GDN_DV), F32),
                        pltpu.VMEM((2, GLA_HEADS, GLA_DK, GLA_DV), F32)],
        compiler_params=_params("parallel", "arbitrary"),
        name="scan",
    )(u0, cq, akd, ggl, m, qd, lgl, u0, cq, akd, ggl, m, qd, lgl)


def _gla_pre_kernel(q_ref, k_ref, v_ref, la_ref, m_ref, qd_ref, gl_ref, intra_ref, *, cpb):
    tri = [_tri_masks(d) for d in range(2)]
    for c in range(cpb):
        rows = slice(c * CHUNK, (c + 1) * CHUNK)
        q = q_ref[rows, :] * (GLA_DK ** -0.5)
        k = k_ref[rows, :]
        v = v_ref[rows, :]
        la = la_ref[rows, :]
        pref = _cumsum_rows(tri[0][0], la)
        tot = jnp.sum(la, axis=0, keepdims=True)
        lanes = lax.broadcasted_iota(jnp.int32, la.shape, 1)
        bwd_lane = lanes >= GLA_QK
        bc = jnp.where(bwd_lane, tot - pref + la, pref)
        mid = CHUNK // 2
        ref = jnp.where(bwd_lane[0:1, :], bc[CHUNK - 1 - mid:CHUNK - mid, :], bc[mid:mid + 1, :])
        q2 = jnp.concatenate([q, q], axis=1)
        k2 = jnp.concatenate([k, k], axis=1)
        qa = q2 * jnp.exp(bc - ref)
        ka = k2 * jnp.exp(ref - bc)
        qd = (q2 * jnp.exp(bc)).astype(BF16)
        kdt = (k2 * jnp.exp(tot - bc)).T
        glcol = jnp.exp(jnp.sum(la.T, axis=1, keepdims=True))
        out_lanes = lax.broadcasted_iota(jnp.int32, (GLA_DK, LANE), 1)
        gl = jnp.zeros((GLA_DK, LANE), F32)
        for h in range(GLA_HEADS):
            vh = v[:, h * GLA_DV:(h + 1) * GLA_DV]
            intra = None
            for d in range(2):
                cs = slice(d * GLA_QK + h * GLA_DK, d * GLA_QK + (h + 1) * GLA_DK)
                attn = jnp.where(tri[d][0], _dot_nt(qa[:, cs], ka[:, cs]), 0.0)
                im = _dot(jnp.concatenate([attn, kdt[cs, :]], axis=0), vh)
                intra = im[0:CHUNK] if intra is None else intra + im[0:CHUNK]
                m_ref[c, d, h] = im[CHUNK:CHUNK + GLA_DK]
                qd_ref[c, d, h] = qd[:, cs]
                gl = jnp.where(out_lanes == d * GLA_HEADS + h, glcol[cs, :], gl)
            intra_ref[rows, h * GLA_DV:(h + 1) * GLA_DV] = intra
        gl_ref[c] = gl


def _gla_pre(rest, loga, cpb):
    b, lext, _ = rest.shape
    ntot = lext // CHUNK
    rows = cpb * CHUNK
    lead = (b, ntot, 2, GLA_HEADS)

    def col(colblk):
        return lambda bi, i: (bi, i, colblk)

    def blk(*tail):
        return pl.BlockSpec((None, cpb, 2, GLA_HEADS) + tail, lambda bi, i: (bi, i, 0, 0, 0, 0))

    sds = jax.ShapeDtypeStruct
    return pl.pallas_call(
        functools.partial(_gla_pre_kernel, cpb=cpb),
        grid=(b, ntot // cpb),
        in_specs=[pl.BlockSpec((None, rows, GLA_QK), col((COL_LQ - COL_GZ) // GLA_QK)),
                  pl.BlockSpec((None, rows, GLA_QK), col((COL_LK - COL_GZ) // GLA_QK)),
                  pl.BlockSpec((None, rows, GLA_V), col((COL_LV - COL_GZ) // GLA_V)),
                  pl.BlockSpec((None, rows, 2 * GLA_QK), col(0))],
        out_specs=[blk(GLA_DK, GLA_DV), blk(CHUNK, GLA_DK),
                   pl.BlockSpec((None, cpb, GLA_DK, LANE), lambda bi, i: (bi, i, 0, 0)),
                   pl.BlockSpec((None, rows, GLA_V), col(0))],
        out_shape=[sds(lead + (GLA_DK, GLA_DV), F32), sds(lead + (CHUNK, GLA_DK), BF16),
                   sds((b, ntot, GLA_DK, LANE), F32), sds((b, lext, GLA_V), F32)],
        compiler_params=_params("parallel", "parallel"),
        name="gla_pre",
    )(rest, rest, rest, loga)


def _head_gate(o, gain, z, heads, dv):
    parts = []
    for h in range(heads):
        sl = slice(h * dv, (h + 1) * dv)
        oh = o[:, sl]
        parts.append(oh * lax.rsqrt(jnp.mean(oh * oh, axis=-1, keepdims=True) + EPS) * gain * _silu(z[:, sl]))
    return jnp.concatenate(parts, axis=-1)


def _ab_mix(refs, w_ref):
    gf_ref, gb_ref, lf_ref, lb_ref, li_ref, gz_ref, lg_ref, gn_ref, ln_ref = refs
    yg = _head_gate(gf_ref[...] + gb_ref[...], gn_ref[...], gz_ref[...], GDN_HEADS, GDN_DV)
    yl = _head_gate(lf_ref[...] + lb_ref[...] + li_ref[...], ln_ref[...], lg_ref[...], GLA_HEADS, GLA_DV)
    return (jnp.dot(yg.astype(BF16), w_ref[0:GDN_V, :], preferred_element_type=F32)
            + jnp.dot(yl.astype(BF16), w_ref[GDN_V:GDN_V + GLA_V, :], preferred_element_type=F32))


def _hy_mix(refs, w_ref):
    return jnp.dot(refs[0][...].astype(BF16), w_ref[...], preferred_element_type=F32)


def _tail_kernel(*refs, n_mix, mix_fn, nsplit, final_norm):
    mix_refs = refs[:n_mix]
    wo_ref, h_ref, g1_ref, gn_ref, sh_ref, sc_ref, g2_ref, w1_ref, w3_ref, w2_ref, fn_ref, o_ref = refs[n_mix:]
    h = h_ref[...] + g1_ref[...] * mix_fn(mix_refs, wo_ref)
    a = _modulated(h, gn_ref[...], sh_ref[...], sc_ref[...]).astype(BF16)
    fc = w1_ref.shape[1] // nsplit
    acc = jnp.zeros(h.shape, F32)
    for j in range(nsplit):
        u1 = jnp.dot(a, w1_ref[:, j * fc:(j + 1) * fc], preferred_element_type=F32)
        u3 = jnp.dot(a, w3_ref[:, j * fc:(j + 1) * fc], preferred_element_type=F32)
        acc = acc + jnp.dot((_silu(u1) * u3).astype(BF16), w2_ref[j * fc:(j + 1) * fc, :],
                            preferred_element_type=F32)
    y = h + g2_ref[...] * acc
    if final_norm:
        y = y * lax.rsqrt(jnp.mean(y * y, axis=-1, keepdims=True) + EPS) * fn_ref[...]
    o_ref[...] = y


def _tail(mix_args, mix_specs, mix_fn, w_out, h, g1, gain, shift, scale, g2, w1, w3, w2, final_w, tm, final_norm):
    b, l, d = h.shape
    dff = w1.shape[1]

    def tok(bi, i):
        return (bi, i, 0)

    def vec(bi, i):
        return (bi, 0, 0)

    def const(bi, i):
        return (0, 0)

    return pl.pallas_call(
        functools.partial(_tail_kernel, n_mix=len(mix_args), mix_fn=mix_fn, nsplit=2, final_norm=final_norm),
        grid=(b, l // tm),
        in_specs=list(mix_specs) + [
            _resident(w_out.shape, const), pl.BlockSpec((None, tm, d), tok), pl.BlockSpec((None, 1, d), vec),
            _resident((1, d), const), pl.BlockSpec((None, 1, d), vec), pl.BlockSpec((None, 1, d), vec),
            pl.BlockSpec((None, 1, d), vec), _resident((d, dff), const), _resident((d, dff), const),
            _resident((dff, d), const), _resident((1, d), const)],
        out_specs=pl.BlockSpec((None, tm, d), tok),
        out_shape=jax.ShapeDtypeStruct((b, l, d), F32),
        compiler_params=_params("parallel", "parallel"),
        name="mix_out_ffn",
    )(*mix_args, w_out, h, g1, gain.reshape(1, d), shift, scale, g2, w1, w3, w2, final_w.reshape(1, d))


def _hy_hidden_kernel(fb_ref, w1t_ref, w1c_ref, w1s_ref, b1_ref, w2_ref, b2_ref, w3_ref, b3_ref, fr_ref, o_ref,
                      *, tile, seq_len):
    rows = pl.program_id(0) * tile + lax.broadcasted_iota(jnp.int32, (tile, 1), 0)
    pos = jnp.where(rows < seq_len, rows, 2 * seq_len - rows).astype(F32)
    t = pos * (1.0 / (seq_len - 1))
    ang = (pos * (2.0 * math.pi / seq_len)) * fb_ref[...]
    fr = fr_ref[...]
    pre = t * w1t_ref[...] + _dot_hi(jnp.cos(ang), w1c_ref[...]) - _dot_hi(jnp.sin(ang), w1s_ref[...])
    hdn = jnp.sin(fr * (pre + b1_ref[...]))
    hdn = jnp.sin(fr * (_dot_hi(hdn, w2_ref[...]) + b2_ref[...]))
    o_ref[...] = jnp.sin(fr * (_dot_hi(hdn, w3_ref[...]) + b3_ref[...]))


def _hy_hidden(w1, b1, w2, b2, w3, b3, freq, seq_len, tile):
    hid = w1.shape[1]
    bands = jnp.linspace(1e-4, HY_BANDS - 1, HY_BANDS, dtype=F32)
    fb = jnp.zeros((1, LANE), F32).at[0, :HY_BANDS].set(bands)
    w1c = jnp.zeros((LANE, hid), F32).at[:HY_BANDS].set(w1[1:1 + HY_BANDS])
    w1s = jnp.zeros((LANE, hid), F32).at[:HY_BANDS].set(w1[1 + HY_BANDS:1 + 2 * HY_BANDS])
    args = (fb, w1[0:1], w1c, w1s, b1.reshape(1, hid), w2, b2.reshape(1, hid), w3, b3.reshape(1, hid),
            freq.reshape(1, hid))
    return pl.pallas_call(
        functools.partial(_hy_hidden_kernel, tile=tile, seq_len=seq_len),
        grid=(2 * seq_len // tile,),
        in_specs=[pl.BlockSpec(a.shape, lambda i: (0, 0)) for a in args],
        out_specs=pl.BlockSpec((tile, hid), lambda i: (i, 0)),
        out_shape=jax.ShapeDtypeStruct((2 * seq_len, hid), F32),
        compiler_params=_params("parallel"),
        name="hy_hidden",
    )(*args)


def _fft_matrices(r):
    n = r * r
    idx = jnp.arange(r, dtype=jnp.int32)

    def cis(m, period):
        ang = (2.0 * math.pi / period) * m.astype(F32)
        return jnp.cos(ang), -jnp.sin(ang)

    fr, fi = cis((idx[:, None] * idx[None, :]) % r, r)
    half = r // 2
    f1_data = jnp.block([[fr[:, :half], -fi[:, :half]], [fi[:, :half], fr[:, :half]]])
    f1_real = jnp.concatenate([fr, fi], axis=0)
    f3 = jnp.block([[fr[:half, :], fi[:half, :]], [-fi[:half, :], fr[:half, :]]]) * (1.0 / n)
    klo = idx[:, None, None]
    khi = idx[None, :, None]
    bb = idx[None, None, :]
    gr, gi = cis((bb * klo + r * bb * khi) % n, n)
    g = jnp.concatenate([jnp.concatenate([gr, -gi], axis=2), jnp.concatenate([gi, gr], axis=2)], axis=1)
    gh = jnp.swapaxes(g, 1, 2)
    return f1_data.astype(BF16), f1_real.astype(BF16), f3.astype(BF16), g.astype(BF16), gh.astype(BF16)


FFT_BG = 16
FFT_DT = 256


def _slab(ref, j):
    return jnp.concatenate([ref[i, :, j, :] for i in range(ref.shape[0])], axis=0)


def _spread_rows(scr, j, bg, val):
    for lg in range(val.shape[1] // LANE):
        scr[lg, pl.ds(j, val.shape[0], stride=bg), :] = val[:, lg * LANE:(lg + 1) * LANE]


def _flush(scr, o_ref):
    for lg in range(scr.shape[0]):
        o_ref[..., lg * LANE:(lg + 1) * LANE] = scr[lg].reshape(o_ref.shape[:-1] + (LANE,)).astype(o_ref.dtype)


def _time_block(arr, sel, n):
    lead = arr.ndim - 4
    return pl.BlockSpec((None,) * lead + (n, arr.shape[-3], FFT_BG, FFT_DT),
                        lambda p, g, dd: tuple(sel) + (p, 0, g, dd))


def _fft_stage1_kernel(f_ref, x_ref, o_ref, ps):
    bg = x_ref.shape[2]
    for j in range(bg):
        p = jnp.dot(f_ref[...], _slab(x_ref, j).astype(BF16), preferred_element_type=F32)
        _spread_rows(ps, j, bg, p)
    _flush(ps, o_ref)


def _fft_stage1(f1, x5, sel, n):
    r2, _ = f1.shape
    r = r2 // 2
    items, _, rb, d = x5.shape[-4:]
    groups = items // n
    return pl.pallas_call(
        _fft_stage1_kernel,
        grid=(groups, rb // FFT_BG, d // FFT_DT),
        in_specs=[_resident(f1.shape, lambda p, g, dd: (0, 0)), _time_block(x5, sel, n)],
        out_specs=pl.BlockSpec((None, 2, r, FFT_BG, FFT_DT), lambda p, g, dd: (p, 0, 0, g, dd)),
        out_shape=jax.ShapeDtypeStruct((groups, 2, r, rb, d), BF16),
        scratch_shapes=[pltpu.VMEM((FFT_DT // LANE, r2 * FFT_BG, LANE), F32)],
        compiler_params=_params("parallel", "parallel", "parallel"),
        name="fft_stage1",
    )(f1, x5)


def _fft_filter_stage1_kernel(f_ref, hdn_ref, fo_ref, dl_ref, o_ref, ps, *, seq_len):
    r, bg, _ = hdn_ref.shape
    orders = fo_ref.shape[1]
    half = r // 2
    a_idx = lax.broadcasted_iota(jnp.int32, (r, 1), 0)
    for j in range(bg):
        rows = a_idx * r + (pl.program_id(0) * bg + j)
        pos = jnp.where(rows < seq_len, rows, 2 * seq_len - rows).astype(F32)
        t = pos * (1.0 / (seq_len - 1))
        window = jnp.where(rows != seq_len, jnp.exp(-t * dl_ref[...]) + HY_SHIFT, 0.0)
        hj = hdn_ref[:, j, :]
        for n in range(orders):
            taps = jnp.concatenate([_dot(hj[:half], fo_ref[0, n]), _dot(hj[half:], fo_ref[1, n])], axis=0) * window
            p = jnp.dot(f_ref[...], taps.astype(BF16), preferred_element_type=F32)
            _spread_rows(ps.at[n], j, bg, p)
    for n in range(orders):
        _flush(ps.at[n], o_ref.at[n])


def _fft_filter_stage1(f1, hdn3, fo, deltas, seq_len):
    r2, r = f1.shape
    hid = hdn3.shape[-1]
    _, orders, _, d = fo.shape
    lanes = FFT_DT // LANE
    return pl.pallas_call(
        functools.partial(_fft_filter_stage1_kernel, seq_len=seq_len),
        grid=(r // FFT_BG, d // FFT_DT),
        in_specs=[_resident(f1.shape, lambda g, dd: (0, 0)),
                  pl.BlockSpec((r, FFT_BG, hid), lambda g, dd: (0, g, 0)),
                  pl.BlockSpec((2, orders, hid, FFT_DT), lambda g, dd: (0, 0, 0, dd)),
                  pl.BlockSpec((1, FFT_DT), lambda g, dd: (0, dd))],
        out_specs=pl.BlockSpec((orders, 2, r, FFT_BG, FFT_DT), lambda g, dd: (0, 0, 0, g, dd)),
        out_shape=jax.ShapeDtypeStruct((orders, 2, r, r, d), BF16),
        scratch_shapes=[pltpu.VMEM((orders, lanes, r2 * FFT_BG, LANE), F32)],
        compiler_params=_params("parallel", "parallel"),
        name="fft_filter_stage1",
    )(f1, hdn3, fo, deltas)


def _fft_stage2_kernel(g_ref, gh_ref, k_ref, p_ref, o_ref):
    r = p_ref.shape[2]
    kq = k_ref[...]
    h = jnp.dot(g_ref[...], kq.reshape(2 * r, kq.shape[2]), preferred_element_type=F32)
    hr, hi = h[:r], h[r:]
    for n in range(p_ref.shape[0]):
        p = p_ref[n]
        x = jnp.dot(g_ref[...], p.reshape(2 * r, p.shape[2]), preferred_element_type=F32)
        xr, xi = x[:r], x[r:]
        y = jnp.concatenate([xr * hr - xi * hi, xr * hi + xi * hr], axis=0)
        q = jnp.dot(gh_ref[...], y.astype(BF16), preferred_element_type=F32)
        o_ref[n] = q.reshape(o_ref.shape[1:]).astype(o_ref.dtype)


def _fft_stage2(g, gh, pk, order, p5):
    pairs, _, r, _, d = p5.shape
    return pl.pallas_call(
        _fft_stage2_kernel,
        grid=(r,),
        in_specs=[pl.BlockSpec((None, 2 * r, 2 * r), lambda kk: (kk, 0, 0)),
                  pl.BlockSpec((None, 2 * r, 2 * r), lambda kk: (kk, 0, 0)),
                  pl.BlockSpec((None, 2, None, r, d), lambda kk: (order, 0, kk, 0, 0)),
                  pl.BlockSpec((pairs, 2, None, r, d), lambda kk: (0, 0, kk, 0, 0))],
        out_specs=pl.BlockSpec((pairs, 2, None, r, d), lambda kk: (0, 0, kk, 0, 0)),
        out_shape=jax.ShapeDtypeStruct(p5.shape, BF16),
        compiler_params=_params("parallel"),
        name="fft_stage2",
    )(g, gh, pk, p5)


def _fft_stage3_kernel(f3_ref, f1_ref, q_ref, z_ref, gate_ref, skip_ref, *rest, chain):
    if chain:
        zo_ref, po_ref, qs, zs, ps = rest
    else:
        zo_ref, qs, zs = rest
    bg = z_ref.shape[2]
    rows = q_ref.shape[0] * q_ref.shape[1]
    for lg in range(qs.shape[0]):
        qs[lg] = q_ref[..., lg * LANE:(lg + 1) * LANE].astype(F32).reshape(rows * bg, LANE)
    skip = skip_ref[...]
    for j in range(bg):
        qj = jnp.concatenate([qs[lg, pl.ds(j, rows, stride=bg), :] for lg in range(qs.shape[0])], axis=1)
        y = jnp.dot(f3_ref[...], qj.astype(BF16), preferred_element_type=F32)
        zn = _slab(gate_ref, j) * (y + _slab(z_ref, j) * skip)
        _spread_rows(zs, j, bg, zn)
        if chain:
            _spread_rows(ps, j, bg, jnp.dot(f1_ref[...], zn.astype(BF16), preferred_element_type=F32))
    _flush(zs, zo_ref)
    if chain:
        _flush(ps, po_ref)


def _fft_stage3(f3, f1, q5, z5, z_sel, g5, g_sel, skip, chain):
    pairs, _, r, rb, d = q5.shape
    a = z5.shape[-3]
    sds = jax.ShapeDtypeStruct
    out_specs = [pl.BlockSpec((2, a, FFT_BG, FFT_DT), lambda p, g, dd: (p, 0, g, dd))]
    out_shape = [sds((2 * pairs, a, rb, d), F32)]
    lanes = FFT_DT // LANE
    scratch = [pltpu.VMEM((lanes, 2 * r * FFT_BG, LANE), F32), pltpu.VMEM((lanes, 2 * a * FFT_BG, LANE), F32)]
    if chain:
        out_specs.append(pl.BlockSpec((None, 2, r, FFT_BG, FFT_DT), lambda p, g, dd: (p, 0, 0, g, dd)))
        out_shape.append(sds(q5.shape, BF16))
        scratch.append(pltpu.VMEM((lanes, 2 * r * FFT_BG, LANE), F32))
    return pl.pallas_call(
        functools.partial(_fft_stage3_kernel, chain=chain),
        grid=(pairs, rb // FFT_BG, d // FFT_DT),
        in_specs=[_resident(f3.shape, lambda p, g, dd: (0, 0)), _resident(f1.shape, lambda p, g, dd: (0, 0)),
                  pl.BlockSpec((None, 2, r, FFT_BG, FFT_DT), lambda p, g, dd: (p, 0, 0, g, dd)),
                  _time_block(z5, z_sel, 2), _time_block(g5, g_sel, 2),
                  pl.BlockSpec((1, FFT_DT), lambda p, g, dd: (0, dd))],
        out_specs=out_specs,
        out_shape=out_shape,
        scratch_shapes=scratch,
        compiler_params=_params("parallel", "parallel", "parallel"),
        name="fft_stage3",
    )(f3, f1, q5, z5, g5, skip)


def _regroup_ab_weights(w_in):
    d = w_in.shape[0]
    sizes = (GDN_QK, GDN_QK, GDN_V, GDN_V, 2 * GDN_HEADS, 2 * GDN_HEADS, GLA_QK, GLA_QK, GLA_V, GLA_V,
             2 * GLA_GATE_RANK)
    offs = [0]
    for s in sizes:
        offs.append(offs[-1] + s)
    gq, gk, gv, gz, ga, gb, lq, lk, lv, lg, llr = (w_in[:, offs[i]:offs[i + 1]] for i in range(len(sizes)))
    pad = jnp.zeros((d, LANE - (ga.shape[1] + gb.shape[1] + llr.shape[1])), w_in.dtype)
    return jnp.concatenate([gq, gk, gv, gz, lq, lk, lv, lg, ga, gb, llr, pad], axis=1)


def _lane_row(vals, offset):
    flat = vals.reshape(-1).astype(F32)
    return jnp.zeros((1, LANE), F32).at[0, offset:offset + flat.shape[0]].set(flat)


def _even_layer(h, ctx, mods, norm1, norm2, w_in, conv_w, a_log, dt_bias, gdn_norm, gate_w, gate_b, gla_norm,
                w_out, w1, w3, w2, final_w, final_norm):
    b, l, d = h.shape
    lctx = ctx.shape[1]
    tm = 256
    sh1, sc1, g1, sh2, sc2, g2 = mods["lat"]
    csh1, csc1 = mods["ctx"][0], mods["ctx"][1]
    xe = jnp.concatenate([ctx, h], axis=1)
    shift = jnp.stack([jnp.broadcast_to(csh1, (b, d)), sh1], axis=1)[:, :, None, :]
    scale = jnp.stack([jnp.broadcast_to(csc1, (b, d)), sc1], axis=1)[:, :, None, :]
    gw = jnp.zeros((LANE, 2 * GLA_QK), F32)
    for dd in range(2):
        gw = gw.at[SM_LR + dd * GLA_GATE_RANK:SM_LR + (dd + 1) * GLA_GATE_RANK,
                   dd * GLA_QK:(dd + 1) * GLA_QK].set(gate_w[dd])
    q, k, v, sm, loga, rest = _ab_front(xe, norm1, shift, scale, _regroup_ab_weights(w_in).astype(BF16), conv_w,
                                        _lane_row(a_log, SM_G), _lane_row(dt_bias, SM_G), gw,
                                        gate_b.reshape(1, 2 * GLA_QK), tm, lctx)
    nctx = lctx // CHUNK
    u0, cq, akd, ggl = _gdn_pre(q, k, v, sm, 2)
    m, qd, lgl, ol_intra = _gla_pre(rest, loga, 2)
    og_f, og_b, ol_f, ol_b = _scan(u0, cq, akd, ggl, m, qd, lgl, nctx, 4)
    ctx_tiles = lctx // tm

    def lat(width):
        return pl.BlockSpec((None, tm, width), lambda bi, i: (bi, i, 0))

    def ext(width, colblk):
        return pl.BlockSpec((None, tm, width), lambda bi, i: (bi, i + ctx_tiles, colblk))

    mix_args = (og_f, og_b, ol_f, ol_b, ol_intra, rest, rest, gdn_norm.reshape(1, GDN_DV), gla_norm.reshape(1, GLA_DV))
    mix_specs = [lat(GDN_V), lat(GDN_V), lat(GLA_V), lat(GLA_V), ext(GLA_V, 0), ext(GDN_V, 0),
                 ext(GLA_V, (COL_LG - COL_GZ) // GLA_V), _resident((1, GDN_DV), lambda bi, i: (0, 0)),
                 _resident((1, GLA_DV), lambda bi, i: (0, 0))]
    return _tail(mix_args, mix_specs, _ab_mix, w_out.astype(BF16), h, g1[:, None, :], norm2, sh2[:, None, :],
                 sc2[:, None, :], g2[:, None, :], w1.astype(BF16), w3.astype(BF16), w2.astype(BF16), final_w, tm,
                 final_norm)


def _hyena_layer(h, mods, norm1, norm2, w_in, conv_w, pw1, pb1, pw2, pb2, pw3, pb3, freq, filt_out, skip, w_out,
                 w1, w3, w2, final_w, final_norm):
    b, l, d = h.shape
    tm = 256
    sh1, sc1, g1, sh2, sc2, g2 = mods["lat"]
    u3 = _hy_front(h, norm1, sh1[:, None, :], sc1[:, None, :], w_in.astype(BF16), conv_w, tm)

    n2 = 2 * l
    r = math.isqrt(n2)
    assert r * r == n2 and b % 2 == 0
    orders = skip.shape[0]
    hdn = _hy_hidden(pw1, pb1, pw2, pb2, pw3, pb3, freq, l, 512)
    fo = jnp.transpose(filt_out.reshape(filt_out.shape[0], 2, orders, d), (1, 2, 0, 3))
    deltas = jnp.abs(jnp.linspace(HY_MIN_DECAY, HY_MAX_DECAY, d, dtype=F32))[None, :]

    f1_data, f1_real, f3, g, gh = _fft_matrices(r)
    pk = _fft_filter_stage1(f1_real, hdn.reshape(r, r, hdn.shape[1]), fo, deltas, l)

    u5 = u3.reshape(3, b, r // 2, r, d)
    z_arr, z_sel = u5, (0,)
    p = _fft_stage1(f1_data, z_arr, z_sel, 2)
    for n in range(orders):
        q = _fft_stage2(g, gh, pk, n, p)
        chain = n + 1 < orders
        res = _fft_stage3(f3, f1_data, q, z_arr, z_sel, u5, (n + 1,), skip[n][None, :], chain)
        z_arr, z_sel = res[0], ()
        if chain:
            p = res[1]
    mix_specs = [pl.BlockSpec((None, tm, d), lambda bi, i: (bi, i, 0))]
    return _tail((z_arr.reshape(b, l, d),), mix_specs, _hy_mix, w_out.astype(BF16), h, g1[:, None, :], norm2,
                 sh2[:, None, :], sc2[:, None, :], g2[:, None, :], w1.astype(BF16), w3.astype(BF16), w2.astype(BF16),
                 final_w, tm, final_norm)


def kernel(x, c, ctx, c_ctx, mod_w, mod_b, norm1_w, norm2_w, ab_w_in, ab_conv_w, gdn_a_log, gdn_dt_bias, gdn_norm_w, gla_gate_w, gla_gate_b, gla_norm_w, ab_w_out, hy_w_in, hy_conv_w, hy_pos_w1, hy_pos_b1, hy_pos_w2, hy_pos_b2, hy_pos_w3, hy_pos_b3, hy_freq, hy_filt_out, hy_skip, hy_w_out, ffn_w1, ffn_w3, ffn_w2, final_norm_w):
    b, l, d = x.shape
    depth = mod_w.shape[0]
    assert b + 1 <= SUBLANE
    cvec = jnp.concatenate([c, c_ctx[None, :], jnp.zeros((SUBLANE - b - 1, d), F32)], axis=0)
    mod = _modulation(cvec, mod_w, mod_b)
    h = x
    for i in range(depth):
        mods = {"lat": [mod[i, :b, j * d:(j + 1) * d] for j in range(6)],
                "ctx": [mod[i, b, j * d:(j + 1) * d] for j in range(6)]}
        last = i == depth - 1
        if i % 2 == 0:
            assert not any(j % 2 == 0 for j in range(i + 1, depth))
            e = i // 2
            h = _even_layer(h, ctx, mods, norm1_w[i], norm2_w[i], ab_w_in[e], ab_conv_w[e], gdn_a_log[e],
                            gdn_dt_bias[e], gdn_norm_w[e], gla_gate_w[e], gla_gate_b[e], gla_norm_w[e], ab_w_out[e],
                            ffn_w1[i], ffn_w3[i], ffn_w2[i], final_norm_w, last)
        else:
            o = i // 2
            h = _hyena_layer(h, mods, norm1_w[i], norm2_w[i], hy_w_in[o], hy_conv_w[o], hy_pos_w1[o], hy_pos_b1[o],
                             hy_pos_w2[o], hy_pos_b2[o], hy_pos_w3[o], hy_pos_b3[o], hy_freq[o], hy_filt_out[o],
                             hy_skip[o], hy_w_out[o], ffn_w1[i], ffn_w3[i], ffn_w2[i], final_norm_w, last)
    return h
```

```python
import functools
import math

import jax
import jax.numpy as jnp
from jax import lax
from jax.experimental import pallas as pl
from jax.experimental.pallas import tpu as pltpu

F32 = jnp.float32
BF16 = jnp.bfloat16
HI = lax.Precision.HIGHEST

EPS = 1e-6
CHUNK = 64
GDN_HEADS, GDN_DK, GDN_DV = 4, 128, 128
GLA_HEADS, GLA_DK, GLA_DV = 4, 64, 128
GLA_GATE_RANK = 16
GLA_GATE_TAU = 16.0
GDN_QK = GDN_HEADS * GDN_DK
GDN_V = GDN_HEADS * GDN_DV
GLA_QK = GLA_HEADS * GLA_DK
GLA_V = GLA_HEADS * GLA_DV
HY_EMB = 33
HY_BANDS = (HY_EMB - 1) // 2
HY_MIN_DECAY = math.log(1e-2) / 1.5
HY_MAX_DECAY = math.log(1e-2) / 0.3
HY_SHIFT = 0.05

LANE = 128
SUBLANE = 8
VMEM_LIMIT_BYTES = 56 * 1024 * 1024

COL_GQ, COL_GK, COL_GV, COL_GZ = 0, 512, 1024, 1536
COL_LQ, COL_LK, COL_LV, COL_LG = 2048, 2304, 2560, 3072
COL_SMALL = 3584
SM_G, SM_B, SM_LR = 0, 8, 16


def _params(*sem):
    return pltpu.CompilerParams(dimension_semantics=sem, vmem_limit_bytes=VMEM_LIMIT_BYTES)


def _resident(shape, index_map):
    return pl.BlockSpec(shape, index_map, pipeline_mode=pl.Buffered(1))


def _dot(a, b):
    return jnp.dot(a.astype(BF16), b.astype(BF16), preferred_element_type=F32)


def _dot_nt(a, b):
    return lax.dot_general(a.astype(BF16), b.astype(BF16), (((1,), (1,)), ((), ())),
                           preferred_element_type=F32)


def _dot_hi(a, b):
    return jnp.dot(a, b, preferred_element_type=F32, precision=HI)


def _sigmoid(x):
    return 1.0 / (1.0 + jnp.exp(-x))


def _silu(x):
    return x * _sigmoid(x)


def _softplus(x):
    return jnp.maximum(x, 0.0) + jnp.log(1.0 + jnp.exp(-jnp.abs(x)))


def _mod_kernel(c_ref, w_ref, b_ref, o_ref):
    o_ref[...] = _dot_hi(_silu(c_ref[...]), w_ref[...]) + b_ref[...]


def _modulation(cvec, mod_w, mod_b):
    depth, d, n = mod_w.shape
    tn = n // 4
    return pl.pallas_call(
        _mod_kernel,
        grid=(depth, n // tn),
        in_specs=[pl.BlockSpec((SUBLANE, d), lambda i, j: (0, 0)),
                  pl.BlockSpec((None, d, tn), lambda i, j: (i, 0, j)),
                  pl.BlockSpec((None, 1, tn), lambda i, j: (i, 0, j))],
        out_specs=pl.BlockSpec((None, SUBLANE, tn), lambda i, j: (i, 0, j)),
        out_shape=jax.ShapeDtypeStruct((depth, SUBLANE, n), F32),
        compiler_params=_params("parallel", "parallel"),
        name="modulation",
    )(cvec, mod_w, mod_b.reshape(depth, 1, n))


CONV_HALO = SUBLANE


def _modulated(x, gain, shift, scale):
    ms = jnp.mean(x * x, axis=-1, keepdims=True)
    return x * lax.rsqrt(ms + EPS) * gain * (1.0 + scale) + shift


def _window_start(i, tm, win, seq_len):
    return SUBLANE * jnp.clip(i * (tm // SUBLANE) - CONV_HALO // SUBLANE, 0, (seq_len - win) // SUBLANE)


def _window_spec(tm, win, seq_len, d):
    return pl.BlockSpec((pl.Element(1), pl.Element(win), pl.Element(d)),
                        lambda bi, i: (bi, _window_start(i, tm, win, seq_len), 0))


def _conv3_rows(u, w, gidx, starts, ends):
    rows = u.shape[0]
    no_prev = functools.reduce(jnp.logical_or, [gidx == r for r in starts])
    no_next = functools.reduce(jnp.logical_or, [gidx == r for r in ends])
    prev = jnp.where(no_prev, 0.0, pltpu.roll(u, 1, 0))
    nxt = jnp.where(no_next, 0.0, pltpu.roll(u, rows - 1, 0))
    return prev * w[0:1, :] + u * w[1:2, :] + nxt * w[2:3, :]


def _hy_front_kernel(x_ref, g_ref, sh_ref, sc_ref, w_ref, cw_ref, o_ref, scr, *, tm, seq_len):
    i = pl.program_id(1)
    win = x_ref.shape[1]
    ws = _window_start(i, tm, win, seq_len)
    off = pl.multiple_of(i * tm - ws, SUBLANE)
    a = _modulated(x_ref[0], g_ref[...], sh_ref[...], sc_ref[...])
    u = jnp.dot(a.astype(BF16), w_ref[...], preferred_element_type=F32)
    gidx = ws + lax.broadcasted_iota(jnp.int32, (win, 1), 0)
    scr[...] = _conv3_rows(u, cw_ref[...], gidx, (0,), (seq_len - 1,))
    d = o_ref.shape[-1]
    for g in range(o_ref.shape[0]):
        o_ref[g] = scr[pl.ds(off, tm), g * d:(g + 1) * d]


def _hy_front(x, gain, shift, scale, w, conv_w, tm):
    b, l, d = x.shape
    n = w.shape[1]
    groups = n // d
    win = tm + 2 * CONV_HALO

    def vec(bi, i):
        return (bi, 0, 0)

    return pl.pallas_call(
        functools.partial(_hy_front_kernel, tm=tm, seq_len=l),
        grid=(b, l // tm),
        in_specs=[_window_spec(tm, win, l, d), _resident((1, d), lambda bi, i: (0, 0)),
                  pl.BlockSpec((None, 1, d), vec), pl.BlockSpec((None, 1, d), vec),
                  _resident((d, n), lambda bi, i: (0, 0)), _resident((3, n), lambda bi, i: (0, 0))],
        out_specs=pl.BlockSpec((groups, None, tm, d), lambda bi, i: (0, bi, i, 0)),
        out_shape=jax.ShapeDtypeStruct((groups, b, l, d), F32),
        scratch_shapes=[pltpu.VMEM((win, n), F32)],
        compiler_params=_params("parallel", "parallel"),
        name="hy_front",
    )(x, gain.reshape(1, d), shift, scale, w, conv_w)


def _ab_front_kernel(x_ref, g_ref, sh_ref, sc_ref, w_ref, cw_ref, alog_ref, dtb_ref, gw_ref, gb_ref,
                     q_ref, k_ref, v_ref, so_ref, la_ref, rest_ref, scr, *, tm, lctx, lext):
    i = pl.program_id(1)
    win = x_ref.shape[1]
    ws = _window_start(i, tm, win, lext)
    off = pl.multiple_of(i * tm - ws, SUBLANE)
    gidx = ws + lax.broadcasted_iota(jnp.int32, (win, 1), 0)
    is_ctx = gidx < lctx
    shift = jnp.where(is_ctx, sh_ref[0], sh_ref[1])
    scale = jnp.where(is_ctx, sc_ref[0], sc_ref[1])
    a = _modulated(x_ref[0], g_ref[...], shift, scale)
    u = jnp.dot(a.astype(BF16), w_ref[...], preferred_element_type=F32)
    ncv = cw_ref.shape[1]
    scr[:, 0:ncv] = _conv3_rows(u[:, 0:ncv], cw_ref[...], gidx, (0, lctx), (lctx - 1, lext - 1))
    scr[:, ncv:] = u[:, ncv:]
    rows = pl.ds(off, tm)
    for h in range(GDN_HEADS):
        sl = slice(h * GDN_DK, (h + 1) * GDN_DK)
        qh = _silu(scr[rows, COL_GQ + h * GDN_DK:COL_GQ + (h + 1) * GDN_DK])
        kh = _silu(scr[rows, COL_GK + h * GDN_DK:COL_GK + (h + 1) * GDN_DK])
        q_ref[:, sl] = qh * lax.rsqrt(jnp.sum(qh * qh, axis=-1, keepdims=True) + EPS) * (GDN_DK ** -0.5)
        k_ref[:, sl] = kh * lax.rsqrt(jnp.sum(kh * kh, axis=-1, keepdims=True) + EPS)
    v_ref[...] = _silu(scr[rows, COL_GV:COL_GV + GDN_V])
    rest_ref[...] = scr[rows, COL_GZ:COL_SMALL]
    sm = scr[rows, COL_SMALL:COL_SMALL + LANE]
    g = -jnp.exp(alog_ref[...]) * _softplus(sm + dtb_ref[...])
    beta = _sigmoid(sm)
    lanes = lax.broadcasted_iota(jnp.int32, sm.shape, 1)
    so_ref[...] = jnp.where(lanes < SM_B, g, beta)
    gate = _dot(sm, gw_ref[...]) + gb_ref[...]
    la_ref[...] = (jnp.minimum(gate, 0.0) - jnp.log(1.0 + jnp.exp(-jnp.abs(gate)))) * (1.0 / GLA_GATE_TAU)


def _ab_front(xe, gain, shift, scale, w, conv_w, alog_row, dtb_row, gate_w, gate_b, tm, lctx):
    b, lext, d = xe.shape
    n = w.shape[1]
    win = tm + 2 * CONV_HALO
    nrest = COL_SMALL - COL_GZ

    def const(bi, i):
        return (0, 0)

    def tok(width):
        return pl.BlockSpec((None, tm, width), lambda bi, i: (bi, i, 0))

    sds = jax.ShapeDtypeStruct
    return pl.pallas_call(
        functools.partial(_ab_front_kernel, tm=tm, lctx=lctx, lext=lext),
        grid=(b, lext // tm),
        in_specs=[_window_spec(tm, win, lext, d), _resident((1, d), const),
                  pl.BlockSpec((None, 2, 1, d), lambda bi, i: (bi, 0, 0, 0)),
                  pl.BlockSpec((None, 2, 1, d), lambda bi, i: (bi, 0, 0, 0)),
                  _resident((d, n), const), _resident(conv_w.shape, const),
                  _resident((1, LANE), const), _resident((1, LANE), const),
                  _resident((LANE, 2 * GLA_QK), const), _resident((1, 2 * GLA_QK), const)],
        out_specs=[tok(GDN_QK), tok(GDN_QK), tok(GDN_V), tok(LANE), tok(2 * GLA_QK), tok(nrest)],
        out_shape=[sds((b, lext, GDN_QK), F32), sds((b, lext, GDN_QK), F32), sds((b, lext, GDN_V), F32),
                   sds((b, lext, LANE), F32), sds((b, lext, 2 * GLA_QK), F32), sds((b, lext, nrest), F32)],
        scratch_shapes=[pltpu.VMEM((win, n), F32)],
        compiler_params=_params("parallel", "parallel"),
        name="ab_front",
    )(xe, gain.reshape(1, d), shift, scale, w, conv_w, alog_row, dtb_row, gate_w, gate_b)


def _ext_chunk(d, s, nctx, ntot):
    if d == 0:
        return s
    return jnp.where(s < nctx, nctx - 1 - s, ntot + nctx - 1 - s)


def _lat_chunk(d, s, nctx, ntot):
    nlat = ntot - nctx
    if d == 0:
        return jnp.maximum(s - nctx, 0)
    return jnp.minimum(ntot - 1 - s, nlat - 1)


def _tri_masks(d):
    ii = lax.broadcasted_iota(jnp.int32, (CHUNK, CHUNK), 0)
    jj = lax.broadcasted_iota(jnp.int32, (CHUNK, CHUNK), 1)
    if d == 0:
        return ii >= jj, ii > jj, ii == jj
    return ii <= jj, ii < jj, ii == jj


def _cumsum_rows(incl, x):
    m = incl.astype(BF16)
    x1 = x.astype(BF16)
    r1 = x - x1.astype(F32)
    x2 = r1.astype(BF16)
    x3 = (r1 - x2.astype(F32)).astype(BF16)
    dot = functools.partial(jnp.dot, preferred_element_type=F32)
    return dot(m, x1) + dot(m, x2) + dot(m, x3)


def _merge_masks(d):
    ii = lax.broadcasted_iota(jnp.int32, (CHUNK, CHUNK), 0)
    jj = lax.broadcasted_iota(jnp.int32, (CHUNK, CHUNK), 1)
    lo, hi = (jj, ii) if d == 0 else (ii, jj)
    masks = {}
    s = 1
    while s < CHUNK:
        masks[s] = jnp.logical_and(ii // (2 * s) == jj // (2 * s),
                                   jnp.logical_and((hi // s) % 2 == 1, (lo // s) % 2 == 0))
        s *= 2
    return masks


def _gdn_pre_kernel(q_ref, k_ref, v_ref, sm_ref, u0_ref, cq_ref, akd_ref, gl_ref, *, cpb):
    tri = [_tri_masks(d) for d in range(2)]
    merge = [_merge_masks(d) for d in range(2)]
    insts = [(d, h) for d in range(2) for h in range(GDN_HEADS)]
    for c in range(cpb):
        rows = slice(c * CHUNK, (c + 1) * CHUNK)
        q = q_ref[rows, :]
        k = k_ref[rows, :]
        v = v_ref[rows, :]
        sm = sm_ref[rows, :]
        pref = _cumsum_rows(tri[0][0], sm)
        tot = jnp.sum(sm, axis=0, keepdims=True)
        lanes = lax.broadcasted_iota(jnp.int32, sm.shape, 1)
        bwd_lane = jnp.logical_and(lanes >= SM_G + GDN_HEADS, lanes < SM_G + 2 * GDN_HEADS)
        gc = jnp.where(bwd_lane, tot - pref + sm, pref)
        gct = gc.T
        totc = jnp.sum(sm.T, axis=1, keepdims=True)
        gl_ref[c] = jnp.exp(jnp.broadcast_to(totc[0:SUBLANE, :], (SUBLANE, LANE)))
        egc = jnp.exp(gc)
        etg = jnp.exp(tot - gc)
        kh = [k[:, h * GDN_DK:(h + 1) * GDN_DK] for h in range(GDN_HEADS)]
        qh = [q[:, h * GDN_DK:(h + 1) * GDN_DK] for h in range(GDN_HEADS)]
        vh = [v[:, h * GDN_DV:(h + 1) * GDN_DV] for h in range(GDN_HEADS)]
        gq = [_dot_nt(jnp.concatenate([kh[h], qh[h]], axis=0), kh[h]) for h in range(GDN_HEADS)]
        a, t, beta, ecol = {}, {}, {}, {}
        for d, h in insts:
            incl, strict, diag = tri[d]
            col = SM_G + d * GDN_HEADS + h
            decay = jnp.exp(jnp.where(incl, gc[:, col:col + 1] - gct[col:col + 1, :], -jnp.inf))
            beta[d, h] = sm[:, SM_B + col:SM_B + col + 1]
            ecol[d, h] = egc[:, col:col + 1]
            a[d, h] = jnp.where(strict, gq[h][0:CHUNK] * decay, 0.0) * beta[d, h]
            akd_ref[c, d, h, 0:CHUNK, :] = (gq[h][CHUNK:2 * CHUNK] * decay).astype(BF16)
            akd_ref[c, d, h, CHUNK:CHUNK + GDN_DK, :] = (kh[h] * etg[:, col:col + 1]).T.astype(BF16)
            t[d, h] = diag.astype(F32) - jnp.where(merge[d][1], a[d, h], 0.0)
        s = 2
        while s < CHUNK:
            x = {i: _dot(jnp.where(merge[i[0]][s], a[i], 0.0), t[i]) for i in insts}
            t = {i: t[i] - _dot(t[i], x[i]) for i in insts}
            s *= 2
        for d, h in insts:
            rhs = jnp.concatenate([vh[h] * beta[d, h], kh[h] * (beta[d, h] * ecol[d, h])], axis=1)
            sol = _dot(t[d, h], rhs)
            u0_ref[c, d, h] = sol[:, 0:GDN_DV]
            cq_ref[c, d, h, 0:CHUNK, :] = sol[:, GDN_DV:GDN_DV + GDN_DK].astype(BF16)
            cq_ref[c, d, h, CHUNK:2 * CHUNK, :] = (qh[h] * ecol[d, h]).astype(BF16)


def _gdn_pre(q, k, v, sm, cpb):
    b, lext, _ = q.shape
    ntot = lext // CHUNK
    rows = cpb * CHUNK
    lead = (b, ntot, 2, GDN_HEADS)

    def tok(bi, i):
        return (bi, i, 0)

    def blk(*tail):
        return pl.BlockSpec((None, cpb, 2, GDN_HEADS) + tail, lambda bi, i: (bi, i, 0, 0, 0, 0))

    sds = jax.ShapeDtypeStruct
    return pl.pallas_call(
        functools.partial(_gdn_pre_kernel, cpb=cpb),
        grid=(b, ntot // cpb),
        in_specs=[pl.BlockSpec((None, rows, GDN_QK), tok), pl.BlockSpec((None, rows, GDN_QK), tok),
                  pl.BlockSpec((None, rows, GDN_V), tok), pl.BlockSpec((None, rows, LANE), tok)],
        out_specs=[blk(CHUNK, GDN_DV), blk(2 * CHUNK, GDN_DK), blk(CHUNK + GDN_DK, CHUNK),
                   pl.BlockSpec((None, cpb, SUBLANE, LANE), lambda bi, i: (bi, i, 0, 0))],
        out_shape=[sds(lead + (CHUNK, GDN_DV), F32), sds(lead + (2 * CHUNK, GDN_DK), BF16),
                   sds(lead + (CHUNK + GDN_DK, CHUNK), BF16), sds((b, ntot, SUBLANE, LANE), F32)],
        compiler_params=_params("parallel", "parallel"),
        name="gdn_pre",
    )(q, k, v, sm)


def _scan_kernel(u0f, cqf, akf, ggf, mf, qdf, lgf, u0b, cqb, akb, ggb, mb, qdb, lgb,
                 gof_ref, gob_ref, lof_ref, lob_ref, sg_ref, sl_ref, *, nctx):
    step = pl.program_id(1)

    @pl.when(step == 0)
    def _():
        sg_ref[...] = jnp.zeros(sg_ref.shape, F32)
        sl_ref[...] = jnp.zeros(sl_ref.shape, F32)

    refs = ((u0f, cqf, akf, ggf, mf, qdf, lgf, gof_ref, lof_ref), (u0b, cqb, akb, ggb, mb, qdb, lgb, gob_ref, lob_ref))
    insts = [(d, h) for d in range(2) for h in range(GDN_HEADS)]
    dot = functools.partial(jnp.dot, preferred_element_type=F32)
    cps = u0f.shape[0]
    s = {i: sg_ref[i[0], i[1]] for i in insts}
    sl = {i: sl_ref[i[0], i[1]] for i in insts}
    for ci in range(cps):
        cd = (ci, cps - 1 - ci)
        cqs = {(d, h): dot(refs[d][1][cd[d], h], s[d, h].astype(BF16)) for d, h in insts}
        ol = {(d, h): dot(refs[d][5][cd[d], h], sl[d, h].astype(BF16)) for d, h in insts}
        u = {(d, h): refs[d][0][cd[d], h] - cqs[d, h][0:CHUNK] for d, h in insts}
        r = {(d, h): dot(refs[d][2][cd[d], h], u[d, h].astype(BF16)) for d, h in insts}
        for d, h in insts:
            lane = d * GDN_HEADS + h
            s[d, h] = s[d, h] * refs[d][3][cd[d], lane:lane + 1, :] + r[d, h][CHUNK:CHUNK + GDN_DK]
            sl[d, h] = sl[d, h] * refs[d][6][cd[d], :, lane:lane + 1] + refs[d][4][cd[d], h]

        @pl.when(step >= nctx)
        def _():
            for d, h in insts:
                rows = slice(cd[d] * CHUNK, (cd[d] + 1) * CHUNK)
                refs[d][7][rows, h * GDN_DV:(h + 1) * GDN_DV] = cqs[d, h][CHUNK:2 * CHUNK] + r[d, h][0:CHUNK]
                refs[d][8][rows, h * GLA_DV:(h + 1) * GLA_DV] = ol[d, h]

    for d, h in insts:
        sg_ref[d, h] = s[d, h]
        sl_ref[d, h] = sl[d, h]


def _scan(u0, cq, akd, ggl, m, qd, lgl, nctx, cps):
    b, nchunks = u0.shape[:2]
    assert nctx % cps == 0 and nchunks % cps == 0
    nlat = nchunks - nctx
    ntot, nctx = nchunks // cps, nctx // cps

    def specs(d):
        def chunk(bi, s):
            return (bi, _ext_chunk(d, s, nctx, ntot))

        def big(*tail):
            return pl.BlockSpec((None, cps, None, GDN_HEADS) + tail, lambda bi, s: chunk(bi, s) + (d, 0, 0, 0))

        def small(*tail):
            return pl.BlockSpec((None, cps) + tail, lambda bi, s: chunk(bi, s) + (0, 0))

        return [big(CHUNK, GDN_DV), big(2 * CHUNK, GDN_DK), big(CHUNK + GDN_DK, CHUNK), small(SUBLANE, LANE),
                big(GLA_DK, GLA_DV), big(CHUNK, GLA_DK), small(GLA_DK, LANE)]

    def out_spec(d, width):
        return pl.BlockSpec((None, cps * CHUNK, width), lambda bi, s: (bi, _lat_chunk(d, s, nctx, ntot), 0))

    sds = jax.ShapeDtypeStruct
    return pl.pallas_call(
        functools.partial(_scan_kernel, nctx=nctx),
        grid=(b, ntot),
        in_specs=specs(0) + specs(1),
        out_specs=[out_spec(0, GDN_V), out_spec(1, GDN_V), out_spec(0, GLA_V), out_spec(1, GLA_V)],
        out_shape=[sds((b, nlat * CHUNK, GDN_V), F32)] * 2 + [sds((b, nlat * CHUNK, GLA_V), F32)] * 2,
        scratch_shapes=[pltpu.VMEM((2, GDN_HEADS, GDN_DK, GDN_DV), F32),
                        pltpu.VMEM((2, GLA_HEADS, GLA_DK, GLA_DV), F32)],
        compiler_params=_params("parallel", "arbitrary"),
        name="scan",
    )(u0, cq, akd, ggl, m, qd, lgl, u0, cq, akd, ggl, m, qd, lgl)


def _gla_pre_kernel(q_ref, k_ref, v_ref, la_ref, m_ref, qd_ref, gl_ref, intra_ref, *, cpb):
    tri = [_tri_masks(d) for d in range(2)]
    for c in range(cpb):
        rows = slice(c * CHUNK, (c + 1) * CHUNK)
        q = q_ref[rows, :] * (GLA_DK ** -0.5)
        k = k_ref[rows, :]
        v = v_ref[rows, :]
        la = la_ref[rows, :]
        pref = _cumsum_rows(tri[0][0], la)
        tot = jnp.sum(la, axis=0, keepdims=True)
        lanes = lax.broadcasted_iota(jnp.int32, la.shape, 1)
        bwd_lane = lanes >= GLA_QK
        bc = jnp.where(bwd_lane, tot - pref + la, pref)
        mid = CHUNK // 2
        ref = jnp.where(bwd_lane[0:1, :], bc[CHUNK - 1 - mid:CHUNK - mid, :], bc[mid:mid + 1, :])
        q2 = jnp.concatenate([q, q], axis=1)
        k2 = jnp.concatenate([k, k], axis=1)
        qa = q2 * jnp.exp(bc - ref)
        ka = k2 * jnp.exp(ref - bc)
        qd = (q2 * jnp.exp(bc)).astype(BF16)
        kdt = (k2 * jnp.exp(tot - bc)).T
        glcol = jnp.exp(jnp.sum(la.T, axis=1, keepdims=True))
        out_lanes = lax.broadcasted_iota(jnp.int32, (GLA_DK, LANE), 1)
        gl = jnp.zeros((GLA_DK, LANE), F32)
        for h in range(GLA_HEADS):
            vh = v[:, h * GLA_DV:(h + 1) * GLA_DV]
            intra = None
            for d in range(2):
                cs = slice(d * GLA_QK + h * GLA_DK, d * GLA_QK + (h + 1) * GLA_DK)
                attn = jnp.where(tri[d][0], _dot_nt(qa[:, cs], ka[:, cs]), 0.0)
                im = _dot(jnp.concatenate([attn, kdt[cs, :]], axis=0), vh)
                intra = im[0:CHUNK] if intra is None else intra + im[0:CHUNK]
                m_ref[c, d, h] = im[CHUNK:CHUNK + GLA_DK]
                qd_ref[c, d, h] = qd[:, cs]
                gl = jnp.where(out_lanes == d * GLA_HEADS + h, glcol[cs, :], gl)
            intra_ref[rows, h * GLA_DV:(h + 1) * GLA_DV] = intra
        gl_ref[c] = gl


def _gla_pre(rest, loga, cpb):
    b, lext, _ = rest.shape
    ntot = lext // CHUNK
    rows = cpb * CHUNK
    lead = (b, ntot, 2, GLA_HEADS)

    def col(colblk):
        return lambda bi, i: (bi, i, colblk)

    def blk(*tail):
        return pl.BlockSpec((None, cpb, 2, GLA_HEADS) + tail, lambda bi, i: (bi, i, 0, 0, 0, 0))

    sds = jax.ShapeDtypeStruct
    return pl.pallas_call(
        functools.partial(_gla_pre_kernel, cpb=cpb),
        grid=(b, ntot // cpb),
        in_specs=[pl.BlockSpec((None, rows, GLA_QK), col((COL_LQ - COL_GZ) // GLA_QK)),
                  pl.BlockSpec((None, rows, GLA_QK), col((COL_LK - COL_GZ) // GLA_QK)),
                  pl.BlockSpec((None, rows, GLA_V), col((COL_LV - COL_GZ) // GLA_V)),
                  pl.BlockSpec((None, rows, 2 * GLA_QK), col(0))],
        out_specs=[blk(GLA_DK, GLA_DV), blk(CHUNK, GLA_DK),
                   pl.BlockSpec((None, cpb, GLA_DK, LANE), lambda bi, i: (bi, i, 0, 0)),
                   pl.BlockSpec((None, rows, GLA_V), col(0))],
        out_shape=[sds(lead + (GLA_DK, GLA_DV), F32), sds(lead + (CHUNK, GLA_DK), BF16),
                   sds((b, ntot, GLA_DK, LANE), F32), sds((b, lext, GLA_V), F32)],
        compiler_params=_params("parallel", "parallel"),
        name="gla_pre",
    )(rest, rest, rest, loga)


def _head_gate(o, gain, z, heads, dv):
    parts = []
    for h in range(heads):
        sl = slice(h * dv, (h + 1) * dv)
        oh = o[:, sl]
        parts.append(oh * lax.rsqrt(jnp.mean(oh * oh, axis=-1, keepdims=True) + EPS) * gain * _silu(z[:, sl]))
    return jnp.concatenate(parts, axis=-1)


def _ab_mix(refs, w_ref):
    gf_ref, gb_ref, lf_ref, lb_ref, li_ref, gz_ref, lg_ref, gn_ref, ln_ref = refs
    yg = _head_gate(gf_ref[...] + gb_ref[...], gn_ref[...], gz_ref[0], GDN_HEADS, GDN_DV)
    yl = _head_gate(lf_ref[...] + lb_ref[...] + li_ref[0], ln_ref[...], lg_ref[0], GLA_HEADS, GLA_DV)
    return (jnp.dot(yg.astype(BF16), w_ref[0:GDN_V, :], preferred_element_type=F32)
            + jnp.dot(yl.astype(BF16), w_ref[GDN_V:GDN_V + GLA_V, :], preferred_element_type=F32))


def _hy_mix(refs, w_ref):
    return jnp.dot(refs[0][...].astype(BF16), w_ref[...], preferred_element_type=F32)


def _tail_kernel(*refs, n_mix, mix_fn, nsplit, final_norm):
    mix_refs = refs[:n_mix]
    wo_ref, h_ref, g1_ref, gn_ref, sh_ref, sc_ref, g2_ref, w1_ref, w3_ref, w2_ref, fn_ref, o_ref = refs[n_mix:]
    h = h_ref[...] + g1_ref[...] * mix_fn(mix_refs, wo_ref)
    a = _modulated(h, gn_ref[...], sh_ref[...], sc_ref[...]).astype(BF16)
    fc = w1_ref.shape[1] // nsplit
    acc = jnp.zeros(h.shape, F32)
    for j in range(nsplit):
        u1 = jnp.dot(a, w1_ref[:, j * fc:(j + 1) * fc], preferred_element_type=F32)
        u3 = jnp.dot(a, w3_ref[:, j * fc:(j + 1) * fc], preferred_element_type=F32)
        acc = acc + jnp.dot((_silu(u1) * u3).astype(BF16), w2_ref[j * fc:(j + 1) * fc, :],
                            preferred_element_type=F32)
    y = h + g2_ref[...] * acc
    if final_norm:
        y = y * lax.rsqrt(jnp.mean(y * y, axis=-1, keepdims=True) + EPS) * fn_ref[...]
    o_ref[...] = y


def _tail(mix_args, mix_specs, mix_fn, w_out, h, g1, gain, shift, scale, g2, w1, w3, w2, final_w, tm, final_norm):
    b, l, d = h.shape
    dff = w1.shape[1]

    def tok(bi, i):
        return (bi, i, 0)

    def vec(bi, i):
        return (bi, 0, 0)

    def const(bi, i):
        return (0, 0)

    return pl.pallas_call(
        functools.partial(_tail_kernel, n_mix=len(mix_args), mix_fn=mix_fn, nsplit=2, final_norm=final_norm),
        grid=(b, l // tm),
        in_specs=list(mix_specs) + [
            _resident(w_out.shape, const), pl.BlockSpec((None, tm, d), tok), pl.BlockSpec((None, 1, d), vec),
            _resident((1, d), const), pl.BlockSpec((None, 1, d), vec), pl.BlockSpec((None, 1, d), vec),
            pl.BlockSpec((None, 1, d), vec), _resident((d, dff), const), _resident((d, dff), const),
            _resident((dff, d), const), _resident((1, d), const)],
        out_specs=pl.BlockSpec((None, tm, d), tok),
        out_shape=jax.ShapeDtypeStruct((b, l, d), F32),
        compiler_params=_params("parallel", "parallel"),
        name="mix_out_ffn",
    )(*mix_args, w_out, h, g1, gain.reshape(1, d), shift, scale, g2, w1, w3, w2, final_w.reshape(1, d))


def _hy_hidden_kernel(fb_ref, w1t_ref, w1c_ref, w1s_ref, b1_ref, w2_ref, b2_ref, w3_ref, b3_ref, fr_ref, o_ref,
                      *, tile, seq_len):
    rows = pl.program_id(0) * tile + lax.broadcasted_iota(jnp.int32, (tile, 1), 0)
    pos = jnp.where(rows < seq_len, rows, 2 * seq_len - rows).astype(F32)
    t = pos * (1.0 / (seq_len - 1))
    ang = (pos * (2.0 * math.pi / seq_len)) * fb_ref[...]
    fr = fr_ref[...]
    pre = t * w1t_ref[...] + _dot_hi(jnp.cos(ang), w1c_ref[...]) - _dot_hi(jnp.sin(ang), w1s_ref[...])
    hdn = jnp.sin(fr * (pre + b1_ref[...]))
    hdn = jnp.sin(fr * (_dot_hi(hdn, w2_ref[...]) + b2_ref[...]))
    o_ref[...] = jnp.sin(fr * (_dot_hi(hdn, w3_ref[...]) + b3_ref[...]))


def _hy_hidden(w1, b1, w2, b2, w3, b3, freq, seq_len, tile):
    hid = w1.shape[1]
    bands = jnp.linspace(1e-4, HY_BANDS - 1, HY_BANDS, dtype=F32)
    fb = jnp.zeros((1, LANE), F32).at[0, :HY_BANDS].set(bands)
    w1c = jnp.zeros((LANE, hid), F32).at[:HY_BANDS].set(w1[1:1 + HY_BANDS])
    w1s = jnp.zeros((LANE, hid), F32).at[:HY_BANDS].set(w1[1 + HY_BANDS:1 + 2 * HY_BANDS])
    args = (fb, w1[0:1], w1c, w1s, b1.reshape(1, hid), w2, b2.reshape(1, hid), w3, b3.reshape(1, hid),
            freq.reshape(1, hid))
    return pl.pallas_call(
        functools.partial(_hy_hidden_kernel, tile=tile, seq_len=seq_len),
        grid=(2 * seq_len // tile,),
        in_specs=[pl.BlockSpec(a.shape, lambda i: (0, 0)) for a in args],
        out_specs=pl.BlockSpec((tile, hid), lambda i: (i, 0)),
        out_shape=jax.ShapeDtypeStruct((2 * seq_len, hid), F32),
        compiler_params=_params("parallel"),
        name="hy_hidden",
    )(*args)


def _fft_matrices(r):
    n = r * r
    idx = jnp.arange(r, dtype=jnp.int32)

    def cis(m, period):
        ang = (2.0 * math.pi / period) * m.astype(F32)
        return jnp.cos(ang), -jnp.sin(ang)

    fr, fi = cis((idx[:, None] * idx[None, :]) % r, r)
    half = r // 2
    f1_data = jnp.block([[fr[:, :half], -fi[:, :half]], [fi[:, :half], fr[:, :half]]])
    f1_real = jnp.concatenate([fr, fi], axis=0)
    f3 = jnp.block([[fr[:half, :], fi[:half, :]], [-fi[:half, :], fr[:half, :]]]) * (1.0 / n)
    klo = idx[:, None, None]
    khi = idx[None, :, None]
    bb = idx[None, None, :]
    gr, gi = cis((bb * klo + r * bb * khi) % n, n)
    g = jnp.concatenate([jnp.concatenate([gr, -gi], axis=2), jnp.concatenate([gi, gr], axis=2)], axis=1)
    gh = jnp.swapaxes(g, 1, 2)
    return f1_data.astype(BF16), f1_real.astype(BF16), f3.astype(BF16), g.astype(BF16), gh.astype(BF16)


FFT_BG = 16
FFT_DT = 256


def _slab(ref, j):
    return jnp.concatenate([ref[i, :, j, :] for i in range(ref.shape[0])], axis=0)


def _spread_rows(scr, j, bg, val):
    for lg in range(val.shape[1] // LANE):
        scr[lg, pl.ds(j, val.shape[0], stride=bg), :] = val[:, lg * LANE:(lg + 1) * LANE]


def _flush(scr, o_ref):
    for lg in range(scr.shape[0]):
        o_ref[..., lg * LANE:(lg + 1) * LANE] = scr[lg].reshape(o_ref.shape[:-1] + (LANE,)).astype(o_ref.dtype)


def _time_block(arr, sel, n):
    lead = arr.ndim - 4
    return pl.BlockSpec((None,) * lead + (n, arr.shape[-3], FFT_BG, FFT_DT),
                        lambda p, g, dd: tuple(sel) + (p, 0, g, dd))


def _fft_stage1_kernel(f_ref, x_ref, o_ref, ps):
    bg = x_ref.shape[2]
    for j in range(bg):
        p = jnp.dot(f_ref[...], _slab(x_ref, j).astype(BF16), preferred_element_type=F32)
        _spread_rows(ps, j, bg, p)
    _flush(ps, o_ref)


def _fft_stage1(f1, x5, sel, n):
    r2, _ = f1.shape
    r = r2 // 2
    items, _, rb, d = x5.shape[-4:]
    groups = items // n
    return pl.pallas_call(
        _fft_stage1_kernel,
        grid=(groups, rb // FFT_BG, d // FFT_DT),
        in_specs=[_resident(f1.shape, lambda p, g, dd: (0, 0)), _time_block(x5, sel, n)],
        out_specs=pl.BlockSpec((None, 2, r, FFT_BG, FFT_DT), lambda p, g, dd: (p, 0, 0, g, dd)),
        out_shape=jax.ShapeDtypeStruct((groups, 2, r, rb, d), BF16),
        scratch_shapes=[pltpu.VMEM((FFT_DT // LANE, r2 * FFT_BG, LANE), F32)],
        compiler_params=_params("parallel", "parallel", "parallel"),
        name="fft_stage1",
    )(f1, x5)


def _fft_filter_stage1_kernel(f_ref, hdn_ref, fo_ref, dl_ref, o_ref, ps, *, seq_len):
    r, bg, _ = hdn_ref.shape
    orders = fo_ref.shape[1]
    half = r // 2
    a_idx = lax.broadcasted_iota(jnp.int32, (r, 1), 0)
    for j in range(bg):
        rows = a_idx * r + (pl.program_id(0) * bg + j)
        pos = jnp.where(rows < seq_len, rows, 2 * seq_len - rows).astype(F32)
        t = pos * (1.0 / (seq_len - 1))
        window = jnp.where(rows != seq_len, jnp.exp(-t * dl_ref[...]) + HY_SHIFT, 0.0)
        hj = hdn_ref[:, j, :]
        for n in range(orders):
            taps = jnp.concatenate([_dot(hj[:half], fo_ref[0, n]), _dot(hj[half:], fo_ref[1, n])], axis=0) * window
            p = jnp.dot(f_ref[...], taps.astype(BF16), preferred_element_type=F32)
            _spread_rows(ps.at[n], j, bg, p)
    for n in range(orders):
        _flush(ps.at[n], o_ref.at[n])


def _fft_filter_stage1(f1, hdn3, fo, deltas, seq_len):
    r2, r = f1.shape
    hid = hdn3.shape[-1]
    _, orders, _, d = fo.shape
    lanes = FFT_DT // LANE
    return pl.pallas_call(
        functools.partial(_fft_filter_stage1_kernel, seq_len=seq_len),
        grid=(r // FFT_BG, d // FFT_DT),
        in_specs=[_resident(f1.shape, lambda g, dd: (0, 0)),
                  pl.BlockSpec((r, FFT_BG, hid), lambda g, dd: (0, g, 0)),
                  pl.BlockSpec((2, orders, hid, FFT_DT), lambda g, dd: (0, 0, 0, dd)),
                  pl.BlockSpec((1, FFT_DT), lambda g, dd: (0, dd))],
        out_specs=pl.BlockSpec((orders, 2, r, FFT_BG, FFT_DT), lambda g, dd: (0, 0, 0, g, dd)),
        out_shape=jax.ShapeDtypeStruct((orders, 2, r, r, d), BF16),
        scratch_shapes=[pltpu.VMEM((orders, lanes, r2 * FFT_BG, LANE), F32)],
        compiler_params=_params("parallel", "parallel"),
        name="fft_filter_stage1",
    )(f1, hdn3, fo, deltas)


def _fft_stage2_kernel(g_ref, gh_ref, k_ref, p_ref, o_ref):
    r = p_ref.shape[2]
    kq = k_ref[...]
    h = jnp.dot(g_ref[...], kq.reshape(2 * r, kq.shape[2]), preferred_element_type=F32)
    hr, hi = h[:r], h[r:]
    for n in range(p_ref.shape[0]):
        p = p_ref[n]
        x = jnp.dot(g_ref[...], p.reshape(2 * r, p.shape[2]), preferred_element_type=F32)
        xr, xi = x[:r], x[r:]
        y = jnp.concatenate([xr * hr - xi * hi, xr * hi + xi * hr], axis=0)
        q = jnp.dot(gh_ref[...], y.astype(BF16), preferred_element_type=F32)
        o_ref[n] = q.reshape(o_ref.shape[1:]).astype(o_ref.dtype)


def _fft_stage2(g, gh, pk, order, p5):
    pairs, _, r, _, d = p5.shape
    return pl.pallas_call(
        _fft_stage2_kernel,
        grid=(r,),
        in_specs=[pl.BlockSpec((None, 2 * r, 2 * r), lambda kk: (kk, 0, 0)),
                  pl.BlockSpec((None, 2 * r, 2 * r), lambda kk: (kk, 0, 0)),
                  pl.BlockSpec((None, 2, None, r, d), lambda kk: (order, 0, kk, 0, 0)),
                  pl.BlockSpec((pairs, 2, None, r, d), lambda kk: (0, 0, kk, 0, 0))],
        out_specs=pl.BlockSpec((pairs, 2, None, r, d), lambda kk: (0, 0, kk, 0, 0)),
        out_shape=jax.ShapeDtypeStruct(p5.shape, BF16),
        compiler_params=_params("parallel"),
        name="fft_stage2",
    )(g, gh, pk, p5)


def _fft_stage3_kernel(f3_ref, f1_ref, q_ref, z_ref, gate_ref, skip_ref, *rest, chain):
    if chain:
        zo_ref, po_ref, qs, zs, ps = rest
    else:
        zo_ref, qs, zs = rest
    bg = z_ref.shape[2]
    rows = q_ref.shape[0] * q_ref.shape[1]
    for lg in range(qs.shape[0]):
        qs[lg] = q_ref[..., lg * LANE:(lg + 1) * LANE].astype(F32).reshape(rows * bg, LANE)
    skip = skip_ref[...]
    for j in range(bg):
        qj = jnp.concatenate([qs[lg, pl.ds(j, rows, stride=bg), :] for lg in range(qs.shape[0])], axis=1)
        y = jnp.dot(f3_ref[...], qj.astype(BF16), preferred_element_type=F32)
        zn = _slab(gate_ref, j) * (y + _slab(z_ref, j) * skip)
        _spread_rows(zs, j, bg, zn)
        if chain:
            _spread_rows(ps, j, bg, jnp.dot(f1_ref[...], zn.astype(BF16), preferred_element_type=F32))
    _flush(zs, zo_ref)
    if chain:
        _flush(ps, po_ref)


def _fft_stage3(f3, f1, q5, z5, z_sel, g5, g_sel, skip, chain):
    pairs, _, r, rb, d = q5.shape
    a = z5.shape[-3]
    sds = jax.ShapeDtypeStruct
    out_specs = [pl.BlockSpec((2, a, FFT_BG, FFT_DT), lambda p, g, dd: (p, 0, g, dd))]
    out_shape = [sds((2 * pairs, a, rb, d), F32)]
    lanes = FFT_DT // LANE
    scratch = [pltpu.VMEM((lanes, 2 * r * FFT_BG, LANE), F32), pltpu.VMEM((lanes, 2 * a * FFT_BG, LANE), F32)]
    if chain:
        out_specs.append(pl.BlockSpec((None, 2, r, FFT_BG, FFT_DT), lambda p, g, dd: (p, 0, 0, g, dd)))
        out_shape.append(sds(q5.shape, BF16))
        scratch.append(pltpu.VMEM((lanes, 2 * r * FFT_BG, LANE), F32))
    return pl.pallas_call(
        functools.partial(_fft_stage3_kernel, chain=chain),
        grid=(pairs, rb // FFT_BG, d // FFT_DT),
        in_specs=[_resident(f3.shape, lambda p, g, dd: (0, 0)), _resident(f1.shape, lambda p, g, dd: (0, 0)),
                  pl.BlockSpec((None, 2, r, FFT_BG, FFT_DT), lambda p, g, dd: (p, 0, 0, g, dd)),
                  _time_block(z5, z_sel, 2), _time_block(g5, g_sel, 2),
                  pl.BlockSpec((1, FFT_DT), lambda p, g, dd: (0, dd))],
        out_specs=out_specs,
        out_shape=out_shape,
        scratch_shapes=scratch,
        compiler_params=_params("parallel", "parallel", "parallel"),
        name="fft_stage3",
    )(f3, f1, q5, z5, g5, skip)


TAIL_ROWS = 512
HY_FRONT_ROWS = 512
AB_FRONT_ROWS = 384


def _regroup_ab_weights(w_in):
    d = w_in.shape[0]
    sizes = (GDN_QK, GDN_QK, GDN_V, GDN_V, 2 * GDN_HEADS, 2 * GDN_HEADS, GLA_QK, GLA_QK, GLA_V, GLA_V,
             2 * GLA_GATE_RANK)
    offs = [0]
    for s in sizes:
        offs.append(offs[-1] + s)
    gq, gk, gv, gz, ga, gb, lq, lk, lv, lg, llr = (w_in[:, offs[i]:offs[i + 1]] for i in range(len(sizes)))
    pad = jnp.zeros((d, LANE - (ga.shape[1] + gb.shape[1] + llr.shape[1])), w_in.dtype)
    return jnp.concatenate([gq, gk, gv, gz, lq, lk, lv, lg, ga, gb, llr, pad], axis=1)


def _lane_row(vals, offset):
    flat = vals.reshape(-1).astype(F32)
    return jnp.zeros((1, LANE), F32).at[0, offset:offset + flat.shape[0]].set(flat)


def _even_layer(h, ctx, mods, norm1, norm2, w_in, conv_w, a_log, dt_bias, gdn_norm, gate_w, gate_b, gla_norm,
                w_out, w1, w3, w2, final_w, final_norm):
    b, l, d = h.shape
    lctx = ctx.shape[1]
    tm = TAIL_ROWS
    sh1, sc1, g1, sh2, sc2, g2 = mods["lat"]
    csh1, csc1 = mods["ctx"][0], mods["ctx"][1]
    xe = jnp.concatenate([ctx, h], axis=1)
    shift = jnp.stack([jnp.broadcast_to(csh1, (b, d)), sh1], axis=1)[:, :, None, :]
    scale = jnp.stack([jnp.broadcast_to(csc1, (b, d)), sc1], axis=1)[:, :, None, :]
    gw = jnp.zeros((LANE, 2 * GLA_QK), F32)
    for dd in range(2):
        gw = gw.at[SM_LR + dd * GLA_GATE_RANK:SM_LR + (dd + 1) * GLA_GATE_RANK,
                   dd * GLA_QK:(dd + 1) * GLA_QK].set(gate_w[dd])
    q, k, v, sm, loga, rest = _ab_front(xe, norm1, shift, scale, _regroup_ab_weights(w_in).astype(BF16), conv_w,
                                        _lane_row(a_log, SM_G), _lane_row(dt_bias, SM_G), gw,
                                        gate_b.reshape(1, 2 * GLA_QK), AB_FRONT_ROWS, lctx)
    nctx = lctx // CHUNK
    u0, cq, akd, ggl = _gdn_pre(q, k, v, sm, 2)
    m, qd, lgl, ol_intra = _gla_pre(rest, loga, 2)
    og_f, og_b, ol_f, ol_b = _scan(u0, cq, akd, ggl, m, qd, lgl, nctx, 4)
    def lat(width):
        return pl.BlockSpec((None, tm, width), lambda bi, i: (bi, i, 0))

    def ext(width, colblk):
        return pl.BlockSpec((pl.Element(1), pl.Element(tm), pl.Element(width)),
                            lambda bi, i: (bi, SUBLANE * (lctx // SUBLANE + i * (tm // SUBLANE)), colblk * width))

    mix_args = (og_f, og_b, ol_f, ol_b, ol_intra, rest, rest, gdn_norm.reshape(1, GDN_DV), gla_norm.reshape(1, GLA_DV))
    mix_specs = [lat(GDN_V), lat(GDN_V), lat(GLA_V), lat(GLA_V), ext(GLA_V, 0), ext(GDN_V, 0),
                 ext(GLA_V, (COL_LG - COL_GZ) // GLA_V), _resident((1, GDN_DV), lambda bi, i: (0, 0)),
                 _resident((1, GLA_DV), lambda bi, i: (0, 0))]
    return _tail(mix_args, mix_specs, _ab_mix, w_out.astype(BF16), h, g1[:, None, :], norm2, sh2[:, None, :],
                 sc2[:, None, :], g2[:, None, :], w1.astype(BF16), w3.astype(BF16), w2.astype(BF16), final_w, tm,
                 final_norm)


def _hyena_layer(h, mods, norm1, norm2, w_in, conv_w, pw1, pb1, pw2, pb2, pw3, pb3, freq, filt_out, skip, w_out,
                 w1, w3, w2, final_w, final_norm):
    b, l, d = h.shape
    tm = TAIL_ROWS
    sh1, sc1, g1, sh2, sc2, g2 = mods["lat"]
    u3 = _hy_front(h, norm1, sh1[:, None, :], sc1[:, None, :], w_in.astype(BF16), conv_w, HY_FRONT_ROWS)

    n2 = 2 * l
    r = math.isqrt(n2)
    assert r * r == n2 and b % 2 == 0
    orders = skip.shape[0]
    hdn = _hy_hidden(pw1, pb1, pw2, pb2, pw3, pb3, freq, l, 512)
    fo = jnp.transpose(filt_out.reshape(filt_out.shape[0], 2, orders, d), (1, 2, 0, 3))
    deltas = jnp.abs(jnp.linspace(HY_MIN_DECAY, HY_MAX_DECAY, d, dtype=F32))[None, :]

    f1_data, f1_real, f3, g, gh = _fft_matrices(r)
    pk = _fft_filter_stage1(f1_real, hdn.reshape(r, r, hdn.shape[1]), fo, deltas, l)

    u5 = u3.reshape(3, b, r // 2, r, d)
    z_arr, z_sel = u5, (0,)
    p = _fft_stage1(f1_data, z_arr, z_sel, 2)
    for n in range(orders):
        q = _fft_stage2(g, gh, pk, n, p)
        chain = n + 1 < orders
        res = _fft_stage3(f3, f1_data, q, z_arr, z_sel, u5, (n + 1,), skip[n][None, :], chain)
        z_arr, z_sel = res[0], ()
        if chain:
            p = res[1]
    mix_specs = [pl.BlockSpec((None, tm, d), lambda bi, i: (bi, i, 0))]
    return _tail((z_arr.reshape(b, l, d),), mix_specs, _hy_mix, w_out.astype(BF16), h, g1[:, None, :], norm2,
                 sh2[:, None, :], sc2[:, None, :], g2[:, None, :], w1.astype(BF16), w3.astype(BF16), w2.astype(BF16),
                 final_w, tm, final_norm)


def kernel(x, c, ctx, c_ctx, mod_w, mod_b, norm1_w, norm2_w, ab_w_in, ab_conv_w, gdn_a_log, gdn_dt_bias, gdn_norm_w, gla_gate_w, gla_gate_b, gla_norm_w, ab_w_out, hy_w_in, hy_conv_w, hy_pos_w1, hy_pos_b1, hy_pos_w2, hy_pos_b2, hy_pos_w3, hy_pos_b3, hy_freq, hy_filt_out, hy_skip, hy_w_out, ffn_w1, ffn_w3, ffn_w2, final_norm_w):
    b, l, d = x.shape
    depth = mod_w.shape[0]
    assert b + 1 <= SUBLANE
    cvec = jnp.concatenate([c, c_ctx[None, :], jnp.zeros((SUBLANE - b - 1, d), F32)], axis=0)
    mod = _modulation(cvec, mod_w, mod_b)
    h = x
    for i in range(depth):
        mods = {"lat": [mod[i, :b, j * d:(j + 1) * d] for j in range(6)],
                "ctx": [mod[i, b, j * d:(j + 1) * d] for j in range(6)]}
        last = i == depth - 1
        if i % 2 == 0:
            assert not any(j % 2 == 0 for j in range(i + 1, depth))
            e = i // 2
            h = _even_layer(h, ctx, mods, norm1_w[i], norm2_w[i], ab_w_in[e], ab_conv_w[e], gdn_a_log[e],
                            gdn_dt_bias[e], gdn_norm_w[e], gla_gate_w[e], gla_gate_b[e], gla_norm_w[e], ab_w_out[e],
                            ffn_w1[i], ffn_w3[i], ffn_w2[i], final_norm_w, last)
        else:
            o = i // 2
            h = _hyena_layer(h, mods, norm1_w[i], norm2_w[i], hy_w_in[o], hy_conv_w[o], hy_pos_w1[o], hy_pos_b1[o],
                             hy_pos_w2[o], hy_pos_b2[o], hy_pos_w3[o], hy_pos_b3[o], hy_freq[o], hy_filt_out[o],
                             hy_skip[o], hy_w_out[o], ffn_w1[i], ffn_w3[i], ffn_w2[i], final_norm_w, last)
    return h
```

```python
import functools
import math

import jax
import jax.numpy as jnp
from jax import lax
from jax.experimental import pallas as pl
from jax.experimental.pallas import tpu as pltpu

F32 = jnp.float32
BF16 = jnp.bfloat16
HI = lax.Precision.HIGHEST

EPS = 1e-6
CHUNK = 64
GDN_HEADS, GDN_DK, GDN_DV = 4, 128, 128
GLA_HEADS, GLA_DK, GLA_DV = 4, 64, 128
GLA_GATE_RANK = 16
GLA_GATE_TAU = 16.0
GDN_QK = GDN_HEADS * GDN_DK
GDN_V = GDN_HEADS * GDN_DV
GLA_QK = GLA_HEADS * GLA_DK
GLA_V = GLA_HEADS * GLA_DV
HY_EMB = 33
HY_BANDS = (HY_EMB - 1) // 2
HY_MIN_DECAY = math.log(1e-2) / 1.5
HY_MAX_DECAY = math.log(1e-2) / 0.3
HY_SHIFT = 0.05

LANE = 128
SUBLANE = 8
VMEM_LIMIT_BYTES = 56 * 1024 * 1024

COL_GQ, COL_GK, COL_GV, COL_GZ = 0, 512, 1024, 1536
COL_LQ, COL_LK, COL_LV, COL_LG = 2048, 2304, 2560, 3072
COL_SMALL = 3584
SM_G, SM_B, SM_LR = 0, 8, 16


def _params(*sem):
    return pltpu.CompilerParams(dimension_semantics=sem, vmem_limit_bytes=VMEM_LIMIT_BYTES)


def _resident(shape, index_map):
    return pl.BlockSpec(shape, index_map, pipeline_mode=pl.Buffered(1))


def _dot(a, b):
    return jnp.dot(a.astype(BF16), b.astype(BF16), preferred_element_type=F32)


def _dot_nt(a, b):
    return lax.dot_general(a.astype(BF16), b.astype(BF16), (((1,), (1,)), ((), ())),
                           preferred_element_type=F32)


def _dot_hi(a, b):
    return jnp.dot(a, b, preferred_element_type=F32, precision=HI)


def _sigmoid(x):
    return 1.0 / (1.0 + jnp.exp(-x))


def _silu(x):
    return x * _sigmoid(x)


def _softplus(x):
    return jnp.maximum(x, 0.0) + jnp.log(1.0 + jnp.exp(-jnp.abs(x)))


def _mod_kernel(c_ref, w_ref, b_ref, o_ref):
    o_ref[...] = _dot_hi(_silu(c_ref[...]), w_ref[...]) + b_ref[...]


def _modulation(cvec, mod_w, mod_b):
    depth, d, n = mod_w.shape
    tn = n // 4
    return pl.pallas_call(
        _mod_kernel,
        grid=(depth, n // tn),
        in_specs=[pl.BlockSpec((SUBLANE, d), lambda i, j: (0, 0)),
                  pl.BlockSpec((None, d, tn), lambda i, j: (i, 0, j)),
                  pl.BlockSpec((None, 1, tn), lambda i, j: (i, 0, j))],
        out_specs=pl.BlockSpec((None, SUBLANE, tn), lambda i, j: (i, 0, j)),
        out_shape=jax.ShapeDtypeStruct((depth, SUBLANE, n), F32),
        compiler_params=_params("parallel", "parallel"),
        name="modulation",
    )(cvec, mod_w, mod_b.reshape(depth, 1, n))


CONV_HALO = SUBLANE


def _modulated(x, gain, shift, scale):
    ms = jnp.mean(x * x, axis=-1, keepdims=True)
    return x * lax.rsqrt(ms + EPS) * gain * (1.0 + scale) + shift


def _window_start(i, tm, win, seq_len):
    return SUBLANE * jnp.clip(i * (tm // SUBLANE) - CONV_HALO // SUBLANE, 0, (seq_len - win) // SUBLANE)


def _window_spec(tm, win, seq_len, d):
    return pl.BlockSpec((pl.Element(1), pl.Element(win), pl.Element(d)),
                        lambda bi, i: (bi, _window_start(i, tm, win, seq_len), 0))


def _conv3_rows(u, w, gidx, starts, ends):
    rows = u.shape[0]
    no_prev = functools.reduce(jnp.logical_or, [gidx == r for r in starts])
    no_next = functools.reduce(jnp.logical_or, [gidx == r for r in ends])
    prev = jnp.where(no_prev, 0.0, pltpu.roll(u, 1, 0))
    nxt = jnp.where(no_next, 0.0, pltpu.roll(u, rows - 1, 0))
    return prev * w[0:1, :] + u * w[1:2, :] + nxt * w[2:3, :]


def _hy_front_kernel(x_ref, g_ref, sh_ref, sc_ref, w_ref, cw_ref, o_ref, scr, *, tm, seq_len):
    i = pl.program_id(1)
    win = x_ref.shape[1]
    ws = _window_start(i, tm, win, seq_len)
    off = pl.multiple_of(i * tm - ws, SUBLANE)
    a = _modulated(x_ref[0], g_ref[...], sh_ref[...], sc_ref[...])
    u = jnp.dot(a.astype(BF16), w_ref[...], preferred_element_type=F32)
    gidx = ws + lax.broadcasted_iota(jnp.int32, (win, 1), 0)
    scr[...] = _conv3_rows(u, cw_ref[...], gidx, (0,), (seq_len - 1,))
    d = o_ref.shape[-1]
    for g in range(o_ref.shape[0]):
        o_ref[g] = scr[pl.ds(off, tm), g * d:(g + 1) * d]


def _hy_front(x, gain, shift, scale, w, conv_w, tm):
    b, l, d = x.shape
    n = w.shape[1]
    groups = n // d
    win = tm + 2 * CONV_HALO

    def vec(bi, i):
        return (bi, 0, 0)

    return pl.pallas_call(
        functools.partial(_hy_front_kernel, tm=tm, seq_len=l),
        grid=(b, l // tm),
        in_specs=[_window_spec(tm, win, l, d), _resident((1, d), lambda bi, i: (0, 0)),
                  pl.BlockSpec((None, 1, d), vec), pl.BlockSpec((None, 1, d), vec),
                  _resident((d, n), lambda bi, i: (0, 0)), _resident((3, n), lambda bi, i: (0, 0))],
        out_specs=pl.BlockSpec((groups, None, tm, d), lambda bi, i: (0, bi, i, 0)),
        out_shape=jax.ShapeDtypeStruct((groups, b, l, d), F32),
        scratch_shapes=[pltpu.VMEM((win, n), F32)],
        compiler_params=_params("parallel", "parallel"),
        name="hy_front",
    )(x, gain.reshape(1, d), shift, scale, w, conv_w)


def _ab_front_kernel(x_ref, g_ref, sh_ref, sc_ref, w_ref, cw_ref, alog_ref, dtb_ref, gw_ref, gb_ref,
                     q_ref, k_ref, v_ref, so_ref, la_ref, rest_ref, scr, *, tm, lctx, lext):
    i = pl.program_id(1)
    win = x_ref.shape[1]
    ws = _window_start(i, tm, win, lext)
    off = pl.multiple_of(i * tm - ws, SUBLANE)
    gidx = ws + lax.broadcasted_iota(jnp.int32, (win, 1), 0)
    is_ctx = gidx < lctx
    shift = jnp.where(is_ctx, sh_ref[0], sh_ref[1])
    scale = jnp.where(is_ctx, sc_ref[0], sc_ref[1])
    a = _modulated(x_ref[0], g_ref[...], shift, scale)
    u = jnp.dot(a.astype(BF16), w_ref[...], preferred_element_type=F32)
    ncv = cw_ref.shape[1]
    scr[:, 0:ncv] = _conv3_rows(u[:, 0:ncv], cw_ref[...], gidx, (0, lctx), (lctx - 1, lext - 1))
    scr[:, ncv:] = u[:, ncv:]
    rows = pl.ds(off, tm)
    for h in range(GDN_HEADS):
        sl = slice(h * GDN_DK, (h + 1) * GDN_DK)
        qh = _silu(scr[rows, COL_GQ + h * GDN_DK:COL_GQ + (h + 1) * GDN_DK])
        kh = _silu(scr[rows, COL_GK + h * GDN_DK:COL_GK + (h + 1) * GDN_DK])
        q_ref[:, sl] = qh * lax.rsqrt(jnp.sum(qh * qh, axis=-1, keepdims=True) + EPS) * (GDN_DK ** -0.5)
        k_ref[:, sl] = kh * lax.rsqrt(jnp.sum(kh * kh, axis=-1, keepdims=True) + EPS)
    v_ref[...] = _silu(scr[rows, COL_GV:COL_GV + GDN_V])
    rest_ref[...] = scr[rows, COL_GZ:COL_SMALL]
    sm = scr[rows, COL_SMALL:COL_SMALL + LANE]
    g = -jnp.exp(alog_ref[...]) * _softplus(sm + dtb_ref[...])
    beta = _sigmoid(sm)
    lanes = lax.broadcasted_iota(jnp.int32, sm.shape, 1)
    so_ref[...] = jnp.where(lanes < SM_B, g, beta)
    gate = _dot(sm, gw_ref[...]) + gb_ref[...]
    la_ref[...] = (jnp.minimum(gate, 0.0) - jnp.log(1.0 + jnp.exp(-jnp.abs(gate)))) * (1.0 / GLA_GATE_TAU)


def _ab_front(xe, gain, shift, scale, w, conv_w, alog_row, dtb_row, gate_w, gate_b, tm, lctx):
    b, lext, d = xe.shape
    n = w.shape[1]
    win = tm + 2 * CONV_HALO
    nrest = COL_SMALL - COL_GZ

    def const(bi, i):
        return (0, 0)

    def tok(width):
        return pl.BlockSpec((None, tm, width), lambda bi, i: (bi, i, 0))

    sds = jax.ShapeDtypeStruct
    return pl.pallas_call(
        functools.partial(_ab_front_kernel, tm=tm, lctx=lctx, lext=lext),
        grid=(b, lext // tm),
        in_specs=[_window_spec(tm, win, lext, d), _resident((1, d), const),
                  pl.BlockSpec((None, 2, 1, d), lambda bi, i: (bi, 0, 0, 0)),
                  pl.BlockSpec((None, 2, 1, d), lambda bi, i: (bi, 0, 0, 0)),
                  _resident((d, n), const), _resident(conv_w.shape, const),
                  _resident((1, LANE), const), _resident((1, LANE), const),
                  _resident((LANE, 2 * GLA_QK), const), _resident((1, 2 * GLA_QK), const)],
        out_specs=[tok(GDN_QK), tok(GDN_QK), tok(GDN_V), tok(LANE), tok(2 * GLA_QK), tok(nrest)],
        out_shape=[sds((b, lext, GDN_QK), F32), sds((b, lext, GDN_QK), F32), sds((b, lext, GDN_V), F32),
                   sds((b, lext, LANE), F32), sds((b, lext, 2 * GLA_QK), F32), sds((b, lext, nrest), F32)],
        scratch_shapes=[pltpu.VMEM((win, n), F32)],
        compiler_params=_params("parallel", "parallel"),
        name="ab_front",
    )(xe, gain.reshape(1, d), shift, scale, w, conv_w, alog_row, dtb_row, gate_w, gate_b)


def _ext_chunk(d, s, nctx, ntot):
    if d == 0:
        return s
    return jnp.where(s < nctx, nctx - 1 - s, ntot + nctx - 1 - s)


def _lat_chunk(d, s, nctx, ntot):
    nlat = ntot - nctx
    if d == 0:
        return jnp.maximum(s - nctx, 0)
    return jnp.minimum(ntot - 1 - s, nlat - 1)


def _tri_masks(d):
    ii = lax.broadcasted_iota(jnp.int32, (CHUNK, CHUNK), 0)
    jj = lax.broadcasted_iota(jnp.int32, (CHUNK, CHUNK), 1)
    if d == 0:
        return ii >= jj, ii > jj, ii == jj
    return ii <= jj, ii < jj, ii == jj


def _cumsum_rows(incl, x):
    m = incl.astype(BF16)
    x1 = x.astype(BF16)
    r1 = x - x1.astype(F32)
    x2 = r1.astype(BF16)
    x3 = (r1 - x2.astype(F32)).astype(BF16)
    dot = functools.partial(jnp.dot, preferred_element_type=F32)
    return dot(m, x1) + dot(m, x2) + dot(m, x3)


def _merge_masks(d):
    ii = lax.broadcasted_iota(jnp.int32, (CHUNK, CHUNK), 0)
    jj = lax.broadcasted_iota(jnp.int32, (CHUNK, CHUNK), 1)
    lo, hi = (jj, ii) if d == 0 else (ii, jj)
    masks = {}
    s = 1
    while s < CHUNK:
        masks[s] = jnp.logical_and(ii // (2 * s) == jj // (2 * s),
                                   jnp.logical_and((hi // s) % 2 == 1, (lo // s) % 2 == 0))
        s *= 2
    return masks


def _gdn_pre_kernel(q_ref, k_ref, v_ref, sm_ref, u0_ref, cq_ref, akd_ref, gl_ref, *, cpb):
    tri = [_tri_masks(d) for d in range(2)]
    merge = [_merge_masks(d) for d in range(2)]
    insts = [(d, h) for d in range(2) for h in range(GDN_HEADS)]
    for c in range(cpb):
        rows = slice(c * CHUNK, (c + 1) * CHUNK)
        q = q_ref[rows, :]
        k = k_ref[rows, :]
        v = v_ref[rows, :]
        sm = sm_ref[rows, :]
        pref = _cumsum_rows(tri[0][0], sm)
        tot = jnp.sum(sm, axis=0, keepdims=True)
        lanes = lax.broadcasted_iota(jnp.int32, sm.shape, 1)
        bwd_lane = jnp.logical_and(lanes >= SM_G + GDN_HEADS, lanes < SM_G + 2 * GDN_HEADS)
        gc = jnp.where(bwd_lane, tot - pref + sm, pref)
        gct = gc.T
        totc = jnp.sum(sm.T, axis=1, keepdims=True)
        gl_ref[c] = jnp.exp(jnp.broadcast_to(totc[0:SUBLANE, :], (SUBLANE, LANE)))
        egc = jnp.exp(gc)
        etg = jnp.exp(tot - gc)
        kh = [k[:, h * GDN_DK:(h + 1) * GDN_DK] for h in range(GDN_HEADS)]
        qh = [q[:, h * GDN_DK:(h + 1) * GDN_DK] for h in range(GDN_HEADS)]
        vh = [v[:, h * GDN_DV:(h + 1) * GDN_DV] for h in range(GDN_HEADS)]
        gq = [_dot_nt(jnp.concatenate([kh[h], qh[h]], axis=0), kh[h]) for h in range(GDN_HEADS)]
        a, t, beta, ecol = {}, {}, {}, {}
        for d, h in insts:
            incl, strict, diag = tri[d]
            col = SM_G + d * GDN_HEADS + h
            decay = jnp.exp(jnp.where(incl, gc[:, col:col + 1] - gct[col:col + 1, :], -jnp.inf))
            beta[d, h] = sm[:, SM_B + col:SM_B + col + 1]
            ecol[d, h] = egc[:, col:col + 1]
            a[d, h] = jnp.where(strict, gq[h][0:CHUNK] * decay, 0.0) * beta[d, h]
            hl = slice((h % 2) * CHUNK, (h % 2 + 1) * CHUNK)
            akd_ref[c, d, h // 2, 0:CHUNK, hl] = (gq[h][CHUNK:2 * CHUNK] * decay).astype(BF16)
            akd_ref[c, d, h // 2, CHUNK:CHUNK + GDN_DK, hl] = (kh[h] * etg[:, col:col + 1]).T.astype(BF16)
            t[d, h] = diag.astype(F32) - jnp.where(merge[d][1], a[d, h], 0.0)
        s = 2
        while s < CHUNK:
            x = {i: _dot(jnp.where(merge[i[0]][s], a[i], 0.0), t[i]) for i in insts}
            t = {i: t[i] - _dot(t[i], x[i]) for i in insts}
            s *= 2
        for d, h in insts:
            rhs = jnp.concatenate([vh[h] * beta[d, h], kh[h] * (beta[d, h] * ecol[d, h])], axis=1)
            sol = _dot(t[d, h], rhs)
            u0_ref[c, d, h] = sol[:, 0:GDN_DV].astype(BF16)
            cq_ref[c, d, h, 0:CHUNK, :] = sol[:, GDN_DV:GDN_DV + GDN_DK].astype(BF16)
            cq_ref[c, d, h, CHUNK:2 * CHUNK, :] = (qh[h] * ecol[d, h]).astype(BF16)


def _gdn_pre(q, k, v, sm, cpb):
    b, lext, _ = q.shape
    ntot = lext // CHUNK
    rows = cpb * CHUNK
    lead = (b, ntot, 2, GDN_HEADS)

    def tok(bi, i):
        return (bi, i, 0)

    def blk(*tail, heads=GDN_HEADS):
        return pl.BlockSpec((None, cpb, 2, heads) + tail, lambda bi, i: (bi, i, 0, 0, 0, 0))

    sds = jax.ShapeDtypeStruct
    pairs = GDN_HEADS // 2
    return pl.pallas_call(
        functools.partial(_gdn_pre_kernel, cpb=cpb),
        grid=(b, ntot // cpb),
        in_specs=[pl.BlockSpec((None, rows, GDN_QK), tok), pl.BlockSpec((None, rows, GDN_QK), tok),
                  pl.BlockSpec((None, rows, GDN_V), tok), pl.BlockSpec((None, rows, LANE), tok)],
        out_specs=[blk(CHUNK, GDN_DV), blk(2 * CHUNK, GDN_DK), blk(CHUNK + GDN_DK, 2 * CHUNK, heads=pairs),
                   pl.BlockSpec((None, cpb, SUBLANE, LANE), lambda bi, i: (bi, i, 0, 0))],
        out_shape=[sds(lead + (CHUNK, GDN_DV), BF16), sds(lead + (2 * CHUNK, GDN_DK), BF16),
                   sds((b, ntot, 2, pairs, CHUNK + GDN_DK, 2 * CHUNK), BF16), sds((b, ntot, SUBLANE, LANE), F32)],
        compiler_params=_params("parallel", "parallel"),
        name="gdn_pre",
    )(q, k, v, sm)


def _scan_kernel(u0f, cqf, akf, ggf, mf, qdf, lgf, u0b, cqb, akb, ggb, mb, qdb, lgb,
                 gof_ref, gob_ref, lof_ref, lob_ref, sg_ref, sl_ref, *, nctx):
    step = pl.program_id(1)

    @pl.when(step == 0)
    def _():
        sg_ref[...] = jnp.zeros(sg_ref.shape, F32)
        sl_ref[...] = jnp.zeros(sl_ref.shape, F32)

    refs = ((u0f, cqf, akf, ggf, mf, qdf, lgf, gof_ref, lof_ref), (u0b, cqb, akb, ggb, mb, qdb, lgb, gob_ref, lob_ref))
    insts = [(d, h) for d in range(2) for h in range(GDN_HEADS)]
    dot = functools.partial(jnp.dot, preferred_element_type=F32)
    cps = u0f.shape[0]
    s = {i: sg_ref[i[0], i[1]] for i in insts}
    sl = {i: sl_ref[i[0], i[1]] for i in insts}
    def pair_rhs(xa, xb):
        z = jnp.zeros(xa.shape, xa.dtype)
        return jnp.concatenate([jnp.concatenate([xa, z], axis=1), jnp.concatenate([z, xb], axis=1)], axis=0)

    pairs = [(d, p) for d in range(2) for p in range(GDN_HEADS // 2)]
    for ci in range(cps):
        cd = (ci, cps - 1 - ci)
        cqs = {(d, h): dot(refs[d][1][cd[d], h], s[d, h].astype(BF16)) for d, h in insts}
        ol, r = {}, {}
        for d, p in pairs:
            olp = dot(refs[d][5][cd[d], p], pair_rhs(sl[d, 2 * p].astype(BF16), sl[d, 2 * p + 1].astype(BF16)))
            ol[d, 2 * p], ol[d, 2 * p + 1] = olp[:, 0:GLA_DV], olp[:, GLA_DV:2 * GLA_DV]
        u = {(d, h): refs[d][0][cd[d], h].astype(F32) - cqs[d, h][0:CHUNK] for d, h in insts}
        for d, p in pairs:
            rp = dot(refs[d][2][cd[d], p], pair_rhs(u[d, 2 * p].astype(BF16), u[d, 2 * p + 1].astype(BF16)))
            r[d, 2 * p], r[d, 2 * p + 1] = rp[:, 0:GDN_DV], rp[:, GDN_DV:2 * GDN_DV]
        for d, h in insts:
            lane = d * GDN_HEADS + h
            s[d, h] = s[d, h] * refs[d][3][cd[d], lane:lane + 1, :] + r[d, h][CHUNK:CHUNK + GDN_DK]
            sl[d, h] = sl[d, h] * refs[d][6][cd[d], :, lane:lane + 1] + refs[d][4][cd[d], h].astype(F32)

        @pl.when(step >= nctx)
        def _():
            for d, h in insts:
                rows = slice(cd[d] * CHUNK, (cd[d] + 1) * CHUNK)
                og = cqs[d, h][CHUNK:2 * CHUNK] + r[d, h][0:CHUNK]
                refs[d][7][rows, h * GDN_DV:(h + 1) * GDN_DV] = og.astype(BF16)
                refs[d][8][rows, h * GLA_DV:(h + 1) * GLA_DV] = ol[d, h].astype(BF16)

    for d, h in insts:
        sg_ref[d, h] = s[d, h]
        sl_ref[d, h] = sl[d, h]


def _scan(u0, cq, akd, ggl, m, qd, lgl, nctx, cps):
    b, nchunks = u0.shape[:2]
    assert nctx % cps == 0 and nchunks % cps == 0
    nlat = nchunks - nctx
    ntot, nctx = nchunks // cps, nctx // cps

    def specs(d):
        def chunk(bi, s):
            return (bi, _ext_chunk(d, s, nctx, ntot))

        def big(*tail, heads=GDN_HEADS):
            return pl.BlockSpec((None, cps, None, heads) + tail, lambda bi, s: chunk(bi, s) + (d, 0, 0, 0))

        def small(*tail):
            return pl.BlockSpec((None, cps) + tail, lambda bi, s: chunk(bi, s) + (0, 0))

        half = GDN_HEADS // 2
        return [big(CHUNK, GDN_DV), big(2 * CHUNK, GDN_DK), big(CHUNK + GDN_DK, 2 * CHUNK, heads=half),
                small(SUBLANE, LANE), big(GLA_DK, GLA_DV), big(CHUNK, 2 * GLA_DK, heads=half), small(GLA_DK, LANE)]

    def out_spec(d, width):
        return pl.BlockSpec((None, cps * CHUNK, width), lambda bi, s: (bi, _lat_chunk(d, s, nctx, ntot), 0))

    sds = jax.ShapeDtypeStruct
    return pl.pallas_call(
        functools.partial(_scan_kernel, nctx=nctx),
        grid=(b, ntot),
        in_specs=specs(0) + specs(1),
        out_specs=[out_spec(0, GDN_V), out_spec(1, GDN_V), out_spec(0, GLA_V), out_spec(1, GLA_V)],
        out_shape=[sds((b, nlat * CHUNK, GDN_V), BF16)] * 2 + [sds((b, nlat * CHUNK, GLA_V), BF16)] * 2,
        scratch_shapes=[pltpu.VMEM((2, GDN_HEADS, GDN_DK, GDN_DV), F32),
                        pltpu.VMEM((2, GLA_HEADS, GLA_DK, GLA_DV), F32)],
        compiler_params=_params("parallel", "arbitrary"),
        name="scan",
    )(u0, cq, akd, ggl, m, qd, lgl, u0, cq, akd, ggl, m, qd, lgl)


def _gla_pre_kernel(q_ref, k_ref, v_ref, la_ref, m_ref, qd_ref, gl_ref, intra_ref, *, cpb):
    tri = [_tri_masks(d) for d in range(2)]
    for c in range(cpb):
        rows = slice(c * CHUNK, (c + 1) * CHUNK)
        q = q_ref[rows, :] * (GLA_DK ** -0.5)
        k = k_ref[rows, :]
        v = v_ref[rows, :]
        la = la_ref[rows, :]
        pref = _cumsum_rows(tri[0][0], la)
        tot = jnp.sum(la, axis=0, keepdims=True)
        lanes = lax.broadcasted_iota(jnp.int32, la.shape, 1)
        bwd_lane = lanes >= GLA_QK
        bc = jnp.where(bwd_lane, tot - pref + la, pref)
        mid = CHUNK // 2
        ref = jnp.where(bwd_lane[0:1, :], bc[CHUNK - 1 - mid:CHUNK - mid, :], bc[mid:mid + 1, :])
        q2 = jnp.concatenate([q, q], axis=1)
        k2 = jnp.concatenate([k, k], axis=1)
        qa = q2 * jnp.exp(bc - ref)
        ka = k2 * jnp.exp(ref - bc)
        qd = (q2 * jnp.exp(bc)).astype(BF16)
        kdt = (k2 * jnp.exp(tot - bc)).T
        glcol = jnp.exp(jnp.sum(la.T, axis=1, keepdims=True))
        out_lanes = lax.broadcasted_iota(jnp.int32, (GLA_DK, LANE), 1)
        gl = jnp.zeros((GLA_DK, LANE), F32)
        for h in range(GLA_HEADS):
            vh = v[:, h * GLA_DV:(h + 1) * GLA_DV]
            intra = None
            for d in range(2):
                cs = slice(d * GLA_QK + h * GLA_DK, d * GLA_QK + (h + 1) * GLA_DK)
                attn = jnp.where(tri[d][0], _dot_nt(qa[:, cs], ka[:, cs]), 0.0)
                im = _dot(jnp.concatenate([attn, kdt[cs, :]], axis=0), vh)
                intra = im[0:CHUNK] if intra is None else intra + im[0:CHUNK]
                m_ref[c, d, h] = im[CHUNK:CHUNK + GLA_DK].astype(BF16)
                gl = jnp.where(out_lanes == d * GLA_HEADS + h, glcol[cs, :], gl)
            intra_ref[rows, h * GLA_DV:(h + 1) * GLA_DV] = intra.astype(BF16)
        for d in range(2):
            for p in range(GLA_HEADS // 2):
                qd_ref[c, d, p] = qd[:, d * GLA_QK + 2 * p * GLA_DK:d * GLA_QK + 2 * (p + 1) * GLA_DK]
        gl_ref[c] = gl


def _gla_pre(rest, loga, cpb):
    b, lext, _ = rest.shape
    ntot = lext // CHUNK
    rows = cpb * CHUNK
    lead = (b, ntot, 2, GLA_HEADS)

    def col(colblk):
        return lambda bi, i: (bi, i, colblk)

    def blk(*tail, heads=GLA_HEADS):
        return pl.BlockSpec((None, cpb, 2, heads) + tail, lambda bi, i: (bi, i, 0, 0, 0, 0))

    sds = jax.ShapeDtypeStruct
    return pl.pallas_call(
        functools.partial(_gla_pre_kernel, cpb=cpb),
        grid=(b, ntot // cpb),
        in_specs=[pl.BlockSpec((None, rows, GLA_QK), col((COL_LQ - COL_GZ) // GLA_QK)),
                  pl.BlockSpec((None, rows, GLA_QK), col((COL_LK - COL_GZ) // GLA_QK)),
                  pl.BlockSpec((None, rows, GLA_V), col((COL_LV - COL_GZ) // GLA_V)),
                  pl.BlockSpec((None, rows, 2 * GLA_QK), col(0))],
        out_specs=[blk(GLA_DK, GLA_DV), blk(CHUNK, 2 * GLA_DK, heads=GLA_HEADS // 2),
                   pl.BlockSpec((None, cpb, GLA_DK, LANE), lambda bi, i: (bi, i, 0, 0)),
                   pl.BlockSpec((None, rows, GLA_V), col(0))],
        out_shape=[sds(lead + (GLA_DK, GLA_DV), BF16), sds((b, ntot, 2, GLA_HEADS // 2, CHUNK, 2 * GLA_DK), BF16),
                   sds((b, ntot, GLA_DK, LANE), F32), sds((b, lext, GLA_V), BF16)],
        compiler_params=_params("parallel", "parallel"),
        name="gla_pre",
    )(rest, rest, rest, loga)


def _head_gate(o, gain, z, heads, dv):
    parts = []
    for h in range(heads):
        sl = slice(h * dv, (h + 1) * dv)
        oh = o[:, sl]
        parts.append(oh * lax.rsqrt(jnp.mean(oh * oh, axis=-1, keepdims=True) + EPS) * gain * _silu(z[:, sl]))
    return jnp.concatenate(parts, axis=-1)


def _ab_mix(refs, w_ref):
    gf_ref, gb_ref, lf_ref, lb_ref, li_ref, gz_ref, lg_ref, gn_ref, ln_ref = refs
    og = gf_ref[...].astype(F32) + gb_ref[...].astype(F32)
    ol = lf_ref[...].astype(F32) + lb_ref[...].astype(F32) + li_ref[0].astype(F32)
    yg = _head_gate(og, gn_ref[...], gz_ref[0], GDN_HEADS, GDN_DV)
    yl = _head_gate(ol, ln_ref[...], lg_ref[0], GLA_HEADS, GLA_DV)
    return (jnp.dot(yg.astype(BF16), w_ref[0:GDN_V, :], preferred_element_type=F32)
            + jnp.dot(yl.astype(BF16), w_ref[GDN_V:GDN_V + GLA_V, :], preferred_element_type=F32))


def _hy_mix(refs, w_ref):
    return jnp.dot(refs[0][...].astype(BF16), w_ref[...], preferred_element_type=F32)


def _tail_kernel(*refs, n_mix, mix_fn, nsplit, final_norm):
    mix_refs = refs[:n_mix]
    wo_ref, h_ref, g1_ref, gn_ref, sh_ref, sc_ref, g2_ref, w1_ref, w3_ref, w2_ref, fn_ref, o_ref = refs[n_mix:]
    h = h_ref[...] + g1_ref[...] * mix_fn(mix_refs, wo_ref)
    a = _modulated(h, gn_ref[...], sh_ref[...], sc_ref[...]).astype(BF16)
    fc = w1_ref.shape[1] // nsplit
    acc = jnp.zeros(h.shape, F32)
    for j in range(nsplit):
        u1 = jnp.dot(a, w1_ref[:, j * fc:(j + 1) * fc], preferred_element_type=F32)
        u3 = jnp.dot(a, w3_ref[:, j * fc:(j + 1) * fc], preferred_element_type=F32)
        acc = acc + jnp.dot((_silu(u1) * u3).astype(BF16), w2_ref[j * fc:(j + 1) * fc, :],
                            preferred_element_type=F32)
    y = h + g2_ref[...] * acc
    if final_norm:
        y = y * lax.rsqrt(jnp.mean(y * y, axis=-1, keepdims=True) + EPS) * fn_ref[...]
    o_ref[...] = y


def _tail(mix_args, mix_specs, mix_fn, w_out, h, g1, gain, shift, scale, g2, w1, w3, w2, final_w, tm, final_norm):
    b, l, d = h.shape
    dff = w1.shape[1]

    def tok(bi, i):
        return (bi, i, 0)

    def vec(bi, i):
        return (bi, 0, 0)

    def const(bi, i):
        return (0, 0)

    return pl.pallas_call(
        functools.partial(_tail_kernel, n_mix=len(mix_args), mix_fn=mix_fn, nsplit=2, final_norm=final_norm),
        grid=(b, l // tm),
        in_specs=list(mix_specs) + [
            _resident(w_out.shape, const), pl.BlockSpec((None, tm, d), tok), pl.BlockSpec((None, 1, d), vec),
            _resident((1, d), const), pl.BlockSpec((None, 1, d), vec), pl.BlockSpec((None, 1, d), vec),
            pl.BlockSpec((None, 1, d), vec), _resident((d, dff), const), _resident((d, dff), const),
            _resident((dff, d), const), _resident((1, d), const)],
        out_specs=pl.BlockSpec((None, tm, d), tok),
        out_shape=jax.ShapeDtypeStruct((b, l, d), F32),
        compiler_params=_params("parallel", "parallel"),
        name="mix_out_ffn",
    )(*mix_args, w_out, h, g1, gain.reshape(1, d), shift, scale, g2, w1, w3, w2, final_w.reshape(1, d))


def _hy_hidden_kernel(fb_ref, w1t_ref, w1c_ref, w1s_ref, b1_ref, w2_ref, b2_ref, w3_ref, b3_ref, fr_ref, o_ref,
                      *, tile, seq_len):
    rows = pl.program_id(0) * tile + lax.broadcasted_iota(jnp.int32, (tile, 1), 0)
    pos = jnp.where(rows < seq_len, rows, 2 * seq_len - rows).astype(F32)
    t = pos * (1.0 / (seq_len - 1))
    ang = (pos * (2.0 * math.pi / seq_len)) * fb_ref[...]
    fr = fr_ref[...]
    pre = t * w1t_ref[...] + _dot_hi(jnp.cos(ang), w1c_ref[...]) - _dot_hi(jnp.sin(ang), w1s_ref[...])
    hdn = jnp.sin(fr * (pre + b1_ref[...]))
    hdn = jnp.sin(fr * (_dot_hi(hdn, w2_ref[...]) + b2_ref[...]))
    o_ref[...] = jnp.sin(fr * (_dot_hi(hdn, w3_ref[...]) + b3_ref[...]))


def _hy_hidden(w1, b1, w2, b2, w3, b3, freq, seq_len, tile):
    hid = w1.shape[1]
    bands = jnp.linspace(1e-4, HY_BANDS - 1, HY_BANDS, dtype=F32)
    fb = jnp.zeros((1, LANE), F32).at[0, :HY_BANDS].set(bands)
    w1c = jnp.zeros((LANE, hid), F32).at[:HY_BANDS].set(w1[1:1 + HY_BANDS])
    w1s = jnp.zeros((LANE, hid), F32).at[:HY_BANDS].set(w1[1 + HY_BANDS:1 + 2 * HY_BANDS])
    args = (fb, w1[0:1], w1c, w1s, b1.reshape(1, hid), w2, b2.reshape(1, hid), w3, b3.reshape(1, hid),
            freq.reshape(1, hid))
    return pl.pallas_call(
        functools.partial(_hy_hidden_kernel, tile=tile, seq_len=seq_len),
        grid=(2 * seq_len // tile,),
        in_specs=[pl.BlockSpec(a.shape, lambda i: (0, 0)) for a in args],
        out_specs=pl.BlockSpec((tile, hid), lambda i: (i, 0)),
        out_shape=jax.ShapeDtypeStruct((2 * seq_len, hid), F32),
        compiler_params=_params("parallel"),
        name="hy_hidden",
    )(*args)


def _fft_matrices(r):
    n = r * r
    idx = jnp.arange(r, dtype=jnp.int32)

    def cis(m, period):
        ang = (2.0 * math.pi / period) * m.astype(F32)
        return jnp.cos(ang), -jnp.sin(ang)

    fr, fi = cis((idx[:, None] * idx[None, :]) % r, r)
    half = r // 2
    f1_data = jnp.block([[fr[:, :half], -fi[:, :half]], [fi[:, :half], fr[:, :half]]])
    f1_real = jnp.concatenate([fr, fi], axis=0)
    f3 = jnp.block([[fr[:half, :], fi[:half, :]], [-fi[:half, :], fr[:half, :]]]) * (1.0 / n)
    klo = idx[:, None, None]
    khi = idx[None, :, None]
    bb = idx[None, None, :]
    gr, gi = cis((bb * klo + r * bb * khi) % n, n)
    g = jnp.concatenate([jnp.concatenate([gr, -gi], axis=2), jnp.concatenate([gi, gr], axis=2)], axis=1)
    gh = jnp.swapaxes(g, 1, 2)
    return f1_data.astype(BF16), f1_real.astype(BF16), f3.astype(BF16), g.astype(BF16), gh.astype(BF16)


FFT_BG = 16
FFT_DT = 256


def _slab(ref, j):
    return jnp.concatenate([ref[i, :, j, :] for i in range(ref.shape[0])], axis=0)


def _spread_rows(scr, j, bg, val):
    for lg in range(val.shape[1] // LANE):
        scr[lg, pl.ds(j, val.shape[0], stride=bg), :] = val[:, lg * LANE:(lg + 1) * LANE]


def _flush(scr, o_ref):
    for lg in range(scr.shape[0]):
        o_ref[..., lg * LANE:(lg + 1) * LANE] = scr[lg].reshape(o_ref.shape[:-1] + (LANE,)).astype(o_ref.dtype)


def _time_block(arr, sel, n):
    lead = arr.ndim - 4
    return pl.BlockSpec((None,) * lead + (n, arr.shape[-3], FFT_BG, FFT_DT),
                        lambda p, g, dd: tuple(sel) + (p, 0, g, dd))


def _fft_stage1_kernel(f_ref, x_ref, o_ref, ps):
    bg = x_ref.shape[2]
    for j in range(bg):
        p = jnp.dot(f_ref[...], _slab(x_ref, j).astype(BF16), preferred_element_type=F32)
        _spread_rows(ps, j, bg, p)
    _flush(ps, o_ref)


def _fft_stage1(f1, x5, sel, n):
    r2, _ = f1.shape
    r = r2 // 2
    items, _, rb, d = x5.shape[-4:]
    groups = items // n
    return pl.pallas_call(
        _fft_stage1_kernel,
        grid=(groups, rb // FFT_BG, d // FFT_DT),
        in_specs=[_resident(f1.shape, lambda p, g, dd: (0, 0)), _time_block(x5, sel, n)],
        out_specs=pl.BlockSpec((None, 2, r, FFT_BG, FFT_DT), lambda p, g, dd: (p, 0, 0, g, dd)),
        out_shape=jax.ShapeDtypeStruct((groups, 2, r, rb, d), BF16),
        scratch_shapes=[pltpu.VMEM((FFT_DT // LANE, r2 * FFT_BG, LANE), F32)],
        compiler_params=_params("parallel", "parallel", "parallel"),
        name="fft_stage1",
    )(f1, x5)


def _fft_filter_stage1_kernel(f_ref, hdn_ref, fo_ref, dl_ref, o_ref, ps, *, seq_len):
    r, bg, _ = hdn_ref.shape
    orders = fo_ref.shape[1]
    half = r // 2
    a_idx = lax.broadcasted_iota(jnp.int32, (r, 1), 0)
    for j in range(bg):
        rows = a_idx * r + (pl.program_id(0) * bg + j)
        pos = jnp.where(rows < seq_len, rows, 2 * seq_len - rows).astype(F32)
        t = pos * (1.0 / (seq_len - 1))
        window = jnp.where(rows != seq_len, jnp.exp(-t * dl_ref[...]) + HY_SHIFT, 0.0)
        hj = hdn_ref[:, j, :]
        for n in range(orders):
            taps = jnp.concatenate([_dot(hj[:half], fo_ref[0, n]), _dot(hj[half:], fo_ref[1, n])], axis=0) * window
            p = jnp.dot(f_ref[...], taps.astype(BF16), preferred_element_type=F32)
            _spread_rows(ps.at[n], j, bg, p)
    for n in range(orders):
        _flush(ps.at[n], o_ref.at[n])


def _fft_filter_stage1(f1, hdn3, fo, deltas, seq_len):
    r2, r = f1.shape
    hid = hdn3.shape[-1]
    _, orders, _, d = fo.shape
    lanes = FFT_DT // LANE
    return pl.pallas_call(
        functools.partial(_fft_filter_stage1_kernel, seq_len=seq_len),
        grid=(r // FFT_BG, d // FFT_DT),
        in_specs=[_resident(f1.shape, lambda g, dd: (0, 0)),
                  pl.BlockSpec((r, FFT_BG, hid), lambda g, dd: (0, g, 0)),
                  pl.BlockSpec((2, orders, hid, FFT_DT), lambda g, dd: (0, 0, 0, dd)),
                  pl.BlockSpec((1, FFT_DT), lambda g, dd: (0, dd))],
        out_specs=pl.BlockSpec((orders, 2, r, FFT_BG, FFT_DT), lambda g, dd: (0, 0, 0, g, dd)),
        out_shape=jax.ShapeDtypeStruct((orders, 2, r, r, d), BF16),
        scratch_shapes=[pltpu.VMEM((orders, lanes, r2 * FFT_BG, LANE), F32)],
        compiler_params=_params("parallel", "parallel"),
        name="fft_filter_stage1",
    )(f1, hdn3, fo, deltas)


def _fft_stage2_kernel(g_ref, gh_ref, k_ref, p_ref, o_ref):
    r = p_ref.shape[2]
    kq = k_ref[...]
    h = jnp.dot(g_ref[...], kq.reshape(2 * r, kq.shape[2]), preferred_element_type=F32)
    hr, hi = h[:r], h[r:]
    for n in range(p_ref.shape[0]):
        p = p_ref[n]
        x = jnp.dot(g_ref[...], p.reshape(2 * r, p.shape[2]), preferred_element_type=F32)
        xr, xi = x[:r], x[r:]
        y = jnp.concatenate([xr * hr - xi * hi, xr * hi + xi * hr], axis=0)
        q = jnp.dot(gh_ref[...], y.astype(BF16), preferred_element_type=F32)
        o_ref[n] = q.reshape(o_ref.shape[1:]).astype(o_ref.dtype)


def _fft_stage2(g, gh, pk, order, p5):
    pairs, _, r, _, d = p5.shape
    return pl.pallas_call(
        _fft_stage2_kernel,
        grid=(r,),
        in_specs=[pl.BlockSpec((None, 2 * r, 2 * r), lambda kk: (kk, 0, 0)),
                  pl.BlockSpec((None, 2 * r, 2 * r), lambda kk: (kk, 0, 0)),
                  pl.BlockSpec((None, 2, None, r, d), lambda kk: (order, 0, kk, 0, 0)),
                  pl.BlockSpec((pairs, 2, None, r, d), lambda kk: (0, 0, kk, 0, 0))],
        out_specs=pl.BlockSpec((pairs, 2, None, r, d), lambda kk: (0, 0, kk, 0, 0)),
        out_shape=jax.ShapeDtypeStruct(p5.shape, BF16),
        compiler_params=_params("parallel"),
        name="fft_stage2",
    )(g, gh, pk, p5)


def _fft_stage3_kernel(f3_ref, f1_ref, q_ref, z_ref, gate_ref, skip_ref, *rest, chain):
    if chain:
        zo_ref, po_ref, qs, zs, ps = rest
    else:
        zo_ref, qs, zs = rest
    bg = z_ref.shape[2]
    rows = q_ref.shape[0] * q_ref.shape[1]
    for lg in range(qs.shape[0]):
        qs[lg] = q_ref[..., lg * LANE:(lg + 1) * LANE].astype(F32).reshape(rows * bg, LANE)
    skip = skip_ref[...]
    for j in range(bg):
        qj = jnp.concatenate([qs[lg, pl.ds(j, rows, stride=bg), :] for lg in range(qs.shape[0])], axis=1)
        y = jnp.dot(f3_ref[...], qj.astype(BF16), preferred_element_type=F32)
        zn = _slab(gate_ref, j) * (y + _slab(z_ref, j) * skip)
        _spread_rows(zs, j, bg, zn)
        if chain:
            _spread_rows(ps, j, bg, jnp.dot(f1_ref[...], zn.astype(BF16), preferred_element_type=F32))
    _flush(zs, zo_ref)
    if chain:
        _flush(ps, po_ref)


def _fft_stage3(f3, f1, q5, z5, z_sel, g5, g_sel, skip, chain):
    pairs, _, r, rb, d = q5.shape
    a = z5.shape[-3]
    sds = jax.ShapeDtypeStruct
    out_specs = [pl.BlockSpec((2, a, FFT_BG, FFT_DT), lambda p, g, dd: (p, 0, g, dd))]
    out_shape = [sds((2 * pairs, a, rb, d), F32)]
    lanes = FFT_DT // LANE
    scratch = [pltpu.VMEM((lanes, 2 * r * FFT_BG, LANE), F32), pltpu.VMEM((lanes, 2 * a * FFT_BG, LANE), F32)]
    if chain:
        out_specs.append(pl.BlockSpec((None, 2, r, FFT_BG, FFT_DT), lambda p, g, dd: (p, 0, 0, g, dd)))
        out_shape.append(sds(q5.shape, BF16))
        scratch.append(pltpu.VMEM((lanes, 2 * r * FFT_BG, LANE), F32))
    return pl.pallas_call(
        functools.partial(_fft_stage3_kernel, chain=chain),
        grid=(pairs, rb // FFT_BG, d // FFT_DT),
        in_specs=[_resident(f3.shape, lambda p, g, dd: (0, 0)), _resident(f1.shape, lambda p, g, dd: (0, 0)),
                  pl.BlockSpec((None, 2, r, FFT_BG, FFT_DT), lambda p, g, dd: (p, 0, 0, g, dd)),
                  _time_block(z5, z_sel, 2), _time_block(g5, g_sel, 2),
                  pl.BlockSpec((1, FFT_DT), lambda p, g, dd: (0, dd))],
        out_specs=out_specs,
        out_shape=out_shape,
        scratch_shapes=scratch,
        compiler_params=_params("parallel", "parallel", "parallel"),
        name="fft_stage3",
    )(f3, f1, q5, z5, g5, skip)


TAIL_ROWS = 512
HY_FRONT_ROWS = 512
AB_FRONT_ROWS = 384


def _regroup_ab_weights(w_in):
    d = w_in.shape[0]
    sizes = (GDN_QK, GDN_QK, GDN_V, GDN_V, 2 * GDN_HEADS, 2 * GDN_HEADS, GLA_QK, GLA_QK, GLA_V, GLA_V,
             2 * GLA_GATE_RANK)
    offs = [0]
    for s in sizes:
        offs.append(offs[-1] + s)
    gq, gk, gv, gz, ga, gb, lq, lk, lv, lg, llr = (w_in[:, offs[i]:offs[i + 1]] for i in range(len(sizes)))
    pad = jnp.zeros((d, LANE - (ga.shape[1] + gb.shape[1] + llr.shape[1])), w_in.dtype)
    return jnp.concatenate([gq, gk, gv, gz, lq, lk, lv, lg, ga, gb, llr, pad], axis=1)


def _lane_row(vals, offset):
    flat = vals.reshape(-1).astype(F32)
    return jnp.zeros((1, LANE), F32).at[0, offset:offset + flat.shape[0]].set(flat)


def _even_layer(h, ctx, mods, norm1, norm2, w_in, conv_w, a_log, dt_bias, gdn_norm, gate_w, gate_b, gla_norm,
                w_out, w1, w3, w2, final_w, final_norm):
    b, l, d = h.shape
    lctx = ctx.shape[1]
    tm = TAIL_ROWS
    sh1, sc1, g1, sh2, sc2, g2 = mods["lat"]
    csh1, csc1 = mods["ctx"][0], mods["ctx"][1]
    xe = jnp.concatenate([ctx, h], axis=1)
    shift = jnp.stack([jnp.broadcast_to(csh1, (b, d)), sh1], axis=1)[:, :, None, :]
    scale = jnp.stack([jnp.broadcast_to(csc1, (b, d)), sc1], axis=1)[:, :, None, :]
    gw = jnp.zeros((LANE, 2 * GLA_QK), F32)
    for dd in range(2):
        gw = gw.at[SM_LR + dd * GLA_GATE_RANK:SM_LR + (dd + 1) * GLA_GATE_RANK,
                   dd * GLA_QK:(dd + 1) * GLA_QK].set(gate_w[dd])
    q, k, v, sm, loga, rest = _ab_front(xe, norm1, shift, scale, _regroup_ab_weights(w_in).astype(BF16), conv_w,
                                        _lane_row(a_log, SM_G), _lane_row(dt_bias, SM_G), gw,
                                        gate_b.reshape(1, 2 * GLA_QK), AB_FRONT_ROWS, lctx)
    nctx = lctx // CHUNK
    u0, cq, akd, ggl = _gdn_pre(q, k, v, sm, 2)
    m, qd, lgl, ol_intra = _gla_pre(rest, loga, 2)
    og_f, og_b, ol_f, ol_b = _scan(u0, cq, akd, ggl, m, qd, lgl, nctx, 4)
    def lat(width):
        return pl.BlockSpec((None, tm, width), lambda bi, i: (bi, i, 0))

    def ext(width, colblk):
        return pl.BlockSpec((pl.Element(1), pl.Element(tm), pl.Element(width)),
                            lambda bi, i: (bi, SUBLANE * (lctx // SUBLANE + i * (tm // SUBLANE)), colblk * width))

    mix_args = (og_f, og_b, ol_f, ol_b, ol_intra, rest, rest, gdn_norm.reshape(1, GDN_DV), gla_norm.reshape(1, GLA_DV))
    mix_specs = [lat(GDN_V), lat(GDN_V), lat(GLA_V), lat(GLA_V), ext(GLA_V, 0), ext(GDN_V, 0),
                 ext(GLA_V, (COL_LG - COL_GZ) // GLA_V), _resident((1, GDN_DV), lambda bi, i: (0, 0)),
                 _resident((1, GLA_DV), lambda bi, i: (0, 0))]
    return _tail(mix_args, mix_specs, _ab_mix, w_out.astype(BF16), h, g1[:, None, :], norm2, sh2[:, None, :],
                 sc2[:, None, :], g2[:, None, :], w1.astype(BF16), w3.astype(BF16), w2.astype(BF16), final_w, tm,
                 final_norm)


def _hyena_layer(h, mods, norm1, norm2, w_in, conv_w, pw1, pb1, pw2, pb2, pw3, pb3, freq, filt_out, skip, w_out,
                 w1, w3, w2, final_w, final_norm):
    b, l, d = h.shape
    tm = TAIL_ROWS
    sh1, sc1, g1, sh2, sc2, g2 = mods["lat"]
    u3 = _hy_front(h, norm1, sh1[:, None, :], sc1[:, None, :], w_in.astype(BF16), conv_w, HY_FRONT_ROWS)

    n2 = 2 * l
    r = math.isqrt(n2)
    assert r * r == n2 and b % 2 == 0
    orders = skip.shape[0]
    hdn = _hy_hidden(pw1, pb1, pw2, pb2, pw3, pb3, freq, l, 512)
    fo = jnp.transpose(filt_out.reshape(filt_out.shape[0], 2, orders, d), (1, 2, 0, 3))
    deltas = jnp.abs(jnp.linspace(HY_MIN_DECAY, HY_MAX_DECAY, d, dtype=F32))[None, :]

    f1_data, f1_real, f3, g, gh = _fft_matrices(r)
    pk = _fft_filter_stage1(f1_real, hdn.reshape(r, r, hdn.shape[1]), fo, deltas, l)

    u5 = u3.reshape(3, b, r // 2, r, d)
    z_arr, z_sel = u5, (0,)
    p = _fft_stage1(f1_data, z_arr, z_sel, 2)
    for n in range(orders):
        q = _fft_stage2(g, gh, pk, n, p)
        chain = n + 1 < orders
        res = _fft_stage3(f3, f1_data, q, z_arr, z_sel, u5, (n + 1,), skip[n][None, :], chain)
        z_arr, z_sel = res[0], ()
        if chain:
            p = res[1]
    mix_specs = [pl.BlockSpec((None, tm, d), lambda bi, i: (bi, i, 0))]
    return _tail((z_arr.reshape(b, l, d),), mix_specs, _hy_mix, w_out.astype(BF16), h, g1[:, None, :], norm2,
                 sh2[:, None, :], sc2[:, None, :], g2[:, None, :], w1.astype(BF16), w3.astype(BF16), w2.astype(BF16),
                 final_w, tm, final_norm)


def kernel(x, c, ctx, c_ctx, mod_w, mod_b, norm1_w, norm2_w, ab_w_in, ab_conv_w, gdn_a_log, gdn_dt_bias, gdn_norm_w, gla_gate_w, gla_gate_b, gla_norm_w, ab_w_out, hy_w_in, hy_conv_w, hy_pos_w1, hy_pos_b1, hy_pos_w2, hy_pos_b2, hy_pos_w3, hy_pos_b3, hy_freq, hy_filt_out, hy_skip, hy_w_out, ffn_w1, ffn_w3, ffn_w2, final_norm_w):
    b, l, d = x.shape
    depth = mod_w.shape[0]
    assert b + 1 <= SUBLANE
    cvec = jnp.concatenate([c, c_ctx[None, :], jnp.zeros((SUBLANE - b - 1, d), F32)], axis=0)
    mod = _modulation(cvec, mod_w, mod_b)
    h = x
    for i in range(depth):
        mods = {"lat": [mod[i, :b, j * d:(j + 1) * d] for j in range(6)],
                "ctx": [mod[i, b, j * d:(j + 1) * d] for j in range(6)]}
        last = i == depth - 1
        if i % 2 == 0:
            assert not any(j % 2 == 0 for j in range(i + 1, depth))
            e = i // 2
            h = _even_layer(h, ctx, mods, norm1_w[i], norm2_w[i], ab_w_in[e], ab_conv_w[e], gdn_a_log[e],
                            gdn_dt_bias[e], gdn_norm_w[e], gla_gate_w[e], gla_gate_b[e], gla_norm_w[e], ab_w_out[e],
                            ffn_w1[i], ffn_w3[i], ffn_w2[i], final_norm_w, last)
        else:
            o = i // 2
            h = _hyena_layer(h, mods, norm1_w[i], norm2_w[i], hy_w_in[o], hy_conv_w[o], hy_pos_w1[o], hy_pos_b1[o],
                             hy_pos_w2[o], hy_pos_b2[o], hy_pos_w3[o], hy_pos_b3[o], hy_freq[o], hy_filt_out[o],
                             hy_skip[o], hy_w_out[o], ffn_w1[i], ffn_w3[i], ffn_w2[i], final_norm_w, last)
    return h
```

```python
import functools
import math

import jax
import jax.numpy as jnp
from jax import lax
from jax.experimental import pallas as pl
from jax.experimental.pallas import tpu as pltpu

F32 = jnp.float32
BF16 = jnp.bfloat16
HI = lax.Precision.HIGHEST

EPS = 1e-6
CHUNK = 64
GDN_HEADS, GDN_DK, GDN_DV = 4, 128, 128
GLA_HEADS, GLA_DK, GLA_DV = 4, 64, 128
GLA_GATE_RANK = 16
GLA_GATE_TAU = 16.0
GDN_QK = GDN_HEADS * GDN_DK
GDN_V = GDN_HEADS * GDN_DV
GLA_QK = GLA_HEADS * GLA_DK
GLA_V = GLA_HEADS * GLA_DV
HY_EMB = 33
HY_BANDS = (HY_EMB - 1) // 2
HY_MIN_DECAY = math.log(1e-2) / 1.5
HY_MAX_DECAY = math.log(1e-2) / 0.3
HY_SHIFT = 0.05

LANE = 128
SUBLANE = 8
VMEM_LIMIT_BYTES = 56 * 1024 * 1024

COL_GQ, COL_GK, COL_GV, COL_GZ = 0, 512, 1024, 1536
COL_LQ, COL_LK, COL_LV, COL_LG = 2048, 2304, 2560, 3072
COL_SMALL = 3584
SM_G, SM_B, SM_LR = 0, 8, 16


def _params(*sem):
    return pltpu.CompilerParams(dimension_semantics=sem, vmem_limit_bytes=VMEM_LIMIT_BYTES)


def _resident(shape, index_map):
    return pl.BlockSpec(shape, index_map, pipeline_mode=pl.Buffered(1))


def _dot(a, b):
    return jnp.dot(a.astype(BF16), b.astype(BF16), preferred_element_type=F32)


def _dot_nt(a, b):
    return lax.dot_general(a.astype(BF16), b.astype(BF16), (((1,), (1,)), ((), ())),
                           preferred_element_type=F32)


def _dot_hi(a, b):
    return jnp.dot(a, b, preferred_element_type=F32, precision=HI)


def _sigmoid(x):
    return 1.0 / (1.0 + jnp.exp(-x))


def _silu(x):
    return x * _sigmoid(x)


def _softplus(x):
    return jnp.maximum(x, 0.0) + jnp.log(1.0 + jnp.exp(-jnp.abs(x)))


def _mod_kernel(c_ref, w_ref, b_ref, o_ref):
    o_ref[...] = _dot_hi(_silu(c_ref[...]), w_ref[...]) + b_ref[...]


def _modulation(cvec, mod_w, mod_b):
    depth, d, n = mod_w.shape
    tn = n // 4
    return pl.pallas_call(
        _mod_kernel,
        grid=(depth, n // tn),
        in_specs=[pl.BlockSpec((SUBLANE, d), lambda i, j: (0, 0)),
                  pl.BlockSpec((None, d, tn), lambda i, j: (i, 0, j)),
                  pl.BlockSpec((None, 1, tn), lambda i, j: (i, 0, j))],
        out_specs=pl.BlockSpec((None, SUBLANE, tn), lambda i, j: (i, 0, j)),
        out_shape=jax.ShapeDtypeStruct((depth, SUBLANE, n), F32),
        compiler_params=_params("parallel", "parallel"),
        name="modulation",
    )(cvec, mod_w, mod_b.reshape(depth, 1, n))


CONV_HALO = SUBLANE


def _modulated(x, gain, shift, scale):
    ms = jnp.mean(x * x, axis=-1, keepdims=True)
    return x * lax.rsqrt(ms + EPS) * gain * (1.0 + scale) + shift


def _window_start(i, tm, win, seq_len):
    return SUBLANE * jnp.clip(i * (tm // SUBLANE) - CONV_HALO // SUBLANE, 0, (seq_len - win) // SUBLANE)


def _window_spec(tm, win, seq_len, d):
    return pl.BlockSpec((pl.Element(1), pl.Element(win), pl.Element(d)),
                        lambda bi, i: (bi, _window_start(i, tm, win, seq_len), 0))


def _conv3_rows(u, w, gidx=None, starts=(), ends=()):
    rows = u.shape[0]
    prev = pltpu.roll(u, 1, 0)
    nxt = pltpu.roll(u, rows - 1, 0)
    if starts:
        prev = jnp.where(functools.reduce(jnp.logical_or, [gidx == r for r in starts]), 0.0, prev)
    if ends:
        nxt = jnp.where(functools.reduce(jnp.logical_or, [gidx == r for r in ends]), 0.0, nxt)
    return prev * w[0:1, :] + u * w[1:2, :] + nxt * w[2:3, :]


def _hy_front_kernel(x_ref, g_ref, sh_ref, sc_ref, w_ref, cw_ref, o_ref, scr, *, tm, seq_len):
    i = pl.program_id(1)
    win = x_ref.shape[1]
    ws = _window_start(i, tm, win, seq_len)
    off = pl.multiple_of(i * tm - ws, SUBLANE)
    a = _modulated(x_ref[0], g_ref[...], sh_ref[...], sc_ref[...])
    u = jnp.dot(a.astype(BF16), w_ref[...], preferred_element_type=F32)
    cw = cw_ref[...]
    scr[...] = _conv3_rows(u, cw)

    @pl.when(i == 0)
    def _():
        scr[0:1, :] = scr[0:1, :] - u[win - 1:win, :] * cw[0:1, :]

    @pl.when(i == pl.num_programs(1) - 1)
    def _():
        scr[win - 1:win, :] = scr[win - 1:win, :] - u[0:1, :] * cw[2:3, :]

    d = o_ref.shape[-1]
    for g in range(o_ref.shape[0]):
        o_ref[g] = scr[pl.ds(off, tm), g * d:(g + 1) * d]


def _hy_front(x, gain, shift, scale, w, conv_w, tm):
    b, l, d = x.shape
    n = w.shape[1]
    groups = n // d
    win = tm + 2 * CONV_HALO

    def vec(bi, i):
        return (bi, 0, 0)

    return pl.pallas_call(
        functools.partial(_hy_front_kernel, tm=tm, seq_len=l),
        grid=(b, l // tm),
        in_specs=[_window_spec(tm, win, l, d), _resident((1, d), lambda bi, i: (0, 0)),
                  pl.BlockSpec((None, 1, d), vec), pl.BlockSpec((None, 1, d), vec),
                  _resident((d, n), lambda bi, i: (0, 0)), _resident((3, n), lambda bi, i: (0, 0))],
        out_specs=pl.BlockSpec((groups, None, tm, d), lambda bi, i: (0, bi, i, 0)),
        out_shape=jax.ShapeDtypeStruct((groups, b, l, d), F32),
        scratch_shapes=[pltpu.VMEM((win, n), F32)],
        compiler_params=_params("parallel", "parallel"),
        name="hy_front",
    )(x, gain.reshape(1, d), shift, scale, w, conv_w)


def _ab_front_kernel(x_ref, g_ref, sh_ref, sc_ref, w_ref, cw_ref, alog_ref, dtb_ref, gw_ref, gb_ref,
                     q_ref, k_ref, v_ref, so_ref, la_ref, rest_ref, scr, *, tm, lctx, lext):
    i = pl.program_id(1)
    win = x_ref.shape[1]
    ws = _window_start(i, tm, win, lext)
    off = pl.multiple_of(i * tm - ws, SUBLANE)
    gidx = ws + lax.broadcasted_iota(jnp.int32, (win, 1), 0)
    is_ctx = gidx < lctx
    shift = jnp.where(is_ctx, sh_ref[0], sh_ref[1])
    scale = jnp.where(is_ctx, sc_ref[0], sc_ref[1])
    a = _modulated(x_ref[0], g_ref[...], shift, scale)
    u = jnp.dot(a.astype(BF16), w_ref[...], preferred_element_type=F32)
    ncv = cw_ref.shape[1]
    scr[:, 0:ncv] = _conv3_rows(u[:, 0:ncv], cw_ref[...], gidx, (0, lctx), (lctx - 1, lext - 1))
    scr[:, ncv:] = u[:, ncv:]
    rows = pl.ds(off, tm)
    for h in range(GDN_HEADS):
        sl = slice(h * GDN_DK, (h + 1) * GDN_DK)
        qh = _silu(scr[rows, COL_GQ + h * GDN_DK:COL_GQ + (h + 1) * GDN_DK])
        kh = _silu(scr[rows, COL_GK + h * GDN_DK:COL_GK + (h + 1) * GDN_DK])
        q_ref[:, sl] = qh * lax.rsqrt(jnp.sum(qh * qh, axis=-1, keepdims=True) + EPS) * (GDN_DK ** -0.5)
        k_ref[:, sl] = kh * lax.rsqrt(jnp.sum(kh * kh, axis=-1, keepdims=True) + EPS)
    v_ref[...] = _silu(scr[rows, COL_GV:COL_GV + GDN_V])
    rest_ref[...] = scr[rows, COL_GZ:COL_SMALL]
    sm = scr[rows, COL_SMALL:COL_SMALL + LANE]
    g = -jnp.exp(alog_ref[...]) * _softplus(sm + dtb_ref[...])
    beta = _sigmoid(sm)
    lanes = lax.broadcasted_iota(jnp.int32, sm.shape, 1)
    so_ref[...] = jnp.where(lanes < SM_B, g, beta)
    gate = _dot(sm, gw_ref[...]) + gb_ref[...]
    la_ref[...] = (jnp.minimum(gate, 0.0) - jnp.log(1.0 + jnp.exp(-jnp.abs(gate)))) * (1.0 / GLA_GATE_TAU)


def _ab_front(xe, gain, shift, scale, w, conv_w, alog_row, dtb_row, gate_w, gate_b, tm, lctx):
    b, lext, d = xe.shape
    n = w.shape[1]
    win = tm + 2 * CONV_HALO
    nrest = COL_SMALL - COL_GZ

    def const(bi, i):
        return (0, 0)

    def tok(width):
        return pl.BlockSpec((None, tm, width), lambda bi, i: (bi, i, 0))

    sds = jax.ShapeDtypeStruct
    return pl.pallas_call(
        functools.partial(_ab_front_kernel, tm=tm, lctx=lctx, lext=lext),
        grid=(b, lext // tm),
        in_specs=[_window_spec(tm, win, lext, d), _resident((1, d), const),
                  pl.BlockSpec((None, 2, 1, d), lambda bi, i: (bi, 0, 0, 0)),
                  pl.BlockSpec((None, 2, 1, d), lambda bi, i: (bi, 0, 0, 0)),
                  _resident((d, n), const), _resident(conv_w.shape, const),
                  _resident((1, LANE), const), _resident((1, LANE), const),
                  _resident((LANE, 2 * GLA_QK), const), _resident((1, 2 * GLA_QK), const)],
        out_specs=[tok(GDN_QK), tok(GDN_QK), tok(GDN_V), tok(LANE), tok(2 * GLA_QK), tok(nrest)],
        out_shape=[sds((b, lext, GDN_QK), F32), sds((b, lext, GDN_QK), F32), sds((b, lext, GDN_V), F32),
                   sds((b, lext, LANE), F32), sds((b, lext, 2 * GLA_QK), F32), sds((b, lext, nrest), F32)],
        scratch_shapes=[pltpu.VMEM((win, n), F32)],
        compiler_params=_params("parallel", "parallel"),
        name="ab_front",
    )(xe, gain.reshape(1, d), shift, scale, w, conv_w, alog_row, dtb_row, gate_w, gate_b)


def _ext_chunk(d, s, nctx, ntot):
    if d == 0:
        return s
    return jnp.where(s < nctx, nctx - 1 - s, ntot + nctx - 1 - s)


def _lat_chunk(d, s, nctx, ntot):
    nlat = ntot - nctx
    if d == 0:
        return jnp.maximum(s - nctx, 0)
    return jnp.minimum(ntot - 1 - s, nlat - 1)


def _tri_masks(d):
    ii = lax.broadcasted_iota(jnp.int32, (CHUNK, CHUNK), 0)
    jj = lax.broadcasted_iota(jnp.int32, (CHUNK, CHUNK), 1)
    if d == 0:
        return ii >= jj, ii > jj, ii == jj
    return ii <= jj, ii < jj, ii == jj


def _cumsum_rows(incl, x):
    m = incl.astype(BF16)
    x1 = x.astype(BF16)
    r1 = x - x1.astype(F32)
    x2 = r1.astype(BF16)
    x3 = (r1 - x2.astype(F32)).astype(BF16)
    dot = functools.partial(jnp.dot, preferred_element_type=F32)
    return dot(m, x1) + dot(m, x2) + dot(m, x3)


def _merge_masks(d):
    ii = lax.broadcasted_iota(jnp.int32, (CHUNK, CHUNK), 0)
    jj = lax.broadcasted_iota(jnp.int32, (CHUNK, CHUNK), 1)
    lo, hi = (jj, ii) if d == 0 else (ii, jj)
    masks = {}
    s = 1
    while s < CHUNK:
        masks[s] = jnp.logical_and(ii // (2 * s) == jj // (2 * s),
                                   jnp.logical_and((hi // s) % 2 == 1, (lo // s) % 2 == 0))
        s *= 2
    return masks


def _gdn_pre_kernel(q_ref, k_ref, v_ref, sm_ref, u0_ref, cq_ref, akd_ref, gl_ref, *, cpb):
    tri = [_tri_masks(d) for d in range(2)]
    merge = [_merge_masks(d) for d in range(2)]
    insts = [(d, h) for d in range(2) for h in range(GDN_HEADS)]
    for c in range(cpb):
        rows = slice(c * CHUNK, (c + 1) * CHUNK)
        q = q_ref[rows, :]
        k = k_ref[rows, :]
        v = v_ref[rows, :]
        sm = sm_ref[rows, :]
        pref = _cumsum_rows(tri[0][0], sm)
        tot = jnp.sum(sm, axis=0, keepdims=True)
        lanes = lax.broadcasted_iota(jnp.int32, sm.shape, 1)
        bwd_lane = jnp.logical_and(lanes >= SM_G + GDN_HEADS, lanes < SM_G + 2 * GDN_HEADS)
        gc = jnp.where(bwd_lane, tot - pref + sm, pref)
        gct = gc.T
        totc = jnp.sum(sm.T, axis=1, keepdims=True)
        gl_ref[c] = jnp.exp(jnp.broadcast_to(totc[0:SUBLANE, :], (SUBLANE, LANE)))
        egc = jnp.exp(gc)
        etg = jnp.exp(tot - gc)
        kh = [k[:, h * GDN_DK:(h + 1) * GDN_DK] for h in range(GDN_HEADS)]
        qh = [q[:, h * GDN_DK:(h + 1) * GDN_DK] for h in range(GDN_HEADS)]
        vh = [v[:, h * GDN_DV:(h + 1) * GDN_DV] for h in range(GDN_HEADS)]
        gq = [_dot_nt(jnp.concatenate([kh[h], qh[h]], axis=0), kh[h]) for h in range(GDN_HEADS)]
        a, t, beta, ecol = {}, {}, {}, {}
        for d, h in insts:
            incl, strict, diag = tri[d]
            col = SM_G + d * GDN_HEADS + h
            decay = jnp.exp(jnp.where(incl, gc[:, col:col + 1] - gct[col:col + 1, :], -jnp.inf))
            beta[d, h] = sm[:, SM_B + col:SM_B + col + 1]
            ecol[d, h] = egc[:, col:col + 1]
            a[d, h] = jnp.where(strict, gq[h][0:CHUNK] * decay, 0.0) * beta[d, h]
            hl = slice((h % 2) * CHUNK, (h % 2 + 1) * CHUNK)
            akd_ref[c, d, h // 2, 0:CHUNK, hl] = (gq[h][CHUNK:2 * CHUNK] * decay).astype(BF16)
            akd_ref[c, d, h // 2, CHUNK:CHUNK + GDN_DK, hl] = (kh[h] * etg[:, col:col + 1]).T.astype(BF16)
            t[d, h] = diag.astype(F32) - jnp.where(merge[d][1], a[d, h], 0.0)
        s = 2
        while s < CHUNK:
            x = {i: _dot(jnp.where(merge[i[0]][s], a[i], 0.0), t[i]) for i in insts}
            t = {i: t[i] - _dot(t[i], x[i]) for i in insts}
            s *= 2
        for d, h in insts:
            rhs = jnp.concatenate([vh[h] * beta[d, h], kh[h] * (beta[d, h] * ecol[d, h])], axis=1)
            sol = _dot(t[d, h], rhs)
            u0_ref[c, d, h] = sol[:, 0:GDN_DV].astype(BF16)
            cq_ref[c, d, h, 0:CHUNK, :] = sol[:, GDN_DV:GDN_DV + GDN_DK].astype(BF16)
            cq_ref[c, d, h, CHUNK:2 * CHUNK, :] = (qh[h] * ecol[d, h]).astype(BF16)


def _gdn_pre(q, k, v, sm, cpb):
    b, lext, _ = q.shape
    ntot = lext // CHUNK
    rows = cpb * CHUNK
    lead = (b, ntot, 2, GDN_HEADS)

    def tok(bi, i):
        return (bi, i, 0)

    def blk(*tail, heads=GDN_HEADS):
        return pl.BlockSpec((None, cpb, 2, heads) + tail, lambda bi, i: (bi, i, 0, 0, 0, 0))

    sds = jax.ShapeDtypeStruct
    pairs = GDN_HEADS // 2
    return pl.pallas_call(
        functools.partial(_gdn_pre_kernel, cpb=cpb),
        grid=(b, ntot // cpb),
        in_specs=[pl.BlockSpec((None, rows, GDN_QK), tok), pl.BlockSpec((None, rows, GDN_QK), tok),
                  pl.BlockSpec((None, rows, GDN_V), tok), pl.BlockSpec((None, rows, LANE), tok)],
        out_specs=[blk(CHUNK, GDN_DV), blk(2 * CHUNK, GDN_DK), blk(CHUNK + GDN_DK, 2 * CHUNK, heads=pairs),
                   pl.BlockSpec((None, cpb, SUBLANE, LANE), lambda bi, i: (bi, i, 0, 0))],
        out_shape=[sds(lead + (CHUNK, GDN_DV), BF16), sds(lead + (2 * CHUNK, GDN_DK), BF16),
                   sds((b, ntot, 2, pairs, CHUNK + GDN_DK, 2 * CHUNK), BF16), sds((b, ntot, SUBLANE, LANE), F32)],
        compiler_params=_params("parallel", "parallel"),
        name="gdn_pre",
    )(q, k, v, sm)


def _scan_kernel(u0f, cqf, akf, ggf, mf, qdf, lgf, u0b, cqb, akb, ggb, mb, qdb, lgb,
                 gof_ref, gob_ref, lof_ref, lob_ref, sg_ref, sl_ref, *, nctx):
    step = pl.program_id(1)

    @pl.when(step == 0)
    def _():
        sg_ref[...] = jnp.zeros(sg_ref.shape, F32)
        sl_ref[...] = jnp.zeros(sl_ref.shape, F32)

    refs = ((u0f, cqf, akf, ggf, mf, qdf, lgf, gof_ref, lof_ref), (u0b, cqb, akb, ggb, mb, qdb, lgb, gob_ref, lob_ref))
    insts = [(d, h) for d in range(2) for h in range(GDN_HEADS)]
    dot = functools.partial(jnp.dot, preferred_element_type=F32)
    cps = u0f.shape[0]
    s = {i: sg_ref[i[0], i[1]] for i in insts}
    sl = {i: sl_ref[i[0], i[1]] for i in insts}
    def pair_rhs(xa, xb):
        z = jnp.zeros(xa.shape, xa.dtype)
        return jnp.concatenate([jnp.concatenate([xa, z], axis=1), jnp.concatenate([z, xb], axis=1)], axis=0)

    pairs = [(d, p) for d in range(2) for p in range(GDN_HEADS // 2)]
    for ci in range(cps):
        cd = (ci, cps - 1 - ci)
        cqs = {(d, h): dot(refs[d][1][cd[d], h], s[d, h].astype(BF16)) for d, h in insts}
        ol, r = {}, {}
        for d, p in pairs:
            olp = dot(refs[d][5][cd[d], p], pair_rhs(sl[d, 2 * p].astype(BF16), sl[d, 2 * p + 1].astype(BF16)))
            ol[d, 2 * p], ol[d, 2 * p + 1] = olp[:, 0:GLA_DV], olp[:, GLA_DV:2 * GLA_DV]
        u = {(d, h): refs[d][0][cd[d], h].astype(F32) - cqs[d, h][0:CHUNK] for d, h in insts}
        for d, p in pairs:
            rp = dot(refs[d][2][cd[d], p], pair_rhs(u[d, 2 * p].astype(BF16), u[d, 2 * p + 1].astype(BF16)))
            r[d, 2 * p], r[d, 2 * p + 1] = rp[:, 0:GDN_DV], rp[:, GDN_DV:2 * GDN_DV]
        for d, h in insts:
            lane = d * GDN_HEADS + h
            s[d, h] = s[d, h] * refs[d][3][cd[d], lane:lane + 1, :] + r[d, h][CHUNK:CHUNK + GDN_DK]
            sl[d, h] = sl[d, h] * refs[d][6][cd[d], :, lane:lane + 1] + refs[d][4][cd[d], h].astype(F32)

        @pl.when(step >= nctx)
        def _():
            for d, h in insts:
                rows = slice(cd[d] * CHUNK, (cd[d] + 1) * CHUNK)
                og = cqs[d, h][CHUNK:2 * CHUNK] + r[d, h][0:CHUNK]
                refs[d][7][rows, h * GDN_DV:(h + 1) * GDN_DV] = og.astype(BF16)
                refs[d][8][rows, h * GLA_DV:(h + 1) * GLA_DV] = ol[d, h].astype(BF16)

    for d, h in insts:
        sg_ref[d, h] = s[d, h]
        sl_ref[d, h] = sl[d, h]


def _scan(u0, cq, akd, ggl, m, qd, lgl, nctx, cps):
    b, nchunks = u0.shape[:2]
    assert nctx % cps == 0 and nchunks % cps == 0
    nlat = nchunks - nctx
    ntot, nctx = nchunks // cps, nctx // cps

    def specs(d):
        def chunk(bi, s):
            return (bi, _ext_chunk(d, s, nctx, ntot))

        def big(*tail, heads=GDN_HEADS):
            return pl.BlockSpec((None, cps, None, heads) + tail, lambda bi, s: chunk(bi, s) + (d, 0, 0, 0))

        def small(*tail):
            return pl.BlockSpec((None, cps) + tail, lambda bi, s: chunk(bi, s) + (0, 0))

        half = GDN_HEADS // 2
        return [big(CHUNK, GDN_DV), big(2 * CHUNK, GDN_DK), big(CHUNK + GDN_DK, 2 * CHUNK, heads=half),
                small(SUBLANE, LANE), big(GLA_DK, GLA_DV), big(CHUNK, 2 * GLA_DK, heads=half), small(GLA_DK, LANE)]

    def out_spec(d, width):
        return pl.BlockSpec((None, cps * CHUNK, width), lambda bi, s: (bi, _lat_chunk(d, s, nctx, ntot), 0))

    sds = jax.ShapeDtypeStruct
    return pl.pallas_call(
        functools.partial(_scan_kernel, nctx=nctx),
        grid=(b, ntot),
        in_specs=specs(0) + specs(1),
        out_specs=[out_spec(0, GDN_V), out_spec(1, GDN_V), out_spec(0, GLA_V), out_spec(1, GLA_V)],
        out_shape=[sds((b, nlat * CHUNK, GDN_V), BF16)] * 2 + [sds((b, nlat * CHUNK, GLA_V), BF16)] * 2,
        scratch_shapes=[pltpu.VMEM((2, GDN_HEADS, GDN_DK, GDN_DV), F32),
                        pltpu.VMEM((2, GLA_HEADS, GLA_DK, GLA_DV), F32)],
        compiler_params=_params("parallel", "arbitrary"),
        name="scan",
    )(u0, cq, akd, ggl, m, qd, lgl, u0, cq, akd, ggl, m, qd, lgl)


def _gla_pre_kernel(q_ref, k_ref, v_ref, la_ref, m_ref, qd_ref, gl_ref, intra_ref, *, cpb):
    tri = [_tri_masks(d) for d in range(2)]
    for c in range(cpb):
        rows = slice(c * CHUNK, (c + 1) * CHUNK)
        q = q_ref[rows, :] * (GLA_DK ** -0.5)
        k = k_ref[rows, :]
        v = v_ref[rows, :]
        la = la_ref[rows, :]
        pref = _cumsum_rows(tri[0][0], la)
        tot = jnp.sum(la, axis=0, keepdims=True)
        lanes = lax.broadcasted_iota(jnp.int32, la.shape, 1)
        bwd_lane = lanes >= GLA_QK
        bc = jnp.where(bwd_lane, tot - pref + la, pref)
        mid = CHUNK // 2
        ref = jnp.where(bwd_lane[0:1, :], bc[CHUNK - 1 - mid:CHUNK - mid, :], bc[mid:mid + 1, :])
        q2 = jnp.concatenate([q, q], axis=1)
        k2 = jnp.concatenate([k, k], axis=1)
        qa = q2 * jnp.exp(bc - ref)
        ka = k2 * jnp.exp(ref - bc)
        qd = (q2 * jnp.exp(bc)).astype(BF16)
        kdt = (k2 * jnp.exp(tot - bc)).T
        glcol = jnp.exp(jnp.sum(la.T, axis=1, keepdims=True))
        out_lanes = lax.broadcasted_iota(jnp.int32, (GLA_DK, LANE), 1)
        gl = jnp.zeros((GLA_DK, LANE), F32)
        for h in range(GLA_HEADS):
            vh = v[:, h * GLA_DV:(h + 1) * GLA_DV]
            intra = None
            for d in range(2):
                cs = slice(d * GLA_QK + h * GLA_DK, d * GLA_QK + (h + 1) * GLA_DK)
                attn = jnp.where(tri[d][0], _dot_nt(qa[:, cs], ka[:, cs]), 0.0)
                im = _dot(jnp.concatenate([attn, kdt[cs, :]], axis=0), vh)
                intra = im[0:CHUNK] if intra is None else intra + im[0:CHUNK]
                m_ref[c, d, h] = im[CHUNK:CHUNK + GLA_DK].astype(BF16)
                gl = jnp.where(out_lanes == d * GLA_HEADS + h, glcol[cs, :], gl)
            intra_ref[rows, h * GLA_DV:(h + 1) * GLA_DV] = intra.astype(BF16)
        for d in range(2):
            for p in range(GLA_HEADS // 2):
                qd_ref[c, d, p] = qd[:, d * GLA_QK + 2 * p * GLA_DK:d * GLA_QK + 2 * (p + 1) * GLA_DK]
        gl_ref[c] = gl


def _gla_pre(rest, loga, cpb):
    b, lext, _ = rest.shape
    ntot = lext // CHUNK
    rows = cpb * CHUNK
    lead = (b, ntot, 2, GLA_HEADS)

    def col(colblk):
        return lambda bi, i: (bi, i, colblk)

    def blk(*tail, heads=GLA_HEADS):
        return pl.BlockSpec((None, cpb, 2, heads) + tail, lambda bi, i: (bi, i, 0, 0, 0, 0))

    sds = jax.ShapeDtypeStruct
    return pl.pallas_call(
        functools.partial(_gla_pre_kernel, cpb=cpb),
        grid=(b, ntot // cpb),
        in_specs=[pl.BlockSpec((None, rows, GLA_QK), col((COL_LQ - COL_GZ) // GLA_QK)),
                  pl.BlockSpec((None, rows, GLA_QK), col((COL_LK - COL_GZ) // GLA_QK)),
                  pl.BlockSpec((None, rows, GLA_V), col((COL_LV - COL_GZ) // GLA_V)),
                  pl.BlockSpec((None, rows, 2 * GLA_QK), col(0))],
        out_specs=[blk(GLA_DK, GLA_DV), blk(CHUNK, 2 * GLA_DK, heads=GLA_HEADS // 2),
                   pl.BlockSpec((None, cpb, GLA_DK, LANE), lambda bi, i: (bi, i, 0, 0)),
                   pl.BlockSpec((None, rows, GLA_V), col(0))],
        out_shape=[sds(lead + (GLA_DK, GLA_DV), BF16), sds((b, ntot, 2, GLA_HEADS // 2, CHUNK, 2 * GLA_DK), BF16),
                   sds((b, ntot, GLA_DK, LANE), F32), sds((b, lext, GLA_V), BF16)],
        compiler_params=_params("parallel", "parallel"),
        name="gla_pre",
    )(rest, rest, rest, loga)


def _head_gate(o, gain, z, heads, dv):
    parts = []
    for h in range(heads):
        sl = slice(h * dv, (h + 1) * dv)
        oh = o[:, sl]
        parts.append(oh * lax.rsqrt(jnp.mean(oh * oh, axis=-1, keepdims=True) + EPS) * gain * _silu(z[:, sl]))
    return jnp.concatenate(parts, axis=-1)


def _ab_mix(refs, w_ref):
    gf_ref, gb_ref, lf_ref, lb_ref, li_ref, gz_ref, lg_ref, gn_ref, ln_ref = refs
    og = gf_ref[...].astype(F32) + gb_ref[...].astype(F32)
    ol = lf_ref[...].astype(F32) + lb_ref[...].astype(F32) + li_ref[0].astype(F32)
    yg = _head_gate(og, gn_ref[...], gz_ref[0], GDN_HEADS, GDN_DV)
    yl = _head_gate(ol, ln_ref[...], lg_ref[0], GLA_HEADS, GLA_DV)
    return (jnp.dot(yg.astype(BF16), w_ref[0:GDN_V, :], preferred_element_type=F32)
            + jnp.dot(yl.astype(BF16), w_ref[GDN_V:GDN_V + GLA_V, :], preferred_element_type=F32))


def _hy_mix(refs, w_ref):
    return jnp.dot(refs[0][...].astype(BF16), w_ref[...], preferred_element_type=F32)


def _tail_kernel(*refs, n_mix, mix_fn, nsplit, final_norm):
    mix_refs = refs[:n_mix]
    wo_ref, h_ref, g1_ref, gn_ref, sh_ref, sc_ref, g2_ref, w1_ref, w3_ref, w2_ref, fn_ref, o_ref = refs[n_mix:]
    h = h_ref[...] + g1_ref[...] * mix_fn(mix_refs, wo_ref)
    a = _modulated(h, gn_ref[...], sh_ref[...], sc_ref[...]).astype(BF16)
    fc = w1_ref.shape[1] // nsplit
    acc = jnp.zeros(h.shape, F32)
    for j in range(nsplit):
        u1 = jnp.dot(a, w1_ref[:, j * fc:(j + 1) * fc], preferred_element_type=F32)
        u3 = jnp.dot(a, w3_ref[:, j * fc:(j + 1) * fc], preferred_element_type=F32)
        acc = acc + jnp.dot((_silu(u1) * u3).astype(BF16), w2_ref[j * fc:(j + 1) * fc, :],
                            preferred_element_type=F32)
    y = h + g2_ref[...] * acc
    if final_norm:
        y = y * lax.rsqrt(jnp.mean(y * y, axis=-1, keepdims=True) + EPS) * fn_ref[...]
    o_ref[...] = y


def _tail(mix_args, mix_specs, mix_fn, w_out, h, g1, gain, shift, scale, g2, w1, w3, w2, final_w, tm, final_norm):
    b, l, d = h.shape
    dff = w1.shape[1]

    def tok(bi, i):
        return (bi, i, 0)

    def vec(bi, i):
        return (bi, 0, 0)

    def const(bi, i):
        return (0, 0)

    return pl.pallas_call(
        functools.partial(_tail_kernel, n_mix=len(mix_args), mix_fn=mix_fn, nsplit=2, final_norm=final_norm),
        grid=(b, l // tm),
        in_specs=list(mix_specs) + [
            _resident(w_out.shape, const), pl.BlockSpec((None, tm, d), tok), pl.BlockSpec((None, 1, d), vec),
            _resident((1, d), const), pl.BlockSpec((None, 1, d), vec), pl.BlockSpec((None, 1, d), vec),
            pl.BlockSpec((None, 1, d), vec), _resident((d, dff), const), _resident((d, dff), const),
            _resident((dff, d), const), _resident((1, d), const)],
        out_specs=pl.BlockSpec((None, tm, d), tok),
        out_shape=jax.ShapeDtypeStruct((b, l, d), F32),
        compiler_params=_params("parallel", "parallel"),
        name="mix_out_ffn",
    )(*mix_args, w_out, h, g1, gain.reshape(1, d), shift, scale, g2, w1, w3, w2, final_w.reshape(1, d))


def _hy_hidden_kernel(fb_ref, w1t_ref, w1c_ref, w1s_ref, b1_ref, w2_ref, b2_ref, w3_ref, b3_ref, fr_ref, o_ref,
                      *, tile, seq_len):
    half = tile // 2
    hid = o_ref.shape[1]
    lane_hi = lax.broadcasted_iota(jnp.int32, (half, 2 * hid), 1) >= hid
    rows = (pl.program_id(0) * tile + lax.broadcasted_iota(jnp.int32, (half, 2 * hid), 0)
            + jnp.where(lane_hi, half, 0))
    pos = jnp.where(rows < seq_len, rows, 2 * seq_len - rows).astype(F32)
    t = pos * (1.0 / (seq_len - 1))
    ang = (pos * (2.0 * math.pi / seq_len)) * fb_ref[...]
    fr = fr_ref[...]
    pre = t * w1t_ref[...] + _dot_hi(jnp.cos(ang), w1c_ref[...]) - _dot_hi(jnp.sin(ang), w1s_ref[...])
    hdn = jnp.sin(fr * (pre + b1_ref[...]))
    hdn = jnp.sin(fr * (_dot_hi(hdn, w2_ref[...]) + b2_ref[...]))
    hdn = jnp.sin(fr * (_dot_hi(hdn, w3_ref[...]) + b3_ref[...]))
    o_ref[0:half, :] = hdn[:, 0:hid]
    o_ref[half:tile, :] = hdn[:, hid:2 * hid]


def _hy_hidden(w1, b1, w2, b2, w3, b3, freq, seq_len, tile):
    hid = w1.shape[1]
    assert 2 * hid == LANE and HY_BANDS <= hid

    def twice(v):
        return jnp.tile(v.reshape(1, hid), (1, 2))

    def block_diag(w):
        k = w.shape[0]
        return jnp.zeros((2 * hid, 2 * hid), F32).at[:k, :hid].set(w).at[hid:hid + k, hid:].set(w)

    bands = jnp.linspace(1e-4, HY_BANDS - 1, HY_BANDS, dtype=F32)
    fb = jnp.zeros((1, 2 * hid), F32).at[0, :HY_BANDS].set(bands).at[0, hid:hid + HY_BANDS].set(bands)
    args = (fb, twice(w1[0]), block_diag(w1[1:1 + HY_BANDS]), block_diag(w1[1 + HY_BANDS:1 + 2 * HY_BANDS]),
            twice(b1), block_diag(w2), twice(b2), block_diag(w3), twice(b3), twice(freq))
    return pl.pallas_call(
        functools.partial(_hy_hidden_kernel, tile=tile, seq_len=seq_len),
        grid=(2 * seq_len // tile,),
        in_specs=[pl.BlockSpec(a.shape, lambda i: (0, 0)) for a in args],
        out_specs=pl.BlockSpec((tile, hid), lambda i: (i, 0)),
        out_shape=jax.ShapeDtypeStruct((2 * seq_len, hid), F32),
        compiler_params=_params("parallel"),
        name="hy_hidden",
    )(*args)


def _fft_matrices(r):
    n = r * r
    idx = jnp.arange(r, dtype=jnp.int32)

    def cis(m, period):
        ang = (2.0 * math.pi / period) * m.astype(F32)
        return jnp.cos(ang), -jnp.sin(ang)

    fr, fi = cis((idx[:, None] * idx[None, :]) % r, r)
    half = r // 2
    f1_data = jnp.block([[fr[:, :half], -fi[:, :half]], [fi[:, :half], fr[:, :half]]])
    f1_real = jnp.concatenate([fr, fi], axis=0)
    f3 = jnp.block([[fr[:half, :], fi[:half, :]], [-fi[:half, :], fr[:half, :]]]) * (1.0 / n)
    tr, ti = cis((idx[:, None] * idx[None, :]) % n, n)
    gr = tr[:, None, :] * fr[None, :, :] - ti[:, None, :] * fi[None, :, :]
    gi = tr[:, None, :] * fi[None, :, :] + ti[:, None, :] * fr[None, :, :]
    g = jnp.concatenate([jnp.concatenate([gr, -gi], axis=2), jnp.concatenate([gi, gr], axis=2)], axis=1)
    gh = jnp.swapaxes(g, 1, 2)
    return f1_data.astype(BF16), f1_real.astype(BF16), f3.astype(BF16), g.astype(BF16), gh.astype(BF16)


FFT_BG = 16
FFT_DT = 256


def _slab(ref, j):
    return jnp.concatenate([ref[i, :, j, :] for i in range(ref.shape[0])], axis=0)


def _spread_rows(scr, j, bg, val):
    for lg in range(val.shape[1] // LANE):
        scr[lg, pl.ds(j, val.shape[0], stride=bg), :] = val[:, lg * LANE:(lg + 1) * LANE]


def _flush(scr, o_ref):
    for lg in range(scr.shape[0]):
        o_ref[..., lg * LANE:(lg + 1) * LANE] = scr[lg].reshape(o_ref.shape[:-1] + (LANE,)).astype(o_ref.dtype)


def _time_block(arr, sel, n):
    lead = arr.ndim - 4
    return pl.BlockSpec((None,) * lead + (n, arr.shape[-3], FFT_BG, FFT_DT),
                        lambda p, g, dd: tuple(sel) + (p, 0, g, dd))


def _fft_stage1_kernel(f_ref, x_ref, o_ref, ps):
    bg = x_ref.shape[2]
    for j in range(bg):
        p = jnp.dot(f_ref[...], _slab(x_ref, j).astype(BF16), preferred_element_type=F32)
        _spread_rows(ps, j, bg, p)
    _flush(ps, o_ref)


def _fft_stage1(f1, x5, sel, n):
    r2, _ = f1.shape
    r = r2 // 2
    items, _, rb, d = x5.shape[-4:]
    groups = items // n
    return pl.pallas_call(
        _fft_stage1_kernel,
        grid=(groups, rb // FFT_BG, d // FFT_DT),
        in_specs=[_resident(f1.shape, lambda p, g, dd: (0, 0)), _time_block(x5, sel, n)],
        out_specs=pl.BlockSpec((None, 2, r, FFT_BG, FFT_DT), lambda p, g, dd: (p, 0, 0, g, dd)),
        out_shape=jax.ShapeDtypeStruct((groups, 2, r, rb, d), BF16),
        scratch_shapes=[pltpu.VMEM((FFT_DT // LANE, r2 * FFT_BG, LANE), F32)],
        compiler_params=_params("parallel", "parallel", "parallel"),
        name="fft_stage1",
    )(f1, x5)


def _fft_filter_stage1_kernel(f_ref, hdn_ref, fo_ref, dl_ref, o_ref, ps, *, seq_len):
    r, bg, _ = hdn_ref.shape
    orders = fo_ref.shape[1]
    half = r // 2
    a_idx = lax.broadcasted_iota(jnp.int32, (r, 1), 0)
    for j in range(bg):
        rows = a_idx * r + (pl.program_id(0) * bg + j)
        pos = jnp.where(rows < seq_len, rows, 2 * seq_len - rows).astype(F32)
        t = pos * (1.0 / (seq_len - 1))
        window = jnp.where(rows != seq_len, jnp.exp(-t * dl_ref[...]) + HY_SHIFT, 0.0)
        hj = hdn_ref[:, j, :]
        for n in range(orders):
            taps = jnp.concatenate([_dot(hj[:half], fo_ref[0, n]), _dot(hj[half:], fo_ref[1, n])], axis=0) * window
            p = jnp.dot(f_ref[...], taps.astype(BF16), preferred_element_type=F32)
            _spread_rows(ps.at[n], j, bg, p)
    for n in range(orders):
        _flush(ps.at[n], o_ref.at[n])


def _fft_filter_stage1(f1, hdn3, fo, deltas, seq_len):
    r2, r = f1.shape
    hid = hdn3.shape[-1]
    _, orders, _, d = fo.shape
    lanes = FFT_DT // LANE
    return pl.pallas_call(
        functools.partial(_fft_filter_stage1_kernel, seq_len=seq_len),
        grid=(r // FFT_BG, d // FFT_DT),
        in_specs=[_resident(f1.shape, lambda g, dd: (0, 0)),
                  pl.BlockSpec((r, FFT_BG, hid), lambda g, dd: (0, g, 0)),
                  pl.BlockSpec((2, orders, hid, FFT_DT), lambda g, dd: (0, 0, 0, dd)),
                  pl.BlockSpec((1, FFT_DT), lambda g, dd: (0, dd))],
        out_specs=pl.BlockSpec((orders, 2, r, FFT_BG, FFT_DT), lambda g, dd: (0, 0, 0, g, dd)),
        out_shape=jax.ShapeDtypeStruct((orders, 2, r, r, d), BF16),
        scratch_shapes=[pltpu.VMEM((orders, lanes, r2 * FFT_BG, LANE), F32)],
        compiler_params=_params("parallel", "parallel"),
        name="fft_filter_stage1",
    )(f1, hdn3, fo, deltas)


def _fft_stage2_kernel(g_ref, gh_ref, k_ref, p_ref, o_ref):
    r = p_ref.shape[2]
    kq = k_ref[...]
    h = jnp.dot(g_ref[...], kq.reshape(2 * r, kq.shape[2]), preferred_element_type=F32)
    hr, hi = h[:r], h[r:]
    for n in range(p_ref.shape[0]):
        p = p_ref[n]
        x = jnp.dot(g_ref[...], p.reshape(2 * r, p.shape[2]), preferred_element_type=F32)
        xr, xi = x[:r], x[r:]
        y = jnp.concatenate([xr * hr - xi * hi, xr * hi + xi * hr], axis=0)
        q = jnp.dot(gh_ref[...], y.astype(BF16), preferred_element_type=F32)
        o_ref[n] = q.reshape(o_ref.shape[1:]).astype(o_ref.dtype)


def _fft_stage2(g, gh, pk, order, p5):
    pairs, _, r, _, d = p5.shape
    return pl.pallas_call(
        _fft_stage2_kernel,
        grid=(r,),
        in_specs=[pl.BlockSpec((None, 2 * r, 2 * r), lambda kk: (kk, 0, 0)),
                  pl.BlockSpec((None, 2 * r, 2 * r), lambda kk: (kk, 0, 0)),
                  pl.BlockSpec((None, 2, None, r, d), lambda kk: (order, 0, kk, 0, 0)),
                  pl.BlockSpec((pairs, 2, None, r, d), lambda kk: (0, 0, kk, 0, 0))],
        out_specs=pl.BlockSpec((pairs, 2, None, r, d), lambda kk: (0, 0, kk, 0, 0)),
        out_shape=jax.ShapeDtypeStruct(p5.shape, BF16),
        compiler_params=_params("parallel"),
        name="fft_stage2",
    )(g, gh, pk, p5)


def _fft_stage3_kernel(f3_ref, f1_ref, q_ref, z_ref, gate_ref, skip_ref, *rest, chain):
    if chain:
        zo_ref, po_ref, qs, zs, ps = rest
    else:
        zo_ref, qs, zs = rest
    bg = z_ref.shape[2]
    rows = q_ref.shape[0] * q_ref.shape[1]
    for lg in range(qs.shape[0]):
        qs[lg] = q_ref[..., lg * LANE:(lg + 1) * LANE].astype(F32).reshape(rows * bg, LANE)
    skip = skip_ref[...]
    for j in range(bg):
        qj = jnp.concatenate([qs[lg, pl.ds(j, rows, stride=bg), :] for lg in range(qs.shape[0])], axis=1)
        y = jnp.dot(f3_ref[...], qj.astype(BF16), preferred_element_type=F32)
        zn = _slab(gate_ref, j) * (y + _slab(z_ref, j) * skip)
        _spread_rows(zs, j, bg, zn)
        if chain:
            _spread_rows(ps, j, bg, jnp.dot(f1_ref[...], zn.astype(BF16), preferred_element_type=F32))
    _flush(zs, zo_ref)
    if chain:
        _flush(ps, po_ref)


def _fft_stage3(f3, f1, q5, z5, z_sel, g5, g_sel, skip, chain):
    pairs, _, r, rb, d = q5.shape
    a = z5.shape[-3]
    sds = jax.ShapeDtypeStruct
    out_specs = [pl.BlockSpec((2, a, FFT_BG, FFT_DT), lambda p, g, dd: (p, 0, g, dd))]
    out_shape = [sds((2 * pairs, a, rb, d), F32)]
    lanes = FFT_DT // LANE
    scratch = [pltpu.VMEM((lanes, 2 * r * FFT_BG, LANE), F32), pltpu.VMEM((lanes, 2 * a * FFT_BG, LANE), F32)]
    if chain:
        out_specs.append(pl.BlockSpec((None, 2, r, FFT_BG, FFT_DT), lambda p, g, dd: (p, 0, 0, g, dd)))
        out_shape.append(sds(q5.shape, BF16))
        scratch.append(pltpu.VMEM((lanes, 2 * r * FFT_BG, LANE), F32))
    return pl.pallas_call(
        functools.partial(_fft_stage3_kernel, chain=chain),
        grid=(pairs, rb // FFT_BG, d // FFT_DT),
        in_specs=[_resident(f3.shape, lambda p, g, dd: (0, 0)), _resident(f1.shape, lambda p, g, dd: (0, 0)),
                  pl.BlockSpec((None, 2, r, FFT_BG, FFT_DT), lambda p, g, dd: (p, 0, 0, g, dd)),
                  _time_block(z5, z_sel, 2), _time_block(g5, g_sel, 2),
                  pl.BlockSpec((1, FFT_DT), lambda p, g, dd: (0, dd))],
        out_specs=out_specs,
        out_shape=out_shape,
        scratch_shapes=scratch,
        compiler_params=_params("parallel", "parallel", "parallel"),
        name="fft_stage3",
    )(f3, f1, q5, z5, g5, skip)


TAIL_ROWS = 512
HY_FRONT_ROWS = 512
AB_FRONT_ROWS = 384


def _regroup_ab_weights(w_in):
    d = w_in.shape[0]
    sizes = (GDN_QK, GDN_QK, GDN_V, GDN_V, 2 * GDN_HEADS, 2 * GDN_HEADS, GLA_QK, GLA_QK, GLA_V, GLA_V,
             2 * GLA_GATE_RANK)
    offs = [0]
    for s in sizes:
        offs.append(offs[-1] + s)
    gq, gk, gv, gz, ga, gb, lq, lk, lv, lg, llr = (w_in[:, offs[i]:offs[i + 1]] for i in range(len(sizes)))
    pad = jnp.zeros((d, LANE - (ga.shape[1] + gb.shape[1] + llr.shape[1])), w_in.dtype)
    return jnp.concatenate([gq, gk, gv, gz, lq, lk, lv, lg, ga, gb, llr, pad], axis=1)


def _lane_row(vals, offset):
    flat = vals.reshape(-1).astype(F32)
    return jnp.zeros((1, LANE), F32).at[0, offset:offset + flat.shape[0]].set(flat)


def _even_layer(h, ctx, mods, norm1, norm2, w_in, conv_w, a_log, dt_bias, gdn_norm, gate_w, gate_b, gla_norm,
                w_out, w1, w3, w2, final_w, final_norm):
    b, l, d = h.shape
    lctx = ctx.shape[1]
    tm = TAIL_ROWS
    sh1, sc1, g1, sh2, sc2, g2 = mods["lat"]
    csh1, csc1 = mods["ctx"][0], mods["ctx"][1]
    xe = jnp.concatenate([ctx, h], axis=1)
    shift = jnp.stack([jnp.broadcast_to(csh1, (b, d)), sh1], axis=1)[:, :, None, :]
    scale = jnp.stack([jnp.broadcast_to(csc1, (b, d)), sc1], axis=1)[:, :, None, :]
    gw = jnp.zeros((LANE, 2 * GLA_QK), F32)
    for dd in range(2):
        gw = gw.at[SM_LR + dd * GLA_GATE_RANK:SM_LR + (dd + 1) * GLA_GATE_RANK,
                   dd * GLA_QK:(dd + 1) * GLA_QK].set(gate_w[dd])
    q, k, v, sm, loga, rest = _ab_front(xe, norm1, shift, scale, _regroup_ab_weights(w_in).astype(BF16), conv_w,
                                        _lane_row(a_log, SM_G), _lane_row(dt_bias, SM_G), gw,
                                        gate_b.reshape(1, 2 * GLA_QK), AB_FRONT_ROWS, lctx)
    nctx = lctx // CHUNK
    u0, cq, akd, ggl = _gdn_pre(q, k, v, sm, 2)
    m, qd, lgl, ol_intra = _gla_pre(rest, loga, 2)
    og_f, og_b, ol_f, ol_b = _scan(u0, cq, akd, ggl, m, qd, lgl, nctx, 4)
    def lat(width):
        return pl.BlockSpec((None, tm, width), lambda bi, i: (bi, i, 0))

    def ext(width, colblk):
        return pl.BlockSpec((pl.Element(1), pl.Element(tm), pl.Element(width)),
                            lambda bi, i: (bi, SUBLANE * (lctx // SUBLANE + i * (tm // SUBLANE)), colblk * width))

    mix_args = (og_f, og_b, ol_f, ol_b, ol_intra, rest, rest, gdn_norm.reshape(1, GDN_DV), gla_norm.reshape(1, GLA_DV))
    mix_specs = [lat(GDN_V), lat(GDN_V), lat(GLA_V), lat(GLA_V), ext(GLA_V, 0), ext(GDN_V, 0),
                 ext(GLA_V, (COL_LG - COL_GZ) // GLA_V), _resident((1, GDN_DV), lambda bi, i: (0, 0)),
                 _resident((1, GLA_DV), lambda bi, i: (0, 0))]
    return _tail(mix_args, mix_specs, _ab_mix, w_out.astype(BF16), h, g1[:, None, :], norm2, sh2[:, None, :],
                 sc2[:, None, :], g2[:, None, :], w1.astype(BF16), w3.astype(BF16), w2.astype(BF16), final_w, tm,
                 final_norm)


def _hyena_layer(h, mods, norm1, norm2, w_in, conv_w, pw1, pb1, pw2, pb2, pw3, pb3, freq, filt_out, skip, w_out,
                 w1, w3, w2, final_w, final_norm):
    b, l, d = h.shape
    tm = TAIL_ROWS
    sh1, sc1, g1, sh2, sc2, g2 = mods["lat"]
    u3 = _hy_front(h, norm1, sh1[:, None, :], sc1[:, None, :], w_in.astype(BF16), conv_w, HY_FRONT_ROWS)

    n2 = 2 * l
    r = math.isqrt(n2)
    assert r * r == n2 and b % 2 == 0
    orders = skip.shape[0]
    hdn = _hy_hidden(pw1, pb1, pw2, pb2, pw3, pb3, freq, l, 512)
    fo = jnp.transpose(filt_out.reshape(filt_out.shape[0], 2, orders, d), (1, 2, 0, 3))
    deltas = jnp.abs(jnp.linspace(HY_MIN_DECAY, HY_MAX_DECAY, d, dtype=F32))[None, :]

    f1_data, f1_real, f3, g, gh = _fft_matrices(r)
    pk = _fft_filter_stage1(f1_real, hdn.reshape(r, r, hdn.shape[1]), fo, deltas, l)

    u5 = u3.reshape(3, b, r // 2, r, d)
    z_arr, z_sel = u5, (0,)
    p = _fft_stage1(f1_data, z_arr, z_sel, 2)
    for n in range(orders):
        q = _fft_stage2(g, gh, pk, n, p)
        chain = n + 1 < orders
        res = _fft_stage3(f3, f1_data, q, z_arr, z_sel, u5, (n + 1,), skip[n][None, :], chain)
        z_arr, z_sel = res[0], ()
        if chain:
            p = res[1]
    mix_specs = [pl.BlockSpec((None, tm, d), lambda bi, i: (bi, i, 0))]
    return _tail((z_arr.reshape(b, l, d),), mix_specs, _hy_mix, w_out.astype(BF16), h, g1[:, None, :], norm2,
                 sh2[:, None, :], sc2[:, None, :], g2[:, None, :], w1.astype(BF16), w3.astype(BF16), w2.astype(BF16),
                 final_w, tm, final_norm)


def kernel(x, c, ctx, c_ctx, mod_w, mod_b, norm1_w, norm2_w, ab_w_in, ab_conv_w, gdn_a_log, gdn_dt_bias, gdn_norm_w, gla_gate_w, gla_gate_b, gla_norm_w, ab_w_out, hy_w_in, hy_conv_w, hy_pos_w1, hy_pos_b1, hy_pos_w2, hy_pos_b2, hy_pos_w3, hy_pos_b3, hy_freq, hy_filt_out, hy_skip, hy_w_out, ffn_w1, ffn_w3, ffn_w2, final_norm_w):
    b, l, d = x.shape
    depth = mod_w.shape[0]
    assert b + 1 <= SUBLANE
    cvec = jnp.concatenate([c, c_ctx[None, :], jnp.zeros((SUBLANE - b - 1, d), F32)], axis=0)
    mod = _modulation(cvec, mod_w, mod_b)
    h = x
    for i in range(depth):
        mods = {"lat": [mod[i, :b, j * d:(j + 1) * d] for j in range(6)],
                "ctx": [mod[i, b, j * d:(j + 1) * d] for j in range(6)]}
        last = i == depth - 1
        if i % 2 == 0:
            assert not any(j % 2 == 0 for j in range(i + 1, depth))
            e = i // 2
            h = _even_layer(h, ctx, mods, norm1_w[i], norm2_w[i], ab_w_in[e], ab_conv_w[e], gdn_a_log[e],
                            gdn_dt_bias[e], gdn_norm_w[e], gla_gate_w[e], gla_gate_b[e], gla_norm_w[e], ab_w_out[e],
                            ffn_w1[i], ffn_w3[i], ffn_w2[i], final_norm_w, last)
        else:
            o = i // 2
            h = _hyena_layer(h, mods, norm1_w[i], norm2_w[i], hy_w_in[o], hy_conv_w[o], hy_pos_w1[o], hy_pos_b1[o],
                             hy_pos_w2[o], hy_pos_b2[o], hy_pos_w3[o], hy_pos_b3[o], hy_freq[o], hy_filt_out[o],
                             hy_skip[o], hy_w_out[o], ffn_w1[i], ffn_w3[i], ffn_w2[i], final_norm_w, last)
    return h
```

```python
import functools
import math

import jax
import jax.numpy as jnp
from jax import lax
from jax.experimental import pallas as pl
from jax.experimental.pallas import tpu as pltpu

F32 = jnp.float32
BF16 = jnp.bfloat16
HI = lax.Precision.HIGHEST

EPS = 1e-6
CHUNK = 64
GDN_HEADS, GDN_DK, GDN_DV = 4, 128, 128
GLA_HEADS, GLA_DK, GLA_DV = 4, 64, 128
GLA_GATE_RANK = 16
GLA_GATE_TAU = 16.0
GDN_QK = GDN_HEADS * GDN_DK
GDN_V = GDN_HEADS * GDN_DV
GLA_QK = GLA_HEADS * GLA_DK
GLA_V = GLA_HEADS * GLA_DV
HY_EMB = 33
HY_BANDS = (HY_EMB - 1) // 2
HY_MIN_DECAY = math.log(1e-2) / 1.5
HY_MAX_DECAY = math.log(1e-2) / 0.3
HY_SHIFT = 0.05

LANE = 128
SUBLANE = 8
VMEM_LIMIT_BYTES = 56 * 1024 * 1024

COL_GQ, COL_GK, COL_GV, COL_GZ = 0, 512, 1024, 1536
COL_LQ, COL_LK, COL_LV, COL_LG = 2048, 2304, 2560, 3072
COL_SMALL = 3584
SM_G, SM_B, SM_LR = 0, 8, 16


def _params(*sem):
    return pltpu.CompilerParams(dimension_semantics=sem, vmem_limit_bytes=VMEM_LIMIT_BYTES)


def _resident(shape, index_map):
    return pl.BlockSpec(shape, index_map, pipeline_mode=pl.Buffered(1))


def _dot(a, b):
    return jnp.dot(a.astype(BF16), b.astype(BF16), preferred_element_type=F32)


def _dot_nt(a, b):
    return lax.dot_general(a.astype(BF16), b.astype(BF16), (((1,), (1,)), ((), ())),
                           preferred_element_type=F32)


def _dot_hi(a, b):
    return jnp.dot(a, b, preferred_element_type=F32, precision=HI)


def _sigmoid(x):
    return 1.0 / (1.0 + jnp.exp(-x))


def _silu(x):
    return x * _sigmoid(x)


def _softplus(x):
    return jnp.maximum(x, 0.0) + jnp.log(1.0 + jnp.exp(-jnp.abs(x)))


def _mod_kernel(c_ref, w_ref, b_ref, o_ref):
    o_ref[...] = _dot_hi(_silu(c_ref[...]), w_ref[...]) + b_ref[...]


def _modulation(cvec, mod_w, mod_b):
    depth, d, n = mod_w.shape
    tn = n // 4
    return pl.pallas_call(
        _mod_kernel,
        grid=(depth, n // tn),
        in_specs=[pl.BlockSpec((SUBLANE, d), lambda i, j: (0, 0)),
                  pl.BlockSpec((None, d, tn), lambda i, j: (i, 0, j)),
                  pl.BlockSpec((None, 1, tn), lambda i, j: (i, 0, j))],
        out_specs=pl.BlockSpec((None, SUBLANE, tn), lambda i, j: (i, 0, j)),
        out_shape=jax.ShapeDtypeStruct((depth, SUBLANE, n), F32),
        compiler_params=_params("parallel", "parallel"),
        name="modulation",
    )(cvec, mod_w, mod_b.reshape(depth, 1, n))


CONV_HALO = SUBLANE


def _modulated(x, gain, shift, scale):
    ms = jnp.mean(x * x, axis=-1, keepdims=True)
    return x * lax.rsqrt(ms + EPS) * gain * (1.0 + scale) + shift


def _window_start(i, tm, win, seq_len):
    return SUBLANE * jnp.clip(i * (tm // SUBLANE) - CONV_HALO // SUBLANE, 0, (seq_len - win) // SUBLANE)


def _window_spec(tm, win, seq_len, d):
    return pl.BlockSpec((pl.Element(1), pl.Element(win), pl.Element(d)),
                        lambda bi, i: (bi, _window_start(i, tm, win, seq_len), 0))


def _conv3_rows(u, w, gidx=None, starts=(), ends=()):
    rows = u.shape[0]
    prev = pltpu.roll(u, 1, 0)
    nxt = pltpu.roll(u, rows - 1, 0)
    if starts:
        prev = jnp.where(functools.reduce(jnp.logical_or, [gidx == r for r in starts]), 0.0, prev)
    if ends:
        nxt = jnp.where(functools.reduce(jnp.logical_or, [gidx == r for r in ends]), 0.0, nxt)
    return prev * w[0:1, :] + u * w[1:2, :] + nxt * w[2:3, :]


def _hy_front_kernel(x_ref, g_ref, sh_ref, sc_ref, w_ref, cw_ref, o_ref, scr, *, tm, seq_len):
    i = pl.program_id(1)
    win = x_ref.shape[1]
    ws = _window_start(i, tm, win, seq_len)
    off = pl.multiple_of(i * tm - ws, SUBLANE)
    a = _modulated(x_ref[0], g_ref[...], sh_ref[...], sc_ref[...])
    u = jnp.dot(a.astype(BF16), w_ref[...], preferred_element_type=F32)
    cw = cw_ref[...]
    scr[...] = _conv3_rows(u, cw)

    @pl.when(i == 0)
    def _():
        scr[0:1, :] = scr[0:1, :] - u[win - 1:win, :] * cw[0:1, :]

    @pl.when(i == pl.num_programs(1) - 1)
    def _():
        scr[win - 1:win, :] = scr[win - 1:win, :] - u[0:1, :] * cw[2:3, :]

    d = o_ref.shape[-1]
    for g in range(o_ref.shape[0]):
        o_ref[g] = scr[pl.ds(off, tm), g * d:(g + 1) * d]


def _hy_front(x, gain, shift, scale, w, conv_w, tm):
    b, l, d = x.shape
    n = w.shape[1]
    groups = n // d
    win = tm + 2 * CONV_HALO

    def vec(bi, i):
        return (bi, 0, 0)

    return pl.pallas_call(
        functools.partial(_hy_front_kernel, tm=tm, seq_len=l),
        grid=(b, l // tm),
        in_specs=[_window_spec(tm, win, l, d), _resident((1, d), lambda bi, i: (0, 0)),
                  pl.BlockSpec((None, 1, d), vec), pl.BlockSpec((None, 1, d), vec),
                  _resident((d, n), lambda bi, i: (0, 0)), _resident((3, n), lambda bi, i: (0, 0))],
        out_specs=pl.BlockSpec((groups, None, tm, d), lambda bi, i: (0, bi, i, 0)),
        out_shape=jax.ShapeDtypeStruct((groups, b, l, d), F32),
        scratch_shapes=[pltpu.VMEM((win, n), F32)],
        compiler_params=_params("parallel", "parallel"),
        name="hy_front",
    )(x, gain.reshape(1, d), shift, scale, w, conv_w)


def _ab_front_kernel(x_ref, g_ref, sh_ref, sc_ref, w_ref, cw_ref, alog_ref, dtb_ref, gw_ref, gb_ref,
                     q_ref, k_ref, v_ref, so_ref, la_ref, rest_ref, scr, *, tm, lctx, lext):
    i = pl.program_id(1)
    win = x_ref.shape[1]
    ws = _window_start(i, tm, win, lext)
    off = pl.multiple_of(i * tm - ws, SUBLANE)
    gidx = ws + lax.broadcasted_iota(jnp.int32, (win, 1), 0)
    is_ctx = gidx < lctx
    shift = jnp.where(is_ctx, sh_ref[0], sh_ref[1])
    scale = jnp.where(is_ctx, sc_ref[0], sc_ref[1])
    a = _modulated(x_ref[0], g_ref[...], shift, scale)
    u = jnp.dot(a.astype(BF16), w_ref[...], preferred_element_type=F32)
    ncv = cw_ref.shape[1]
    scr[:, 0:ncv] = _conv3_rows(u[:, 0:ncv], cw_ref[...], gidx, (0, lctx), (lctx - 1, lext - 1))
    scr[:, ncv:] = u[:, ncv:]
    rows = pl.ds(off, tm)
    for h in range(GDN_HEADS):
        sl = slice(h * GDN_DK, (h + 1) * GDN_DK)
        qh = _silu(scr[rows, COL_GQ + h * GDN_DK:COL_GQ + (h + 1) * GDN_DK])
        kh = _silu(scr[rows, COL_GK + h * GDN_DK:COL_GK + (h + 1) * GDN_DK])
        q_ref[:, sl] = qh * lax.rsqrt(jnp.sum(qh * qh, axis=-1, keepdims=True) + EPS) * (GDN_DK ** -0.5)
        k_ref[:, sl] = kh * lax.rsqrt(jnp.sum(kh * kh, axis=-1, keepdims=True) + EPS)
    v_ref[...] = _silu(scr[rows, COL_GV:COL_GV + GDN_V])
    rest_ref[...] = scr[rows, COL_GZ:COL_SMALL]
    sm = scr[rows, COL_SMALL:COL_SMALL + LANE]
    g = -jnp.exp(alog_ref[...]) * _softplus(sm + dtb_ref[...])
    beta = _sigmoid(sm)
    lanes = lax.broadcasted_iota(jnp.int32, sm.shape, 1)
    so_ref[...] = jnp.where(lanes < SM_B, g, beta)
    gate = _dot(sm, gw_ref[...]) + gb_ref[...]
    la_ref[...] = (jnp.minimum(gate, 0.0) - jnp.log(1.0 + jnp.exp(-jnp.abs(gate)))) * (1.0 / GLA_GATE_TAU)


def _ab_front(xe, gain, shift, scale, w, conv_w, alog_row, dtb_row, gate_w, gate_b, tm, lctx):
    b, lext, d = xe.shape
    n = w.shape[1]
    win = tm + 2 * CONV_HALO
    nrest = COL_SMALL - COL_GZ

    def const(bi, i):
        return (0, 0)

    def tok(width):
        return pl.BlockSpec((None, tm, width), lambda bi, i: (bi, i, 0))

    sds = jax.ShapeDtypeStruct
    return pl.pallas_call(
        functools.partial(_ab_front_kernel, tm=tm, lctx=lctx, lext=lext),
        grid=(b, lext // tm),
        in_specs=[_window_spec(tm, win, lext, d), _resident((1, d), const),
                  pl.BlockSpec((None, 2, 1, d), lambda bi, i: (bi, 0, 0, 0)),
                  pl.BlockSpec((None, 2, 1, d), lambda bi, i: (bi, 0, 0, 0)),
                  _resident((d, n), const), _resident(conv_w.shape, const),
                  _resident((1, LANE), const), _resident((1, LANE), const),
                  _resident((LANE, 2 * GLA_QK), const), _resident((1, 2 * GLA_QK), const)],
        out_specs=[tok(GDN_QK), tok(GDN_QK), tok(GDN_V), tok(LANE), tok(2 * GLA_QK), tok(nrest)],
        out_shape=[sds((b, lext, GDN_QK), F32), sds((b, lext, GDN_QK), F32), sds((b, lext, GDN_V), F32),
                   sds((b, lext, LANE), F32), sds((b, lext, 2 * GLA_QK), F32), sds((b, lext, nrest), F32)],
        scratch_shapes=[pltpu.VMEM((win, n), F32)],
        compiler_params=_params("parallel", "parallel"),
        name="ab_front",
    )(xe, gain.reshape(1, d), shift, scale, w, conv_w, alog_row, dtb_row, gate_w, gate_b)


def _ext_chunk(d, s, nctx, ntot):
    if d == 0:
        return s
    return jnp.where(s < nctx, nctx - 1 - s, ntot + nctx - 1 - s)


def _lat_chunk(d, s, nctx, ntot):
    nlat = ntot - nctx
    if d == 0:
        return jnp.maximum(s - nctx, 0)
    return jnp.minimum(ntot - 1 - s, nlat - 1)


def _tri_masks(d):
    ii = lax.broadcasted_iota(jnp.int32, (CHUNK, CHUNK), 0)
    jj = lax.broadcasted_iota(jnp.int32, (CHUNK, CHUNK), 1)
    if d == 0:
        return ii >= jj, ii > jj, ii == jj
    return ii <= jj, ii < jj, ii == jj


def _cumsum_rows(incl, x):
    m = incl.astype(BF16)
    x1 = x.astype(BF16)
    r1 = x - x1.astype(F32)
    x2 = r1.astype(BF16)
    x3 = (r1 - x2.astype(F32)).astype(BF16)
    dot = functools.partial(jnp.dot, preferred_element_type=F32)
    return dot(m, x1) + dot(m, x2) + dot(m, x3)


def _merge_masks(d):
    ii = lax.broadcasted_iota(jnp.int32, (CHUNK, CHUNK), 0)
    jj = lax.broadcasted_iota(jnp.int32, (CHUNK, CHUNK), 1)
    lo, hi = (jj, ii) if d == 0 else (ii, jj)
    masks = {}
    s = 1
    while s < CHUNK:
        masks[s] = jnp.logical_and(ii // (2 * s) == jj // (2 * s),
                                   jnp.logical_and((hi // s) % 2 == 1, (lo // s) % 2 == 0))
        s *= 2
    return masks


def _gdn_pre_kernel(q_ref, k_ref, v_ref, sm_ref, u0_ref, cq_ref, akd_ref, gl_ref, *, cpb):
    tri = [_tri_masks(d) for d in range(2)]
    merge = [_merge_masks(d) for d in range(2)]
    insts = [(d, h) for d in range(2) for h in range(GDN_HEADS)]
    for c in range(cpb):
        rows = slice(c * CHUNK, (c + 1) * CHUNK)
        q = q_ref[rows, :]
        k = k_ref[rows, :]
        v = v_ref[rows, :]
        sm = sm_ref[rows, :]
        pref = _cumsum_rows(tri[0][0], sm)
        tot = jnp.sum(sm, axis=0, keepdims=True)
        lanes = lax.broadcasted_iota(jnp.int32, sm.shape, 1)
        bwd_lane = jnp.logical_and(lanes >= SM_G + GDN_HEADS, lanes < SM_G + 2 * GDN_HEADS)
        gc = jnp.where(bwd_lane, tot - pref + sm, pref)
        gct = gc.T
        totc = jnp.sum(sm.T, axis=1, keepdims=True)
        gl_ref[c] = jnp.exp(jnp.broadcast_to(totc[0:SUBLANE, :], (SUBLANE, LANE)))
        egc = jnp.exp(gc)
        etg = jnp.exp(tot - gc)
        kh = [k[:, h * GDN_DK:(h + 1) * GDN_DK] for h in range(GDN_HEADS)]
        qh = [q[:, h * GDN_DK:(h + 1) * GDN_DK] for h in range(GDN_HEADS)]
        vh = [v[:, h * GDN_DV:(h + 1) * GDN_DV] for h in range(GDN_HEADS)]
        gq = [_dot_nt(jnp.concatenate([kh[h], qh[h]], axis=0), kh[h]) for h in range(GDN_HEADS)]
        a, t, beta, ecol = {}, {}, {}, {}
        for d, h in insts:
            incl, strict, diag = tri[d]
            col = SM_G + d * GDN_HEADS + h
            decay = jnp.exp(jnp.where(incl, gc[:, col:col + 1] - gct[col:col + 1, :], -jnp.inf))
            beta[d, h] = sm[:, SM_B + col:SM_B + col + 1]
            ecol[d, h] = egc[:, col:col + 1]
            a[d, h] = jnp.where(strict, gq[h][0:CHUNK] * decay, 0.0) * beta[d, h]
            hl = slice((h % 2) * CHUNK, (h % 2 + 1) * CHUNK)
            akd_ref[c, d, h // 2, 0:CHUNK, hl] = (gq[h][CHUNK:2 * CHUNK] * decay).astype(BF16)
            akd_ref[c, d, h // 2, CHUNK:CHUNK + GDN_DK, hl] = (kh[h] * etg[:, col:col + 1]).T.astype(BF16)
            t[d, h] = diag.astype(F32) - jnp.where(merge[d][1], a[d, h], 0.0)
        s = 2
        while s < CHUNK:
            x = {i: _dot(jnp.where(merge[i[0]][s], a[i], 0.0), t[i]) for i in insts}
            t = {i: t[i] - _dot(t[i], x[i]) for i in insts}
            s *= 2
        for d, h in insts:
            rhs = jnp.concatenate([vh[h] * beta[d, h], kh[h] * (beta[d, h] * ecol[d, h])], axis=1)
            sol = _dot(t[d, h], rhs)
            u0_ref[c, d, h] = sol[:, 0:GDN_DV].astype(BF16)
            cq_ref[c, d, h, 0:CHUNK, :] = sol[:, GDN_DV:GDN_DV + GDN_DK].astype(BF16)
            cq_ref[c, d, h, CHUNK:2 * CHUNK, :] = (qh[h] * ecol[d, h]).astype(BF16)


def _gdn_pre(q, k, v, sm, cpb):
    b, lext, _ = q.shape
    ntot = lext // CHUNK
    rows = cpb * CHUNK
    lead = (b, ntot, 2, GDN_HEADS)

    def tok(bi, i):
        return (bi, i, 0)

    def blk(*tail, heads=GDN_HEADS):
        return pl.BlockSpec((None, cpb, 2, heads) + tail, lambda bi, i: (bi, i, 0, 0, 0, 0))

    sds = jax.ShapeDtypeStruct
    pairs = GDN_HEADS // 2
    return pl.pallas_call(
        functools.partial(_gdn_pre_kernel, cpb=cpb),
        grid=(b, ntot // cpb),
        in_specs=[pl.BlockSpec((None, rows, GDN_QK), tok), pl.BlockSpec((None, rows, GDN_QK), tok),
                  pl.BlockSpec((None, rows, GDN_V), tok), pl.BlockSpec((None, rows, LANE), tok)],
        out_specs=[blk(CHUNK, GDN_DV), blk(2 * CHUNK, GDN_DK), blk(CHUNK + GDN_DK, 2 * CHUNK, heads=pairs),
                   pl.BlockSpec((None, cpb, SUBLANE, LANE), lambda bi, i: (bi, i, 0, 0))],
        out_shape=[sds(lead + (CHUNK, GDN_DV), BF16), sds(lead + (2 * CHUNK, GDN_DK), BF16),
                   sds((b, ntot, 2, pairs, CHUNK + GDN_DK, 2 * CHUNK), BF16), sds((b, ntot, SUBLANE, LANE), F32)],
        compiler_params=_params("parallel", "parallel"),
        name="gdn_pre",
    )(q, k, v, sm)


def _scan_kernel(u0f, cqf, akf, ggf, mf, qdf, lgf, u0b, cqb, akb, ggb, mb, qdb, lgb,
                 gof_ref, gob_ref, lof_ref, lob_ref, sg_ref, sl_ref, *, nctx):
    step = pl.program_id(1)

    @pl.when(step == 0)
    def _():
        sg_ref[...] = jnp.zeros(sg_ref.shape, F32)
        sl_ref[...] = jnp.zeros(sl_ref.shape, F32)

    refs = ((u0f, cqf, akf, ggf, mf, qdf, lgf, gof_ref, lof_ref), (u0b, cqb, akb, ggb, mb, qdb, lgb, gob_ref, lob_ref))
    insts = [(d, h) for d in range(2) for h in range(GDN_HEADS)]
    dot = functools.partial(jnp.dot, preferred_element_type=F32)
    cps = u0f.shape[0]
    s = {i: sg_ref[i[0], i[1]] for i in insts}
    sl = {i: sl_ref[i[0], i[1]] for i in insts}
    def pair_rhs(xa, xb):
        z = jnp.zeros(xa.shape, xa.dtype)
        return jnp.concatenate([jnp.concatenate([xa, z], axis=1), jnp.concatenate([z, xb], axis=1)], axis=0)

    pairs = [(d, p) for d in range(2) for p in range(GDN_HEADS // 2)]
    for ci in range(cps):
        cd = (ci, cps - 1 - ci)
        cqs = {(d, h): dot(refs[d][1][cd[d], h], s[d, h].astype(BF16)) for d, h in insts}
        ol, r = {}, {}
        for d, p in pairs:
            olp = dot(refs[d][5][cd[d], p], pair_rhs(sl[d, 2 * p].astype(BF16), sl[d, 2 * p + 1].astype(BF16)))
            ol[d, 2 * p], ol[d, 2 * p + 1] = olp[:, 0:GLA_DV], olp[:, GLA_DV:2 * GLA_DV]
        u = {(d, h): refs[d][0][cd[d], h].astype(F32) - cqs[d, h][0:CHUNK] for d, h in insts}
        for d, p in pairs:
            rp = dot(refs[d][2][cd[d], p], pair_rhs(u[d, 2 * p].astype(BF16), u[d, 2 * p + 1].astype(BF16)))
            r[d, 2 * p], r[d, 2 * p + 1] = rp[:, 0:GDN_DV], rp[:, GDN_DV:2 * GDN_DV]
        for d, h in insts:
            lane = d * GDN_HEADS + h
            s[d, h] = s[d, h] * refs[d][3][cd[d], lane:lane + 1, :] + r[d, h][CHUNK:CHUNK + GDN_DK]
            sl[d, h] = sl[d, h] * refs[d][6][cd[d], :, lane:lane + 1] + refs[d][4][cd[d], h].astype(F32)

        @pl.when(step >= nctx)
        def _():
            for d, h in insts:
                rows = slice(cd[d] * CHUNK, (cd[d] + 1) * CHUNK)
                og = cqs[d, h][CHUNK:2 * CHUNK] + r[d, h][0:CHUNK]
                refs[d][7][rows, h * GDN_DV:(h + 1) * GDN_DV] = og.astype(BF16)
                refs[d][8][rows, h * GLA_DV:(h + 1) * GLA_DV] = ol[d, h].astype(BF16)

    for d, h in insts:
        sg_ref[d, h] = s[d, h]
        sl_ref[d, h] = sl[d, h]


def _scan(u0, cq, akd, ggl, m, qd, lgl, nctx, cps):
    b, nchunks = u0.shape[:2]
    assert nctx % cps == 0 and nchunks % cps == 0
    nlat = nchunks - nctx
    ntot, nctx = nchunks // cps, nctx // cps

    def specs(d):
        def chunk(bi, s):
            return (bi, _ext_chunk(d, s, nctx, ntot))

        def big(*tail, heads=GDN_HEADS):
            return pl.BlockSpec((None, cps, None, heads) + tail, lambda bi, s: chunk(bi, s) + (d, 0, 0, 0))

        def small(*tail):
            return pl.BlockSpec((None, cps) + tail, lambda bi, s: chunk(bi, s) + (0, 0))

        half = GDN_HEADS // 2
        return [big(CHUNK, GDN_DV), big(2 * CHUNK, GDN_DK), big(CHUNK + GDN_DK, 2 * CHUNK, heads=half),
                small(SUBLANE, LANE), big(GLA_DK, GLA_DV), big(CHUNK, 2 * GLA_DK, heads=half), small(GLA_DK, LANE)]

    def out_spec(d, width):
        return pl.BlockSpec((None, cps * CHUNK, width), lambda bi, s: (bi, _lat_chunk(d, s, nctx, ntot), 0))

    sds = jax.ShapeDtypeStruct
    return pl.pallas_call(
        functools.partial(_scan_kernel, nctx=nctx),
        grid=(b, ntot),
        in_specs=specs(0) + specs(1),
        out_specs=[out_spec(0, GDN_V), out_spec(1, GDN_V), out_spec(0, GLA_V), out_spec(1, GLA_V)],
        out_shape=[sds((b, nlat * CHUNK, GDN_V), BF16)] * 2 + [sds((b, nlat * CHUNK, GLA_V), BF16)] * 2,
        scratch_shapes=[pltpu.VMEM((2, GDN_HEADS, GDN_DK, GDN_DV), F32),
                        pltpu.VMEM((2, GLA_HEADS, GLA_DK, GLA_DV), F32)],
        compiler_params=_params("parallel", "arbitrary"),
        name="scan",
    )(u0, cq, akd, ggl, m, qd, lgl, u0, cq, akd, ggl, m, qd, lgl)


def _gla_pre_kernel(q_ref, k_ref, v_ref, la_ref, m_ref, qd_ref, gl_ref, intra_ref, *, cpb):
    tri = [_tri_masks(d) for d in range(2)]
    for c in range(cpb):
        rows = slice(c * CHUNK, (c + 1) * CHUNK)
        q = q_ref[rows, :] * (GLA_DK ** -0.5)
        k = k_ref[rows, :]
        v = v_ref[rows, :]
        la = la_ref[rows, :]
        pref = _cumsum_rows(tri[0][0], la)
        tot = jnp.sum(la, axis=0, keepdims=True)
        lanes = lax.broadcasted_iota(jnp.int32, la.shape, 1)
        bwd_lane = lanes >= GLA_QK
        bc = jnp.where(bwd_lane, tot - pref + la, pref)
        mid = CHUNK // 2
        ref = jnp.where(bwd_lane[0:1, :], bc[CHUNK - 1 - mid:CHUNK - mid, :], bc[mid:mid + 1, :])
        q2 = jnp.concatenate([q, q], axis=1)
        k2 = jnp.concatenate([k, k], axis=1)
        qa = q2 * jnp.exp(bc - ref)
        ka = k2 * jnp.exp(ref - bc)
        qd = (q2 * jnp.exp(bc)).astype(BF16)
        kdt = (k2 * jnp.exp(tot - bc)).T
        glcol = jnp.exp(jnp.sum(la.T, axis=1, keepdims=True))
        out_lanes = lax.broadcasted_iota(jnp.int32, (GLA_DK, LANE), 1)
        gl = jnp.zeros((GLA_DK, LANE), F32)
        for h in range(GLA_HEADS):
            vh = v[:, h * GLA_DV:(h + 1) * GLA_DV]
            intra = None
            for d in range(2):
                cs = slice(d * GLA_QK + h * GLA_DK, d * GLA_QK + (h + 1) * GLA_DK)
                attn = jnp.where(tri[d][0], _dot_nt(qa[:, cs], ka[:, cs]), 0.0)
                im = _dot(jnp.concatenate([attn, kdt[cs, :]], axis=0), vh)
                intra = im[0:CHUNK] if intra is None else intra + im[0:CHUNK]
                m_ref[c, d, h] = im[CHUNK:CHUNK + GLA_DK].astype(BF16)
                gl = jnp.where(out_lanes == d * GLA_HEADS + h, glcol[cs, :], gl)
            intra_ref[rows, h * GLA_DV:(h + 1) * GLA_DV] = intra.astype(BF16)
        for d in range(2):
            for p in range(GLA_HEADS // 2):
                qd_ref[c, d, p] = qd[:, d * GLA_QK + 2 * p * GLA_DK:d * GLA_QK + 2 * (p + 1) * GLA_DK]
        gl_ref[c] = gl


def _gla_pre(rest, loga, cpb):
    b, lext, _ = rest.shape
    ntot = lext // CHUNK
    rows = cpb * CHUNK
    lead = (b, ntot, 2, GLA_HEADS)

    def col(colblk):
        return lambda bi, i: (bi, i, colblk)

    def blk(*tail, heads=GLA_HEADS):
        return pl.BlockSpec((None, cpb, 2, heads) + tail, lambda bi, i: (bi, i, 0, 0, 0, 0))

    sds = jax.ShapeDtypeStruct
    return pl.pallas_call(
        functools.partial(_gla_pre_kernel, cpb=cpb),
        grid=(b, ntot // cpb),
        in_specs=[pl.BlockSpec((None, rows, GLA_QK), col((COL_LQ - COL_GZ) // GLA_QK)),
                  pl.BlockSpec((None, rows, GLA_QK), col((COL_LK - COL_GZ) // GLA_QK)),
                  pl.BlockSpec((None, rows, GLA_V), col((COL_LV - COL_GZ) // GLA_V)),
                  pl.BlockSpec((None, rows, 2 * GLA_QK), col(0))],
        out_specs=[blk(GLA_DK, GLA_DV), blk(CHUNK, 2 * GLA_DK, heads=GLA_HEADS // 2),
                   pl.BlockSpec((None, cpb, GLA_DK, LANE), lambda bi, i: (bi, i, 0, 0)),
                   pl.BlockSpec((None, rows, GLA_V), col(0))],
        out_shape=[sds(lead + (GLA_DK, GLA_DV), BF16), sds((b, ntot, 2, GLA_HEADS // 2, CHUNK, 2 * GLA_DK), BF16),
                   sds((b, ntot, GLA_DK, LANE), F32), sds((b, lext, GLA_V), BF16)],
        compiler_params=_params("parallel", "parallel"),
        name="gla_pre",
    )(rest, rest, rest, loga)


def _head_gate(o, gain, z, heads, dv):
    parts = []
    for h in range(heads):
        sl = slice(h * dv, (h + 1) * dv)
        oh = o[:, sl]
        parts.append(oh * lax.rsqrt(jnp.mean(oh * oh, axis=-1, keepdims=True) + EPS) * gain * _silu(z[:, sl]))
    return jnp.concatenate(parts, axis=-1)


def _ab_mix(refs, w_ref):
    gf_ref, gb_ref, lf_ref, lb_ref, li_ref, gz_ref, lg_ref, gn_ref, ln_ref = refs
    og = gf_ref[...].astype(F32) + gb_ref[...].astype(F32)
    ol = lf_ref[...].astype(F32) + lb_ref[...].astype(F32) + li_ref[0].astype(F32)
    yg = _head_gate(og, gn_ref[...], gz_ref[0], GDN_HEADS, GDN_DV)
    yl = _head_gate(ol, ln_ref[...], lg_ref[0], GLA_HEADS, GLA_DV)
    return (jnp.dot(yg.astype(BF16), w_ref[0:GDN_V, :], preferred_element_type=F32)
            + jnp.dot(yl.astype(BF16), w_ref[GDN_V:GDN_V + GLA_V, :], preferred_element_type=F32))


def _hy_mix(refs, w_ref):
    return jnp.dot(refs[0][...].astype(BF16), w_ref[...], preferred_element_type=F32)


def _tail_kernel(*refs, n_mix, mix_fn, nsplit, final_norm):
    mix_refs = refs[:n_mix]
    wo_ref, h_ref, g1_ref, gn_ref, sh_ref, sc_ref, g2_ref, w1_ref, w3_ref, w2_ref, fn_ref, o_ref = refs[n_mix:]
    h = h_ref[...] + g1_ref[...] * mix_fn(mix_refs, wo_ref)
    a = _modulated(h, gn_ref[...], sh_ref[...], sc_ref[...]).astype(BF16)
    fc = w1_ref.shape[1] // nsplit
    acc = jnp.zeros(h.shape, F32)
    for j in range(nsplit):
        u1 = jnp.dot(a, w1_ref[:, j * fc:(j + 1) * fc], preferred_element_type=F32)
        u3 = jnp.dot(a, w3_ref[:, j * fc:(j + 1) * fc], preferred_element_type=F32)
        acc = acc + jnp.dot((_silu(u1) * u3).astype(BF16), w2_ref[j * fc:(j + 1) * fc, :],
                            preferred_element_type=F32)
    y = h + g2_ref[...] * acc
    if final_norm:
        y = y * lax.rsqrt(jnp.mean(y * y, axis=-1, keepdims=True) + EPS) * fn_ref[...]
    o_ref[...] = y


def _tail(mix_args, mix_specs, mix_fn, w_out, h, g1, gain, shift, scale, g2, w1, w3, w2, final_w, tm, final_norm):
    b, l, d = h.shape
    dff = w1.shape[1]

    def tok(bi, i):
        return (bi, i, 0)

    def vec(bi, i):
        return (bi, 0, 0)

    def const(bi, i):
        return (0, 0)

    return pl.pallas_call(
        functools.partial(_tail_kernel, n_mix=len(mix_args), mix_fn=mix_fn, nsplit=2, final_norm=final_norm),
        grid=(b, l // tm),
        in_specs=list(mix_specs) + [
            _resident(w_out.shape, const), pl.BlockSpec((None, tm, d), tok), pl.BlockSpec((None, 1, d), vec),
            _resident((1, d), const), pl.BlockSpec((None, 1, d), vec), pl.BlockSpec((None, 1, d), vec),
            pl.BlockSpec((None, 1, d), vec), _resident((d, dff), const), _resident((d, dff), const),
            _resident((dff, d), const), _resident((1, d), const)],
        out_specs=pl.BlockSpec((None, tm, d), tok),
        out_shape=jax.ShapeDtypeStruct((b, l, d), F32),
        compiler_params=_params("parallel", "parallel"),
        name="mix_out_ffn",
    )(*mix_args, w_out, h, g1, gain.reshape(1, d), shift, scale, g2, w1, w3, w2, final_w.reshape(1, d))


def _hy_hidden_kernel(fb_ref, w1t_ref, w1c_ref, w1s_ref, b1_ref, w2_ref, b2_ref, w3_ref, b3_ref, fr_ref, o_ref,
                      *, tile, seq_len):
    half = tile // 2
    hid = o_ref.shape[1]
    lane_hi = lax.broadcasted_iota(jnp.int32, (half, 2 * hid), 1) >= hid
    rows = (pl.program_id(0) * tile + lax.broadcasted_iota(jnp.int32, (half, 2 * hid), 0)
            + jnp.where(lane_hi, half, 0))
    pos = jnp.where(rows < seq_len, rows, 2 * seq_len - rows).astype(F32)
    t = pos * (1.0 / (seq_len - 1))
    ang = (pos * (2.0 * math.pi / seq_len)) * fb_ref[...]
    fr = fr_ref[...]
    pre = t * w1t_ref[...] + _dot_hi(jnp.cos(ang), w1c_ref[...]) - _dot_hi(jnp.sin(ang), w1s_ref[...])
    hdn = jnp.sin(fr * (pre + b1_ref[...]))
    hdn = jnp.sin(fr * (_dot_hi(hdn, w2_ref[...]) + b2_ref[...]))
    hdn = jnp.sin(fr * (_dot_hi(hdn, w3_ref[...]) + b3_ref[...]))
    o_ref[0:half, :] = hdn[:, 0:hid]
    o_ref[half:tile, :] = hdn[:, hid:2 * hid]


def _hy_hidden(w1, b1, w2, b2, w3, b3, freq, seq_len, tile):
    hid = w1.shape[1]
    assert 2 * hid == LANE and HY_BANDS <= hid

    def twice(v):
        return jnp.tile(v.reshape(1, hid), (1, 2))

    def block_diag(w):
        k = w.shape[0]
        return jnp.zeros((2 * hid, 2 * hid), F32).at[:k, :hid].set(w).at[hid:hid + k, hid:].set(w)

    bands = jnp.linspace(1e-4, HY_BANDS - 1, HY_BANDS, dtype=F32)
    fb = jnp.zeros((1, 2 * hid), F32).at[0, :HY_BANDS].set(bands).at[0, hid:hid + HY_BANDS].set(bands)
    args = (fb, twice(w1[0]), block_diag(w1[1:1 + HY_BANDS]), block_diag(w1[1 + HY_BANDS:1 + 2 * HY_BANDS]),
            twice(b1), block_diag(w2), twice(b2), block_diag(w3), twice(b3), twice(freq))
    return pl.pallas_call(
        functools.partial(_hy_hidden_kernel, tile=tile, seq_len=seq_len),
        grid=(2 * seq_len // tile,),
        in_specs=[pl.BlockSpec(a.shape, lambda i: (0, 0)) for a in args],
        out_specs=pl.BlockSpec((tile, hid), lambda i: (i, 0)),
        out_shape=jax.ShapeDtypeStruct((2 * seq_len, hid), F32),
        compiler_params=_params("parallel"),
        name="hy_hidden",
    )(*args)


def _fft_matrices(r):
    n = r * r
    idx = jnp.arange(r, dtype=jnp.int32)

    def cis(m, period):
        ang = (2.0 * math.pi / period) * m.astype(F32)
        return jnp.cos(ang), -jnp.sin(ang)

    fr, fi = cis((idx[:, None] * idx[None, :]) % r, r)
    half = r // 2
    f1_data = jnp.block([[fr[:, :half], -fi[:, :half]], [fi[:, :half], fr[:, :half]]])
    f1_real = jnp.concatenate([fr, fi], axis=0)
    f3 = jnp.block([[fr[:half, :], fi[:half, :]], [-fi[:half, :], fr[:half, :]]]) * (1.0 / n)
    tr, ti = cis((idx[:, None] * idx[None, :]) % n, n)
    gr = tr[:, None, :] * fr[None, :, :] - ti[:, None, :] * fi[None, :, :]
    gi = tr[:, None, :] * fi[None, :, :] + ti[:, None, :] * fr[None, :, :]
    g = jnp.concatenate([jnp.concatenate([gr, -gi], axis=2), jnp.concatenate([gi, gr], axis=2)], axis=1)
    gh = jnp.swapaxes(g, 1, 2)
    return f1_data.astype(BF16), f1_real.astype(BF16), f3.astype(BF16), g.astype(BF16), gh.astype(BF16)


FFT_BG = 16
FFT_DT = 256


def _slab(ref, j):
    return jnp.concatenate([ref[i, :, j, :] for i in range(ref.shape[0])], axis=0)


def _store_interleaved(o_ref, vals):
    x = pltpu.einshape("jrd->rjd", jnp.stack([v.astype(o_ref.dtype) for v in vals], axis=0))
    o_ref[...] = x.reshape(o_ref.shape)


def _time_block(arr, sel, n):
    lead = arr.ndim - 4
    return pl.BlockSpec((None,) * lead + (n, arr.shape[-3], FFT_BG, FFT_DT),
                        lambda p, g, dd: tuple(sel) + (p, 0, g, dd))


def _fft_stage1_kernel(f_ref, x_ref, o_ref):
    _store_interleaved(o_ref, [jnp.dot(f_ref[...], _slab(x_ref, j).astype(BF16), preferred_element_type=F32)
                               for j in range(x_ref.shape[2])])


def _fft_stage1(f1, x5, sel, n):
    r2, _ = f1.shape
    r = r2 // 2
    items, _, rb, d = x5.shape[-4:]
    groups = items // n
    return pl.pallas_call(
        _fft_stage1_kernel,
        grid=(groups, rb // FFT_BG, d // FFT_DT),
        in_specs=[_resident(f1.shape, lambda p, g, dd: (0, 0)), _time_block(x5, sel, n)],
        out_specs=pl.BlockSpec((None, 2, r, FFT_BG, FFT_DT), lambda p, g, dd: (p, 0, 0, g, dd)),
        out_shape=jax.ShapeDtypeStruct((groups, 2, r, rb, d), BF16),
        compiler_params=_params("parallel", "parallel", "parallel"),
        name="fft_stage1",
    )(f1, x5)


def _fft_filter_stage1_kernel(f_ref, hdn_ref, fo_ref, dl_ref, o_ref, *, seq_len):
    r, bg, _ = hdn_ref.shape
    orders = fo_ref.shape[1]
    half = r // 2
    a_idx = lax.broadcasted_iota(jnp.int32, (r, 1), 0)
    ps = [[] for _ in range(orders)]
    for j in range(bg):
        rows = a_idx * r + (pl.program_id(0) * bg + j)
        pos = jnp.where(rows < seq_len, rows, 2 * seq_len - rows).astype(F32)
        t = pos * (1.0 / (seq_len - 1))
        window = jnp.where(rows != seq_len, jnp.exp(-t * dl_ref[...]) + HY_SHIFT, 0.0)
        hj = hdn_ref[:, j, :]
        for n in range(orders):
            taps = jnp.concatenate([_dot(hj[:half], fo_ref[0, n]), _dot(hj[half:], fo_ref[1, n])], axis=0) * window
            ps[n].append(jnp.dot(f_ref[...], taps.astype(BF16), preferred_element_type=F32))
    for n in range(orders):
        _store_interleaved(o_ref.at[n], ps[n])


def _fft_filter_stage1(f1, hdn3, fo, deltas, seq_len):
    r2, r = f1.shape
    hid = hdn3.shape[-1]
    _, orders, _, d = fo.shape
    return pl.pallas_call(
        functools.partial(_fft_filter_stage1_kernel, seq_len=seq_len),
        grid=(r // FFT_BG, d // FFT_DT),
        in_specs=[_resident(f1.shape, lambda g, dd: (0, 0)),
                  pl.BlockSpec((r, FFT_BG, hid), lambda g, dd: (0, g, 0)),
                  pl.BlockSpec((2, orders, hid, FFT_DT), lambda g, dd: (0, 0, 0, dd)),
                  pl.BlockSpec((1, FFT_DT), lambda g, dd: (0, dd))],
        out_specs=pl.BlockSpec((orders, 2, r, FFT_BG, FFT_DT), lambda g, dd: (0, 0, 0, g, dd)),
        out_shape=jax.ShapeDtypeStruct((orders, 2, r, r, d), BF16),
        compiler_params=_params("parallel", "parallel"),
        name="fft_filter_stage1",
    )(f1, hdn3, fo, deltas)


def _fft_stage2_kernel(g_ref, gh_ref, k_ref, p_ref, o_ref):
    r = p_ref.shape[2]
    kq = k_ref[...]
    h = jnp.dot(g_ref[...], kq.reshape(2 * r, kq.shape[2]), preferred_element_type=F32)
    hr, hi = h[:r], h[r:]
    for n in range(p_ref.shape[0]):
        p = p_ref[n]
        x = jnp.dot(g_ref[...], p.reshape(2 * r, p.shape[2]), preferred_element_type=F32)
        xr, xi = x[:r], x[r:]
        y = jnp.concatenate([xr * hr - xi * hi, xr * hi + xi * hr], axis=0)
        q = jnp.dot(gh_ref[...], y.astype(BF16), preferred_element_type=F32)
        o_ref[n] = q.reshape(o_ref.shape[1:]).astype(o_ref.dtype)


def _fft_stage2(g, gh, pk, order, p5):
    pairs, _, r, _, d = p5.shape
    return pl.pallas_call(
        _fft_stage2_kernel,
        grid=(r,),
        in_specs=[pl.BlockSpec((None, 2 * r, 2 * r), lambda kk: (kk, 0, 0)),
                  pl.BlockSpec((None, 2 * r, 2 * r), lambda kk: (kk, 0, 0)),
                  pl.BlockSpec((None, 2, None, r, d), lambda kk: (order, 0, kk, 0, 0)),
                  pl.BlockSpec((pairs, 2, None, r, d), lambda kk: (0, 0, kk, 0, 0))],
        out_specs=pl.BlockSpec((pairs, 2, None, r, d), lambda kk: (0, 0, kk, 0, 0)),
        out_shape=jax.ShapeDtypeStruct(p5.shape, BF16),
        compiler_params=_params("parallel"),
        name="fft_stage2",
    )(g, gh, pk, p5)


def _rows_by_b(ref):
    bg = ref.shape[2]
    if ref.dtype == F32:
        return [_slab(ref, j) for j in range(bg)]
    x = ref[...]
    x = pltpu.einshape("rjd->jrd", x.reshape(x.shape[0] * x.shape[1], bg, x.shape[3]))
    return [x[j] for j in range(bg)]


def _fft_stage3_kernel(f3_ref, f1_ref, q_ref, z_ref, gate_ref, skip_ref, zo_ref, po_ref=None):
    q, z, gate = _rows_by_b(q_ref), _rows_by_b(z_ref), _rows_by_b(gate_ref)
    skip = skip_ref[...]
    zs, ps = [], []
    for j in range(len(q)):
        y = jnp.dot(f3_ref[...], q[j], preferred_element_type=F32)
        zn = gate[j] * (y + z[j] * skip)
        zs.append(zn)
        if po_ref is not None:
            ps.append(jnp.dot(f1_ref[...], zn.astype(BF16), preferred_element_type=F32))
    _store_interleaved(zo_ref, zs)
    if po_ref is not None:
        _store_interleaved(po_ref, ps)


def _fft_stage3(f3, f1, q5, z5, z_sel, g5, g_sel, skip, chain):
    pairs, _, r, rb, d = q5.shape
    a = z5.shape[-3]
    sds = jax.ShapeDtypeStruct
    out_specs = [pl.BlockSpec((2, a, FFT_BG, FFT_DT), lambda p, g, dd: (p, 0, g, dd))]
    out_shape = [sds((2 * pairs, a, rb, d), BF16)]
    if chain:
        out_specs.append(pl.BlockSpec((None, 2, r, FFT_BG, FFT_DT), lambda p, g, dd: (p, 0, 0, g, dd)))
        out_shape.append(sds(q5.shape, BF16))
    return pl.pallas_call(
        _fft_stage3_kernel,
        grid=(pairs, rb // FFT_BG, d // FFT_DT),
        in_specs=[_resident(f3.shape, lambda p, g, dd: (0, 0)), _resident(f1.shape, lambda p, g, dd: (0, 0)),
                  pl.BlockSpec((None, 2, r, FFT_BG, FFT_DT), lambda p, g, dd: (p, 0, 0, g, dd)),
                  _time_block(z5, z_sel, 2), _time_block(g5, g_sel, 2),
                  pl.BlockSpec((1, FFT_DT), lambda p, g, dd: (0, dd))],
        out_specs=out_specs,
        out_shape=out_shape,
        compiler_params=_params("parallel", "parallel", "parallel"),
        name="fft_stage3",
    )(f3, f1, q5, z5, g5, skip)


TAIL_ROWS = 512
HY_FRONT_ROWS = 512
AB_FRONT_ROWS = 384


def _regroup_ab_weights(w_in):
    d = w_in.shape[0]
    sizes = (GDN_QK, GDN_QK, GDN_V, GDN_V, 2 * GDN_HEADS, 2 * GDN_HEADS, GLA_QK, GLA_QK, GLA_V, GLA_V,
             2 * GLA_GATE_RANK)
    offs = [0]
    for s in sizes:
        offs.append(offs[-1] + s)
    gq, gk, gv, gz, ga, gb, lq, lk, lv, lg, llr = (w_in[:, offs[i]:offs[i + 1]] for i in range(len(sizes)))
    pad = jnp.zeros((d, LANE - (ga.shape[1] + gb.shape[1] + llr.shape[1])), w_in.dtype)
    return jnp.concatenate([gq, gk, gv, gz, lq, lk, lv, lg, ga, gb, llr, pad], axis=1)


def _lane_row(vals, offset):
    flat = vals.reshape(-1).astype(F32)
    return jnp.zeros((1, LANE), F32).at[0, offset:offset + flat.shape[0]].set(flat)


def _even_layer(h, ctx, mods, norm1, norm2, w_in, conv_w, a_log, dt_bias, gdn_norm, gate_w, gate_b, gla_norm,
                w_out, w1, w3, w2, final_w, final_norm):
    b, l, d = h.shape
    lctx = ctx.shape[1]
    tm = TAIL_ROWS
    sh1, sc1, g1, sh2, sc2, g2 = mods["lat"]
    csh1, csc1 = mods["ctx"][0], mods["ctx"][1]
    xe = jnp.concatenate([ctx, h], axis=1)
    shift = jnp.stack([jnp.broadcast_to(csh1, (b, d)), sh1], axis=1)[:, :, None, :]
    scale = jnp.stack([jnp.broadcast_to(csc1, (b, d)), sc1], axis=1)[:, :, None, :]
    gw = jnp.zeros((LANE, 2 * GLA_QK), F32)
    for dd in range(2):
        gw = gw.at[SM_LR + dd * GLA_GATE_RANK:SM_LR + (dd + 1) * GLA_GATE_RANK,
                   dd * GLA_QK:(dd + 1) * GLA_QK].set(gate_w[dd])
    q, k, v, sm, loga, rest = _ab_front(xe, norm1, shift, scale, _regroup_ab_weights(w_in).astype(BF16), conv_w,
                                        _lane_row(a_log, SM_G), _lane_row(dt_bias, SM_G), gw,
                                        gate_b.reshape(1, 2 * GLA_QK), AB_FRONT_ROWS, lctx)
    nctx = lctx // CHUNK
    u0, cq, akd, ggl = _gdn_pre(q, k, v, sm, 2)
    m, qd, lgl, ol_intra = _gla_pre(rest, loga, 2)
    og_f, og_b, ol_f, ol_b = _scan(u0, cq, akd, ggl, m, qd, lgl, nctx, 4)
    def lat(width):
        return pl.BlockSpec((None, tm, width), lambda bi, i: (bi, i, 0))

    def ext(width, colblk):
        return pl.BlockSpec((pl.Element(1), pl.Element(tm), pl.Element(width)),
                            lambda bi, i: (bi, SUBLANE * (lctx // SUBLANE + i * (tm // SUBLANE)), colblk * width))

    mix_args = (og_f, og_b, ol_f, ol_b, ol_intra, rest, rest, gdn_norm.reshape(1, GDN_DV), gla_norm.reshape(1, GLA_DV))
    mix_specs = [lat(GDN_V), lat(GDN_V), lat(GLA_V), lat(GLA_V), ext(GLA_V, 0), ext(GDN_V, 0),
                 ext(GLA_V, (COL_LG - COL_GZ) // GLA_V), _resident((1, GDN_DV), lambda bi, i: (0, 0)),
                 _resident((1, GLA_DV), lambda bi, i: (0, 0))]
    return _tail(mix_args, mix_specs, _ab_mix, w_out.astype(BF16), h, g1[:, None, :], norm2, sh2[:, None, :],
                 sc2[:, None, :], g2[:, None, :], w1.astype(BF16), w3.astype(BF16), w2.astype(BF16), final_w, tm,
                 final_norm)


def _hyena_layer(h, mods, norm1, norm2, w_in, conv_w, pw1, pb1, pw2, pb2, pw3, pb3, freq, filt_out, skip, w_out,
                 w1, w3, w2, final_w, final_norm):
    b, l, d = h.shape
    tm = TAIL_ROWS
    sh1, sc1, g1, sh2, sc2, g2 = mods["lat"]
    u3 = _hy_front(h, norm1, sh1[:, None, :], sc1[:, None, :], w_in.astype(BF16), conv_w, HY_FRONT_ROWS)

    n2 = 2 * l
    r = math.isqrt(n2)
    assert r * r == n2 and b % 2 == 0
    orders = skip.shape[0]
    hdn = _hy_hidden(pw1, pb1, pw2, pb2, pw3, pb3, freq, l, 512)
    fo = jnp.transpose(filt_out.reshape(filt_out.shape[0], 2, orders, d), (1, 2, 0, 3))
    deltas = jnp.abs(jnp.linspace(HY_MIN_DECAY, HY_MAX_DECAY, d, dtype=F32))[None, :]

    f1_data, f1_real, f3, g, gh = _fft_matrices(r)
    pk = _fft_filter_stage1(f1_real, hdn.reshape(r, r, hdn.shape[1]), fo, deltas, l)

    u5 = u3.reshape(3, b, r // 2, r, d)
    z_arr, z_sel = u5, (0,)
    p = _fft_stage1(f1_data, z_arr, z_sel, 2)
    for n in range(orders):
        q = _fft_stage2(g, gh, pk, n, p)
        chain = n + 1 < orders
        res = _fft_stage3(f3, f1_data, q, z_arr, z_sel, u5, (n + 1,), skip[n][None, :], chain)
        z_arr, z_sel = res[0], ()
        if chain:
            p = res[1]
    mix_specs = [pl.BlockSpec((None, tm, d), lambda bi, i: (bi, i, 0))]
    return _tail((z_arr.reshape(b, l, d),), mix_specs, _hy_mix, w_out.astype(BF16), h, g1[:, None, :], norm2,
                 sh2[:, None, :], sc2[:, None, :], g2[:, None, :], w1.astype(BF16), w3.astype(BF16), w2.astype(BF16),
                 final_w, tm, final_norm)


def kernel(x, c, ctx, c_ctx, mod_w, mod_b, norm1_w, norm2_w, ab_w_in, ab_conv_w, gdn_a_log, gdn_dt_bias, gdn_norm_w, gla_gate_w, gla_gate_b, gla_norm_w, ab_w_out, hy_w_in, hy_conv_w, hy_pos_w1, hy_pos_b1, hy_pos_w2, hy_pos_b2, hy_pos_w3, hy_pos_b3, hy_freq, hy_filt_out, hy_skip, hy_w_out, ffn_w1, ffn_w3, ffn_w2, final_norm_w):
    b, l, d = x.shape
    depth = mod_w.shape[0]
    assert b + 1 <= SUBLANE
    cvec = jnp.concatenate([c, c_ctx[None, :], jnp.zeros((SUBLANE - b - 1, d), F32)], axis=0)
    mod = _modulation(cvec, mod_w, mod_b)
    h = x
    for i in range(depth):
        mods = {"lat": [mod[i, :b, j * d:(j + 1) * d] for j in range(6)],
                "ctx": [mod[i, b, j * d:(j + 1) * d] for j in range(6)]}
        last = i == depth - 1
        if i % 2 == 0:
            assert not any(j % 2 == 0 for j in range(i + 1, depth))
            e = i // 2
            h = _even_layer(h, ctx, mods, norm1_w[i], norm2_w[i], ab_w_in[e], ab_conv_w[e], gdn_a_log[e],
                            gdn_dt_bias[e], gdn_norm_w[e], gla_gate_w[e], gla_gate_b[e], gla_norm_w[e], ab_w_out[e],
                            ffn_w1[i], ffn_w3[i], ffn_w2[i], final_norm_w, last)
        else:
            o = i // 2
            h = _hyena_layer(h, mods, norm1_w[i], norm2_w[i], hy_w_in[o], hy_conv_w[o], hy_pos_w1[o], hy_pos_b1[o],
                             hy_pos_w2[o], hy_pos_b2[o], hy_pos_w3[o], hy_pos_b3[o], hy_freq[o], hy_filt_out[o],
                             hy_skip[o], hy_w_out[o], ffn_w1[i], ffn_w3[i], ffn_w2[i], final_norm_w, last)
    return h
```

```python
import functools
import math

import jax
import jax.numpy as jnp
from jax import lax
from jax.experimental import pallas as pl
from jax.experimental.pallas import tpu as pltpu

F32 = jnp.float32
BF16 = jnp.bfloat16
HI = lax.Precision.HIGHEST

EPS = 1e-6
CHUNK = 64
GDN_HEADS, GDN_DK, GDN_DV = 4, 128, 128
GLA_HEADS, GLA_DK, GLA_DV = 4, 64, 128
GLA_GATE_RANK = 16
GLA_GATE_TAU = 16.0
GDN_QK = GDN_HEADS * GDN_DK
GDN_V = GDN_HEADS * GDN_DV
GLA_QK = GLA_HEADS * GLA_DK
GLA_V = GLA_HEADS * GLA_DV
HY_EMB = 33
HY_BANDS = (HY_EMB - 1) // 2
HY_MIN_DECAY = math.log(1e-2) / 1.5
HY_MAX_DECAY = math.log(1e-2) / 0.3
HY_SHIFT = 0.05

LANE = 128
SUBLANE = 8
VMEM_LIMIT_BYTES = 56 * 1024 * 1024

COL_GQ, COL_GK, COL_GV, COL_GZ = 0, 512, 1024, 1536
COL_LQ, COL_LK, COL_LV, COL_LG = 2048, 2304, 2560, 3072
COL_SMALL = 3584
SM_G, SM_B, SM_LR = 0, 8, 16


def _params(*sem):
    return pltpu.CompilerParams(dimension_semantics=sem, vmem_limit_bytes=VMEM_LIMIT_BYTES)


def _resident(shape, index_map):
    return pl.BlockSpec(shape, index_map, pipeline_mode=pl.Buffered(1))


def _dot(a, b):
    return jnp.dot(a.astype(BF16), b.astype(BF16), preferred_element_type=F32)


def _dot_nt(a, b):
    return lax.dot_general(a.astype(BF16), b.astype(BF16), (((1,), (1,)), ((), ())),
                           preferred_element_type=F32)


def _dot_hi(a, b):
    return jnp.dot(a, b, preferred_element_type=F32, precision=HI)


def _sigmoid(x):
    return 1.0 / (1.0 + jnp.exp(-x))


def _silu(x):
    return x * _sigmoid(x)


def _softplus(x):
    return jnp.maximum(x, 0.0) + jnp.log(1.0 + jnp.exp(-jnp.abs(x)))


def _mod_kernel(c_ref, w_ref, b_ref, o_ref):
    o_ref[...] = _dot_hi(_silu(c_ref[...]), w_ref[...]) + b_ref[...]


def _modulation(cvec, mod_w, mod_b):
    depth, d, n = mod_w.shape
    tn = n // 4
    return pl.pallas_call(
        _mod_kernel,
        grid=(depth, n // tn),
        in_specs=[pl.BlockSpec((SUBLANE, d), lambda i, j: (0, 0)),
                  pl.BlockSpec((None, d, tn), lambda i, j: (i, 0, j)),
                  pl.BlockSpec((None, 1, tn), lambda i, j: (i, 0, j))],
        out_specs=pl.BlockSpec((None, SUBLANE, tn), lambda i, j: (i, 0, j)),
        out_shape=jax.ShapeDtypeStruct((depth, SUBLANE, n), F32),
        compiler_params=_params("parallel", "parallel"),
        name="modulation",
    )(cvec, mod_w, mod_b.reshape(depth, 1, n))


CONV_HALO = SUBLANE


def _modulated(x, gain, shift, scale):
    ms = jnp.mean(x * x, axis=-1, keepdims=True)
    return x * lax.rsqrt(ms + EPS) * gain * (1.0 + scale) + shift


def _window_start(i, tm, win, seq_len):
    return SUBLANE * jnp.clip(i * (tm // SUBLANE) - CONV_HALO // SUBLANE, 0, (seq_len - win) // SUBLANE)


def _window_spec(tm, win, seq_len, d):
    return pl.BlockSpec((pl.Element(1), pl.Element(win), pl.Element(d)),
                        lambda bi, i: (bi, _window_start(i, tm, win, seq_len), 0))


def _conv3_rows(u, w, gidx=None, starts=(), ends=()):
    rows = u.shape[0]
    prev = pltpu.roll(u, 1, 0)
    nxt = pltpu.roll(u, rows - 1, 0)
    if starts:
        prev = jnp.where(functools.reduce(jnp.logical_or, [gidx == r for r in starts]), 0.0, prev)
    if ends:
        nxt = jnp.where(functools.reduce(jnp.logical_or, [gidx == r for r in ends]), 0.0, nxt)
    return prev * w[0:1, :] + u * w[1:2, :] + nxt * w[2:3, :]


def _hy_front_kernel(x_ref, g_ref, sh_ref, sc_ref, w_ref, cw_ref, o_ref, scr, *, tm, seq_len):
    i = pl.program_id(1)
    win = x_ref.shape[1]
    ws = _window_start(i, tm, win, seq_len)
    off = pl.multiple_of(i * tm - ws, SUBLANE)
    a = _modulated(x_ref[0], g_ref[...], sh_ref[...], sc_ref[...])
    u = jnp.dot(a.astype(BF16), w_ref[...], preferred_element_type=F32)
    cw = cw_ref[...]
    scr[...] = _conv3_rows(u, cw)

    @pl.when(i == 0)
    def _():
        scr[0:1, :] = scr[0:1, :] - u[win - 1:win, :] * cw[0:1, :]

    @pl.when(i == pl.num_programs(1) - 1)
    def _():
        scr[win - 1:win, :] = scr[win - 1:win, :] - u[0:1, :] * cw[2:3, :]

    d = o_ref.shape[-1]
    for g in range(o_ref.shape[0]):
        o_ref[g] = scr[pl.ds(off, tm), g * d:(g + 1) * d]


def _hy_front(x, gain, shift, scale, w, conv_w, tm):
    b, l, d = x.shape
    n = w.shape[1]
    groups = n // d
    win = tm + 2 * CONV_HALO

    def vec(bi, i):
        return (bi, 0, 0)

    return pl.pallas_call(
        functools.partial(_hy_front_kernel, tm=tm, seq_len=l),
        grid=(b, l // tm),
        in_specs=[_window_spec(tm, win, l, d), _resident((1, d), lambda bi, i: (0, 0)),
                  pl.BlockSpec((None, 1, d), vec), pl.BlockSpec((None, 1, d), vec),
                  _resident((d, n), lambda bi, i: (0, 0)), _resident((3, n), lambda bi, i: (0, 0))],
        out_specs=pl.BlockSpec((groups, None, tm, d), lambda bi, i: (0, bi, i, 0)),
        out_shape=jax.ShapeDtypeStruct((groups, b, l, d), F32),
        scratch_shapes=[pltpu.VMEM((win, n), F32)],
        compiler_params=_params("parallel", "parallel"),
        name="hy_front",
    )(x, gain.reshape(1, d), shift, scale, w, conv_w)


def _ab_front_kernel(x_ref, g_ref, sh_ref, sc_ref, w_ref, cw_ref, alog_ref, dtb_ref, gw_ref, gb_ref,
                     q_ref, k_ref, v_ref, so_ref, la_ref, rest_ref, scr, *, tm, lctx, lext):
    i = pl.program_id(1)
    win = x_ref.shape[1]
    ws = _window_start(i, tm, win, lext)
    off = pl.multiple_of(i * tm - ws, SUBLANE)
    gidx = ws + lax.broadcasted_iota(jnp.int32, (win, 1), 0)
    is_ctx = gidx < lctx
    shift = jnp.where(is_ctx, sh_ref[0], sh_ref[1])
    scale = jnp.where(is_ctx, sc_ref[0], sc_ref[1])
    a = _modulated(x_ref[0], g_ref[...], shift, scale)
    u = jnp.dot(a.astype(BF16), w_ref[...], preferred_element_type=F32)
    ncv = cw_ref.shape[1]
    scr[:, 0:ncv] = _conv3_rows(u[:, 0:ncv], cw_ref[...], gidx, (0, lctx), (lctx - 1, lext - 1))
    scr[:, ncv:] = u[:, ncv:]
    rows = pl.ds(off, tm)
    for h in range(GDN_HEADS):
        sl = slice(h * GDN_DK, (h + 1) * GDN_DK)
        qh = _silu(scr[rows, COL_GQ + h * GDN_DK:COL_GQ + (h + 1) * GDN_DK])
        kh = _silu(scr[rows, COL_GK + h * GDN_DK:COL_GK + (h + 1) * GDN_DK])
        q_ref[:, sl] = qh * lax.rsqrt(jnp.sum(qh * qh, axis=-1, keepdims=True) + EPS) * (GDN_DK ** -0.5)
        k_ref[:, sl] = kh * lax.rsqrt(jnp.sum(kh * kh, axis=-1, keepdims=True) + EPS)
    v_ref[...] = _silu(scr[rows, COL_GV:COL_GV + GDN_V])
    rest_ref[...] = scr[rows, COL_GZ:COL_SMALL]
    sm = scr[rows, COL_SMALL:COL_SMALL + LANE]
    g = -jnp.exp(alog_ref[...]) * _softplus(sm + dtb_ref[...])
    beta = _sigmoid(sm)
    lanes = lax.broadcasted_iota(jnp.int32, sm.shape, 1)
    so_ref[...] = jnp.where(lanes < SM_B, g, beta)
    gate = _dot(sm, gw_ref[...]) + gb_ref[...]
    la_ref[...] = (jnp.minimum(gate, 0.0) - jnp.log(1.0 + jnp.exp(-jnp.abs(gate)))) * (1.0 / GLA_GATE_TAU)


def _ab_front(xe, gain, shift, scale, w, conv_w, alog_row, dtb_row, gate_w, gate_b, tm, lctx):
    b, lext, d = xe.shape
    n = w.shape[1]
    win = tm + 2 * CONV_HALO
    nrest = COL_SMALL - COL_GZ

    def const(bi, i):
        return (0, 0)

    def tok(width):
        return pl.BlockSpec((None, tm, width), lambda bi, i: (bi, i, 0))

    sds = jax.ShapeDtypeStruct
    return pl.pallas_call(
        functools.partial(_ab_front_kernel, tm=tm, lctx=lctx, lext=lext),
        grid=(b, lext // tm),
        in_specs=[_window_spec(tm, win, lext, d), _resident((1, d), const),
                  pl.BlockSpec((None, 2, 1, d), lambda bi, i: (bi, 0, 0, 0)),
                  pl.BlockSpec((None, 2, 1, d), lambda bi, i: (bi, 0, 0, 0)),
                  _resident((d, n), const), _resident(conv_w.shape, const),
                  _resident((1, LANE), const), _resident((1, LANE), const),
                  _resident((LANE, 2 * GLA_QK), const), _resident((1, 2 * GLA_QK), const)],
        out_specs=[tok(GDN_QK), tok(GDN_QK), tok(GDN_V), tok(LANE), tok(2 * GLA_QK), tok(nrest)],
        out_shape=[sds((b, lext, GDN_QK), F32), sds((b, lext, GDN_QK), F32), sds((b, lext, GDN_V), F32),
                   sds((b, lext, LANE), F32), sds((b, lext, 2 * GLA_QK), F32), sds((b, lext, nrest), F32)],
        scratch_shapes=[pltpu.VMEM((win, n), F32)],
        compiler_params=_params("parallel", "parallel"),
        name="ab_front",
    )(xe, gain.reshape(1, d), shift, scale, w, conv_w, alog_row, dtb_row, gate_w, gate_b)


def _ext_chunk(d, s, nctx, ntot):
    if d == 0:
        return s
    return jnp.where(s < nctx, nctx - 1 - s, ntot + nctx - 1 - s)


def _lat_chunk(d, s, nctx, ntot):
    nlat = ntot - nctx
    if d == 0:
        return jnp.maximum(s - nctx, 0)
    return jnp.minimum(ntot - 1 - s, nlat - 1)


def _tri_masks(d):
    ii = lax.broadcasted_iota(jnp.int32, (CHUNK, CHUNK), 0)
    jj = lax.broadcasted_iota(jnp.int32, (CHUNK, CHUNK), 1)
    if d == 0:
        return ii >= jj, ii > jj, ii == jj
    return ii <= jj, ii < jj, ii == jj


def _cumsum_rows(incl, x):
    m = incl.astype(BF16)
    x1 = x.astype(BF16)
    r1 = x - x1.astype(F32)
    x2 = r1.astype(BF16)
    x3 = (r1 - x2.astype(F32)).astype(BF16)
    dot = functools.partial(jnp.dot, preferred_element_type=F32)
    return dot(m, x1) + dot(m, x2) + dot(m, x3)


def _merge_masks(d):
    ii = lax.broadcasted_iota(jnp.int32, (CHUNK, CHUNK), 0)
    jj = lax.broadcasted_iota(jnp.int32, (CHUNK, CHUNK), 1)
    lo, hi = (jj, ii) if d == 0 else (ii, jj)
    masks = {}
    s = 1
    while s < CHUNK:
        masks[s] = jnp.logical_and(ii // (2 * s) == jj // (2 * s),
                                   jnp.logical_and((hi // s) % 2 == 1, (lo // s) % 2 == 0))
        s *= 2
    return masks


def _gdn_pre_kernel(q_ref, k_ref, v_ref, sm_ref, u0_ref, cq_ref, akd_ref, gl_ref, *, cpb):
    tri = [_tri_masks(d) for d in range(2)]
    merge = [_merge_masks(d) for d in range(2)]
    insts = [(d, h) for d in range(2) for h in range(GDN_HEADS)]
    for c in range(cpb):
        rows = slice(c * CHUNK, (c + 1) * CHUNK)
        q = q_ref[rows, :]
        k = k_ref[rows, :]
        v = v_ref[rows, :]
        sm = sm_ref[rows, :]
        pref = _cumsum_rows(tri[0][0], sm)
        tot = jnp.sum(sm, axis=0, keepdims=True)
        lanes = lax.broadcasted_iota(jnp.int32, sm.shape, 1)
        bwd_lane = jnp.logical_and(lanes >= SM_G + GDN_HEADS, lanes < SM_G + 2 * GDN_HEADS)
        gc = jnp.where(bwd_lane, tot - pref + sm, pref)
        gct = gc.T
        totc = jnp.sum(sm.T, axis=1, keepdims=True)
        gl_ref[c] = jnp.exp(jnp.broadcast_to(totc[0:SUBLANE, :], (SUBLANE, LANE)))
        egc = jnp.exp(gc)
        etg = jnp.exp(tot - gc)
        kh = [k[:, h * GDN_DK:(h + 1) * GDN_DK] for h in range(GDN_HEADS)]
        qh = [q[:, h * GDN_DK:(h + 1) * GDN_DK] for h in range(GDN_HEADS)]
        vh = [v[:, h * GDN_DV:(h + 1) * GDN_DV] for h in range(GDN_HEADS)]
        gq = [_dot_nt(jnp.concatenate([kh[h], qh[h]], axis=0), kh[h]) for h in range(GDN_HEADS)]
        a, t, beta, ecol = {}, {}, {}, {}
        for d, h in insts:
            incl, strict, diag = tri[d]
            col = SM_G + d * GDN_HEADS + h
            decay = jnp.exp(jnp.where(incl, gc[:, col:col + 1] - gct[col:col + 1, :], -jnp.inf))
            beta[d, h] = sm[:, SM_B + col:SM_B + col + 1]
            ecol[d, h] = egc[:, col:col + 1]
            a[d, h] = jnp.where(strict, gq[h][0:CHUNK] * decay, 0.0) * beta[d, h]
            hl = slice((h % 2) * CHUNK, (h % 2 + 1) * CHUNK)
            akd_ref[c, d, h // 2, 0:CHUNK, hl] = (gq[h][CHUNK:2 * CHUNK] * decay).astype(BF16)
            akd_ref[c, d, h // 2, CHUNK:CHUNK + GDN_DK, hl] = (kh[h] * etg[:, col:col + 1]).T.astype(BF16)
            t[d, h] = diag.astype(F32) - jnp.where(merge[d][1], a[d, h], 0.0)
        s = 2
        while s < CHUNK:
            x = {i: _dot(jnp.where(merge[i[0]][s], a[i], 0.0), t[i]) for i in insts}
            t = {i: t[i] - _dot(t[i], x[i]) for i in insts}
            s *= 2
        for d, h in insts:
            rhs = jnp.concatenate([vh[h] * beta[d, h], kh[h] * (beta[d, h] * ecol[d, h])], axis=1)
            sol = _dot(t[d, h], rhs)
            u0_ref[c, d, h] = sol[:, 0:GDN_DV].astype(BF16)
            cq_ref[c, d, h, 0:CHUNK, :] = sol[:, GDN_DV:GDN_DV + GDN_DK].astype(BF16)
            cq_ref[c, d, h, CHUNK:2 * CHUNK, :] = (qh[h] * ecol[d, h]).astype(BF16)


def _gdn_pre(q, k, v, sm, cpb):
    b, lext, _ = q.shape
    ntot = lext // CHUNK
    rows = cpb * CHUNK
    lead = (b, ntot, 2, GDN_HEADS)

    def tok(bi, i):
        return (bi, i, 0)

    def blk(*tail, heads=GDN_HEADS):
        return pl.BlockSpec((None, cpb, 2, heads) + tail, lambda bi, i: (bi, i, 0, 0, 0, 0))

    sds = jax.ShapeDtypeStruct
    pairs = GDN_HEADS // 2
    return pl.pallas_call(
        functools.partial(_gdn_pre_kernel, cpb=cpb),
        grid=(b, ntot // cpb),
        in_specs=[pl.BlockSpec((None, rows, GDN_QK), tok), pl.BlockSpec((None, rows, GDN_QK), tok),
                  pl.BlockSpec((None, rows, GDN_V), tok), pl.BlockSpec((None, rows, LANE), tok)],
        out_specs=[blk(CHUNK, GDN_DV), blk(2 * CHUNK, GDN_DK), blk(CHUNK + GDN_DK, 2 * CHUNK, heads=pairs),
                   pl.BlockSpec((None, cpb, SUBLANE, LANE), lambda bi, i: (bi, i, 0, 0))],
        out_shape=[sds(lead + (CHUNK, GDN_DV), BF16), sds(lead + (2 * CHUNK, GDN_DK), BF16),
                   sds((b, ntot, 2, pairs, CHUNK + GDN_DK, 2 * CHUNK), BF16), sds((b, ntot, SUBLANE, LANE), F32)],
        compiler_params=_params("parallel", "parallel"),
        name="gdn_pre",
    )(q, k, v, sm)


def _scan_kernel(u0f, cqf, akf, ggf, mf, qdf, lgf, u0b, cqb, akb, ggb, mb, qdb, lgb,
                 gof_ref, gob_ref, lof_ref, lob_ref, sg_ref, sl_ref, *, nctx):
    step = pl.program_id(1)

    @pl.when(step == 0)
    def _():
        sg_ref[...] = jnp.zeros(sg_ref.shape, F32)
        sl_ref[...] = jnp.zeros(sl_ref.shape, F32)

    refs = ((u0f, cqf, akf, ggf, mf, qdf, lgf, gof_ref, lof_ref), (u0b, cqb, akb, ggb, mb, qdb, lgb, gob_ref, lob_ref))
    insts = [(d, h) for d in range(2) for h in range(GDN_HEADS)]
    dot = functools.partial(jnp.dot, preferred_element_type=F32)
    cps = u0f.shape[0]
    s = {i: sg_ref[i[0], i[1]] for i in insts}
    sl = {i: sl_ref[i[0], i[1]] for i in insts}
    def pair_rhs(xa, xb):
        z = jnp.zeros(xa.shape, xa.dtype)
        return jnp.concatenate([jnp.concatenate([xa, z], axis=1), jnp.concatenate([z, xb], axis=1)], axis=0)

    pairs = [(d, p) for d in range(2) for p in range(GDN_HEADS // 2)]
    for ci in range(cps):
        cd = (ci, cps - 1 - ci)
        cqs = {(d, h): dot(refs[d][1][cd[d], h], s[d, h].astype(BF16)) for d, h in insts}
        ol, r = {}, {}
        for d, p in pairs:
            olp = dot(refs[d][5][cd[d], p], pair_rhs(sl[d, 2 * p].astype(BF16), sl[d, 2 * p + 1].astype(BF16)))
            ol[d, 2 * p], ol[d, 2 * p + 1] = olp[:, 0:GLA_DV], olp[:, GLA_DV:2 * GLA_DV]
        u = {(d, h): refs[d][0][cd[d], h].astype(F32) - cqs[d, h][0:CHUNK] for d, h in insts}
        for d, p in pairs:
            rp = dot(refs[d][2][cd[d], p], pair_rhs(u[d, 2 * p].astype(BF16), u[d, 2 * p + 1].astype(BF16)))
            r[d, 2 * p], r[d, 2 * p + 1] = rp[:, 0:GDN_DV], rp[:, GDN_DV:2 * GDN_DV]
        for d, h in insts:
            lane = d * GDN_HEADS + h
            s[d, h] = s[d, h] * refs[d][3][cd[d], lane:lane + 1, :] + r[d, h][CHUNK:CHUNK + GDN_DK]
            sl[d, h] = sl[d, h] * refs[d][6][cd[d], :, lane:lane + 1] + refs[d][4][cd[d], h].astype(F32)

        @pl.when(step >= nctx)
        def _():
            for d, h in insts:
                rows = slice(cd[d] * CHUNK, (cd[d] + 1) * CHUNK)
                og = cqs[d, h][CHUNK:2 * CHUNK] + r[d, h][0:CHUNK]
                refs[d][7][rows, h * GDN_DV:(h + 1) * GDN_DV] = og.astype(BF16)
                refs[d][8][rows, h * GLA_DV:(h + 1) * GLA_DV] = ol[d, h].astype(BF16)

    for d, h in insts:
        sg_ref[d, h] = s[d, h]
        sl_ref[d, h] = sl[d, h]


def _scan(u0, cq, akd, ggl, m, qd, lgl, nctx, cps):
    b, nchunks = u0.shape[:2]
    assert nctx % cps == 0 and nchunks % cps == 0
    nlat = nchunks - nctx
    ntot, nctx = nchunks // cps, nctx // cps

    def specs(d):
        def chunk(bi, s):
            return (bi, _ext_chunk(d, s, nctx, ntot))

        def big(*tail, heads=GDN_HEADS):
            return pl.BlockSpec((None, cps, None, heads) + tail, lambda bi, s: chunk(bi, s) + (d, 0, 0, 0))

        def small(*tail):
            return pl.BlockSpec((None, cps) + tail, lambda bi, s: chunk(bi, s) + (0, 0))

        half = GDN_HEADS // 2
        return [big(CHUNK, GDN_DV), big(2 * CHUNK, GDN_DK), big(CHUNK + GDN_DK, 2 * CHUNK, heads=half),
                small(SUBLANE, LANE), big(GLA_DK, GLA_DV), big(CHUNK, 2 * GLA_DK, heads=half), small(GLA_DK, LANE)]

    def out_spec(d, width):
        return pl.BlockSpec((None, cps * CHUNK, width), lambda bi, s: (bi, _lat_chunk(d, s, nctx, ntot), 0))

    sds = jax.ShapeDtypeStruct
    return pl.pallas_call(
        functools.partial(_scan_kernel, nctx=nctx),
        grid=(b, ntot),
        in_specs=specs(0) + specs(1),
        out_specs=[out_spec(0, GDN_V), out_spec(1, GDN_V), out_spec(0, GLA_V), out_spec(1, GLA_V)],
        out_shape=[sds((b, nlat * CHUNK, GDN_V), BF16)] * 2 + [sds((b, nlat * CHUNK, GLA_V), BF16)] * 2,
        scratch_shapes=[pltpu.VMEM((2, GDN_HEADS, GDN_DK, GDN_DV), F32),
                        pltpu.VMEM((2, GLA_HEADS, GLA_DK, GLA_DV), F32)],
        compiler_params=_params("parallel", "arbitrary"),
        name="scan",
    )(u0, cq, akd, ggl, m, qd, lgl, u0, cq, akd, ggl, m, qd, lgl)


def _gla_pre_kernel(q_ref, k_ref, v_ref, la_ref, m_ref, qd_ref, gl_ref, intra_ref, *, cpb):
    tri = [_tri_masks(d) for d in range(2)]
    for c in range(cpb):
        rows = slice(c * CHUNK, (c + 1) * CHUNK)
        q = q_ref[rows, :] * (GLA_DK ** -0.5)
        k = k_ref[rows, :]
        v = v_ref[rows, :]
        la = la_ref[rows, :]
        pref = _cumsum_rows(tri[0][0], la)
        tot = jnp.sum(la, axis=0, keepdims=True)
        lanes = lax.broadcasted_iota(jnp.int32, la.shape, 1)
        bwd_lane = lanes >= GLA_QK
        bc = jnp.where(bwd_lane, tot - pref + la, pref)
        mid = CHUNK // 2
        ref = jnp.where(bwd_lane[0:1, :], bc[CHUNK - 1 - mid:CHUNK - mid, :], bc[mid:mid + 1, :])
        q2 = jnp.concatenate([q, q], axis=1)
        k2 = jnp.concatenate([k, k], axis=1)
        qa = q2 * jnp.exp(bc - ref)
        ka = k2 * jnp.exp(ref - bc)
        qd = (q2 * jnp.exp(bc)).astype(BF16)
        kdt = (k2 * jnp.exp(tot - bc)).T
        glcol = jnp.exp(jnp.sum(la.T, axis=1, keepdims=True))
        out_lanes = lax.broadcasted_iota(jnp.int32, (GLA_DK, LANE), 1)
        gl = jnp.zeros((GLA_DK, LANE), F32)
        for h in range(GLA_HEADS):
            vh = v[:, h * GLA_DV:(h + 1) * GLA_DV]
            intra = None
            for d in range(2):
                cs = slice(d * GLA_QK + h * GLA_DK, d * GLA_QK + (h + 1) * GLA_DK)
                attn = jnp.where(tri[d][0], _dot_nt(qa[:, cs], ka[:, cs]), 0.0)
                im = _dot(jnp.concatenate([attn, kdt[cs, :]], axis=0), vh)
                intra = im[0:CHUNK] if intra is None else intra + im[0:CHUNK]
                m_ref[c, d, h] = im[CHUNK:CHUNK + GLA_DK].astype(BF16)
                gl = jnp.where(out_lanes == d * GLA_HEADS + h, glcol[cs, :], gl)
            intra_ref[rows, h * GLA_DV:(h + 1) * GLA_DV] = intra.astype(BF16)
        for d in range(2):
            for p in range(GLA_HEADS // 2):
                qd_ref[c, d, p] = qd[:, d * GLA_QK + 2 * p * GLA_DK:d * GLA_QK + 2 * (p + 1) * GLA_DK]
        gl_ref[c] = gl


def _gla_pre(rest, loga, cpb):
    b, lext, _ = rest.shape
    ntot = lext // CHUNK
    rows = cpb * CHUNK
    lead = (b, ntot, 2, GLA_HEADS)

    def col(colblk):
        return lambda bi, i: (bi, i, colblk)

    def blk(*tail, heads=GLA_HEADS):
        return pl.BlockSpec((None, cpb, 2, heads) + tail, lambda bi, i: (bi, i, 0, 0, 0, 0))

    sds = jax.ShapeDtypeStruct
    return pl.pallas_call(
        functools.partial(_gla_pre_kernel, cpb=cpb),
        grid=(b, ntot // cpb),
        in_specs=[pl.BlockSpec((None, rows, GLA_QK), col((COL_LQ - COL_GZ) // GLA_QK)),
                  pl.BlockSpec((None, rows, GLA_QK), col((COL_LK - COL_GZ) // GLA_QK)),
                  pl.BlockSpec((None, rows, GLA_V), col((COL_LV - COL_GZ) // GLA_V)),
                  pl.BlockSpec((None, rows, 2 * GLA_QK), col(0))],
        out_specs=[blk(GLA_DK, GLA_DV), blk(CHUNK, 2 * GLA_DK, heads=GLA_HEADS // 2),
                   pl.BlockSpec((None, cpb, GLA_DK, LANE), lambda bi, i: (bi, i, 0, 0)),
                   pl.BlockSpec((None, rows, GLA_V), col(0))],
        out_shape=[sds(lead + (GLA_DK, GLA_DV), BF16), sds((b, ntot, 2, GLA_HEADS // 2, CHUNK, 2 * GLA_DK), BF16),
                   sds((b, ntot, GLA_DK, LANE), F32), sds((b, lext, GLA_V), BF16)],
        compiler_params=_params("parallel", "parallel"),
        name="gla_pre",
    )(rest, rest, rest, loga)


def _head_gate(o, gain, z, heads, dv):
    parts = []
    for h in range(heads):
        sl = slice(h * dv, (h + 1) * dv)
        oh = o[:, sl]
        parts.append(oh * lax.rsqrt(jnp.mean(oh * oh, axis=-1, keepdims=True) + EPS) * gain * _silu(z[:, sl]))
    return jnp.concatenate(parts, axis=-1)


def _ab_mix(refs, w_ref):
    gf_ref, gb_ref, lf_ref, lb_ref, li_ref, gz_ref, lg_ref, gn_ref, ln_ref = refs
    og = gf_ref[...].astype(F32) + gb_ref[...].astype(F32)
    ol = lf_ref[...].astype(F32) + lb_ref[...].astype(F32) + li_ref[0].astype(F32)
    yg = _head_gate(og, gn_ref[...], gz_ref[0], GDN_HEADS, GDN_DV)
    yl = _head_gate(ol, ln_ref[...], lg_ref[0], GLA_HEADS, GLA_DV)
    return (jnp.dot(yg.astype(BF16), w_ref[0:GDN_V, :], preferred_element_type=F32)
            + jnp.dot(yl.astype(BF16), w_ref[GDN_V:GDN_V + GLA_V, :], preferred_element_type=F32))


def _hy_mix(refs, w_ref):
    return jnp.dot(refs[0][...].astype(BF16), w_ref[...], preferred_element_type=F32)


def _tail_kernel(*refs, n_mix, mix_fn, nsplit, final_norm):
    mix_refs = refs[:n_mix]
    wo_ref, h_ref, g1_ref, gn_ref, sh_ref, sc_ref, g2_ref, w1_ref, w3_ref, w2_ref, fn_ref, o_ref = refs[n_mix:]
    h = h_ref[...] + g1_ref[...] * mix_fn(mix_refs, wo_ref)
    a = _modulated(h, gn_ref[...], sh_ref[...], sc_ref[...]).astype(BF16)
    fc = w1_ref.shape[1] // nsplit
    acc = jnp.zeros(h.shape, F32)
    for j in range(nsplit):
        u1 = jnp.dot(a, w1_ref[:, j * fc:(j + 1) * fc], preferred_element_type=F32)
        u3 = jnp.dot(a, w3_ref[:, j * fc:(j + 1) * fc], preferred_element_type=F32)
        acc = acc + jnp.dot((_silu(u1) * u3).astype(BF16), w2_ref[j * fc:(j + 1) * fc, :],
                            preferred_element_type=F32)
    y = h + g2_ref[...] * acc
    if final_norm:
        y = y * lax.rsqrt(jnp.mean(y * y, axis=-1, keepdims=True) + EPS) * fn_ref[...]
    o_ref[...] = y


def _tail(mix_args, mix_specs, mix_fn, w_out, h, g1, gain, shift, scale, g2, w1, w3, w2, final_w, tm, final_norm):
    b, l, d = h.shape
    dff = w1.shape[1]

    def tok(bi, i):
        return (bi, i, 0)

    def vec(bi, i):
        return (bi, 0, 0)

    def const(bi, i):
        return (0, 0)

    return pl.pallas_call(
        functools.partial(_tail_kernel, n_mix=len(mix_args), mix_fn=mix_fn, nsplit=2, final_norm=final_norm),
        grid=(b, l // tm),
        in_specs=list(mix_specs) + [
            _resident(w_out.shape, const), pl.BlockSpec((None, tm, d), tok), pl.BlockSpec((None, 1, d), vec),
            _resident((1, d), const), pl.BlockSpec((None, 1, d), vec), pl.BlockSpec((None, 1, d), vec),
            pl.BlockSpec((None, 1, d), vec), _resident((d, dff), const), _resident((d, dff), const),
            _resident((dff, d), const), _resident((1, d), const)],
        out_specs=pl.BlockSpec((None, tm, d), tok),
        out_shape=jax.ShapeDtypeStruct((b, l, d), F32),
        compiler_params=_params("parallel", "parallel"),
        name="mix_out_ffn",
    )(*mix_args, w_out, h, g1, gain.reshape(1, d), shift, scale, g2, w1, w3, w2, final_w.reshape(1, d))


def _hy_hidden_kernel(fb_ref, w1t_ref, w1c_ref, w1s_ref, b1_ref, w2_ref, b2_ref, w3_ref, b3_ref, fr_ref, o_ref,
                      *, tile, seq_len):
    half = tile // 2
    hid = o_ref.shape[1]
    lane_hi = lax.broadcasted_iota(jnp.int32, (half, 2 * hid), 1) >= hid
    rows = (pl.program_id(0) * tile + lax.broadcasted_iota(jnp.int32, (half, 2 * hid), 0)
            + jnp.where(lane_hi, half, 0))
    pos = jnp.where(rows < seq_len, rows, 2 * seq_len - rows).astype(F32)
    t = pos * (1.0 / (seq_len - 1))
    ang = (pos * (2.0 * math.pi / seq_len)) * fb_ref[...]
    fr = fr_ref[...]
    pre = t * w1t_ref[...] + _dot_hi(jnp.cos(ang), w1c_ref[...]) - _dot_hi(jnp.sin(ang), w1s_ref[...])
    hdn = jnp.sin(fr * (pre + b1_ref[...]))
    hdn = jnp.sin(fr * (_dot_hi(hdn, w2_ref[...]) + b2_ref[...]))
    hdn = jnp.sin(fr * (_dot_hi(hdn, w3_ref[...]) + b3_ref[...]))
    o_ref[0:half, :] = hdn[:, 0:hid]
    o_ref[half:tile, :] = hdn[:, hid:2 * hid]


def _hy_hidden(w1, b1, w2, b2, w3, b3, freq, seq_len, tile):
    hid = w1.shape[1]
    assert 2 * hid == LANE and HY_BANDS <= hid

    def twice(v):
        return jnp.tile(v.reshape(1, hid), (1, 2))

    def block_diag(w):
        k = w.shape[0]
        return jnp.zeros((2 * hid, 2 * hid), F32).at[:k, :hid].set(w).at[hid:hid + k, hid:].set(w)

    bands = jnp.linspace(1e-4, HY_BANDS - 1, HY_BANDS, dtype=F32)
    fb = jnp.zeros((1, 2 * hid), F32).at[0, :HY_BANDS].set(bands).at[0, hid:hid + HY_BANDS].set(bands)
    args = (fb, twice(w1[0]), block_diag(w1[1:1 + HY_BANDS]), block_diag(w1[1 + HY_BANDS:1 + 2 * HY_BANDS]),
            twice(b1), block_diag(w2), twice(b2), block_diag(w3), twice(b3), twice(freq))
    return pl.pallas_call(
        functools.partial(_hy_hidden_kernel, tile=tile, seq_len=seq_len),
        grid=(2 * seq_len // tile,),
        in_specs=[pl.BlockSpec(a.shape, lambda i: (0, 0)) for a in args],
        out_specs=pl.BlockSpec((tile, hid), lambda i: (i, 0)),
        out_shape=jax.ShapeDtypeStruct((2 * seq_len, hid), F32),
        compiler_params=_params("parallel"),
        name="hy_hidden",
    )(*args)


def _fft_matrices(r):
    n = r * r
    idx = jnp.arange(r, dtype=jnp.int32)

    def cis(m, period):
        ang = (2.0 * math.pi / period) * m.astype(F32)
        return jnp.cos(ang), -jnp.sin(ang)

    fr, fi = cis((idx[:, None] * idx[None, :]) % r, r)
    half = r // 2
    f1_data = jnp.block([[fr[:, :half], -fi[:, :half]], [fi[:, :half], fr[:, :half]]])
    f1_real = jnp.concatenate([fr, fi], axis=0)
    f3 = jnp.block([[fr[:half, :], fi[:half, :]], [-fi[:half, :], fr[:half, :]]]) * (1.0 / n)
    tr, ti = cis((idx[:, None] * idx[None, :]) % n, n)
    gr = tr[:, None, :] * fr[None, :, :] - ti[:, None, :] * fi[None, :, :]
    gi = tr[:, None, :] * fi[None, :, :] + ti[:, None, :] * fr[None, :, :]
    g = jnp.concatenate([jnp.concatenate([gr, -gi], axis=2), jnp.concatenate([gi, gr], axis=2)], axis=1)
    gh = jnp.swapaxes(g, 1, 2)
    return f1_data.astype(BF16), f1_real.astype(BF16), f3.astype(BF16), g.astype(BF16), gh.astype(BF16)


FFT_BG = 16
FFT_DT = 256


def _slab(ref, j):
    return jnp.concatenate([ref[i, :, j, :] for i in range(ref.shape[0])], axis=0)


def _store_interleaved(o_ref, vals):
    x = jnp.swapaxes(jnp.stack([v.astype(o_ref.dtype) for v in vals], axis=0), 0, 1)
    o_ref[...] = x.reshape(o_ref.shape)


def _time_block(arr, sel, n):
    lead = arr.ndim - 4
    return pl.BlockSpec((None,) * lead + (n, arr.shape[-3], FFT_BG, FFT_DT),
                        lambda p, g, dd: tuple(sel) + (p, 0, g, dd))


def _fft_stage1_kernel(f_ref, x_ref, o_ref):
    _store_interleaved(o_ref, [jnp.dot(f_ref[...], _slab(x_ref, j).astype(BF16), preferred_element_type=F32)
                               for j in range(x_ref.shape[2])])


def _fft_stage1(f1, x5, sel, n):
    r2, _ = f1.shape
    r = r2 // 2
    items, _, rb, d = x5.shape[-4:]
    groups = items // n
    return pl.pallas_call(
        _fft_stage1_kernel,
        grid=(groups, rb // FFT_BG, d // FFT_DT),
        in_specs=[_resident(f1.shape, lambda p, g, dd: (0, 0)), _time_block(x5, sel, n)],
        out_specs=pl.BlockSpec((None, 2, r, FFT_BG, FFT_DT), lambda p, g, dd: (p, 0, 0, g, dd)),
        out_shape=jax.ShapeDtypeStruct((groups, 2, r, rb, d), BF16),
        compiler_params=_params("parallel", "parallel", "parallel"),
        name="fft_stage1",
    )(f1, x5)


def _fft_filter_stage1_kernel(f_ref, hdn_ref, fo_ref, dl_ref, o_ref, *, seq_len):
    r, bg, _ = hdn_ref.shape
    orders = fo_ref.shape[1]
    half = r // 2
    a_idx = lax.broadcasted_iota(jnp.int32, (r, 1), 0)
    ps = [[] for _ in range(orders)]
    for j in range(bg):
        rows = a_idx * r + (pl.program_id(0) * bg + j)
        pos = jnp.where(rows < seq_len, rows, 2 * seq_len - rows).astype(F32)
        t = pos * (1.0 / (seq_len - 1))
        window = jnp.where(rows != seq_len, jnp.exp(-t * dl_ref[...]) + HY_SHIFT, 0.0)
        hj = hdn_ref[:, j, :]
        for n in range(orders):
            taps = jnp.concatenate([_dot(hj[:half], fo_ref[0, n]), _dot(hj[half:], fo_ref[1, n])], axis=0) * window
            ps[n].append(jnp.dot(f_ref[...], taps.astype(BF16), preferred_element_type=F32))
    for n in range(orders):
        _store_interleaved(o_ref.at[n], ps[n])


def _fft_filter_stage1(f1, hdn3, fo, deltas, seq_len):
    r2, r = f1.shape
    hid = hdn3.shape[-1]
    _, orders, _, d = fo.shape
    return pl.pallas_call(
        functools.partial(_fft_filter_stage1_kernel, seq_len=seq_len),
        grid=(r // FFT_BG, d // FFT_DT),
        in_specs=[_resident(f1.shape, lambda g, dd: (0, 0)),
                  pl.BlockSpec((r, FFT_BG, hid), lambda g, dd: (0, g, 0)),
                  pl.BlockSpec((2, orders, hid, FFT_DT), lambda g, dd: (0, 0, 0, dd)),
                  pl.BlockSpec((1, FFT_DT), lambda g, dd: (0, dd))],
        out_specs=pl.BlockSpec((orders, 2, r, FFT_BG, FFT_DT), lambda g, dd: (0, 0, 0, g, dd)),
        out_shape=jax.ShapeDtypeStruct((orders, 2, r, r, d), BF16),
        compiler_params=_params("parallel", "parallel"),
        name="fft_filter_stage1",
    )(f1, hdn3, fo, deltas)


def _fft_stage2_kernel(g_ref, gh_ref, k_ref, p_ref, o_ref):
    r = p_ref.shape[2]
    kq = k_ref[...]
    h = jnp.dot(g_ref[...], kq.reshape(2 * r, kq.shape[2]), preferred_element_type=F32)
    hr, hi = h[:r], h[r:]
    for n in range(p_ref.shape[0]):
        p = p_ref[n]
        x = jnp.dot(g_ref[...], p.reshape(2 * r, p.shape[2]), preferred_element_type=F32)
        xr, xi = x[:r], x[r:]
        y = jnp.concatenate([xr * hr - xi * hi, xr * hi + xi * hr], axis=0)
        q = jnp.dot(gh_ref[...], y.astype(BF16), preferred_element_type=F32)
        o_ref[n] = q.reshape(o_ref.shape[1:]).astype(o_ref.dtype)


def _fft_stage2(g, gh, pk, order, p5):
    pairs, _, r, _, d = p5.shape
    return pl.pallas_call(
        _fft_stage2_kernel,
        grid=(r,),
        in_specs=[pl.BlockSpec((None, 2 * r, 2 * r), lambda kk: (kk, 0, 0)),
                  pl.BlockSpec((None, 2 * r, 2 * r), lambda kk: (kk, 0, 0)),
                  pl.BlockSpec((None, 2, None, r, d), lambda kk: (order, 0, kk, 0, 0)),
                  pl.BlockSpec((pairs, 2, None, r, d), lambda kk: (0, 0, kk, 0, 0))],
        out_specs=pl.BlockSpec((pairs, 2, None, r, d), lambda kk: (0, 0, kk, 0, 0)),
        out_shape=jax.ShapeDtypeStruct(p5.shape, BF16),
        compiler_params=_params("parallel"),
        name="fft_stage2",
    )(g, gh, pk, p5)


def _rows_by_b(ref):
    bg = ref.shape[2]
    if ref.dtype == F32:
        return [_slab(ref, j) for j in range(bg)]
    x = ref[...]
    x = jnp.swapaxes(x.reshape(x.shape[0] * x.shape[1], bg, x.shape[3]), 0, 1)
    return [x[j] for j in range(bg)]


def _fft_stage3_kernel(f3_ref, f1_ref, q_ref, z_ref, gate_ref, skip_ref, zo_ref, po_ref=None):
    q, z, gate = _rows_by_b(q_ref), _rows_by_b(z_ref), _rows_by_b(gate_ref)
    skip = skip_ref[...]
    zs, ps = [], []
    for j in range(len(q)):
        y = jnp.dot(f3_ref[...], q[j], preferred_element_type=F32)
        zn = gate[j] * (y + z[j] * skip)
        zs.append(zn)
        if po_ref is not None:
            ps.append(jnp.dot(f1_ref[...], zn.astype(BF16), preferred_element_type=F32))
    _store_interleaved(zo_ref, zs)
    if po_ref is not None:
        _store_interleaved(po_ref, ps)


def _fft_stage3(f3, f1, q5, z5, z_sel, g5, g_sel, skip, chain):
    pairs, _, r, rb, d = q5.shape
    a = z5.shape[-3]
    sds = jax.ShapeDtypeStruct
    out_specs = [pl.BlockSpec((2, a, FFT_BG, FFT_DT), lambda p, g, dd: (p, 0, g, dd))]
    out_shape = [sds((2 * pairs, a, rb, d), BF16)]
    if chain:
        out_specs.append(pl.BlockSpec((None, 2, r, FFT_BG, FFT_DT), lambda p, g, dd: (p, 0, 0, g, dd)))
        out_shape.append(sds(q5.shape, BF16))
    return pl.pallas_call(
        _fft_stage3_kernel,
        grid=(pairs, rb // FFT_BG, d // FFT_DT),
        in_specs=[_resident(f3.shape, lambda p, g, dd: (0, 0)), _resident(f1.shape, lambda p, g, dd: (0, 0)),
                  pl.BlockSpec((None, 2, r, FFT_BG, FFT_DT), lambda p, g, dd: (p, 0, 0, g, dd)),
                  _time_block(z5, z_sel, 2), _time_block(g5, g_sel, 2),
                  pl.BlockSpec((1, FFT_DT), lambda p, g, dd: (0, dd))],
        out_specs=out_specs,
        out_shape=out_shape,
        compiler_params=_params("parallel", "parallel", "parallel"),
        name="fft_stage3",
    )(f3, f1, q5, z5, g5, skip)


TAIL_ROWS = 512
HY_FRONT_ROWS = 512
AB_FRONT_ROWS = 384


def _regroup_ab_weights(w_in):
    d = w_in.shape[0]
    sizes = (GDN_QK, GDN_QK, GDN_V, GDN_V, 2 * GDN_HEADS, 2 * GDN_HEADS, GLA_QK, GLA_QK, GLA_V, GLA_V,
             2 * GLA_GATE_RANK)
    offs = [0]
    for s in sizes:
        offs.append(offs[-1] + s)
    gq, gk, gv, gz, ga, gb, lq, lk, lv, lg, llr = (w_in[:, offs[i]:offs[i + 1]] for i in range(len(sizes)))
    pad = jnp.zeros((d, LANE - (ga.shape[1] + gb.shape[1] + llr.shape[1])), w_in.dtype)
    return jnp.concatenate([gq, gk, gv, gz, lq, lk, lv, lg, ga, gb, llr, pad], axis=1)


def _lane_row(vals, offset):
    flat = vals.reshape(-1).astype(F32)
    return jnp.zeros((1, LANE), F32).at[0, offset:offset + flat.shape[0]].set(flat)


def _even_layer(h, ctx, mods, norm1, norm2, w_in, conv_w, a_log, dt_bias, gdn_norm, gate_w, gate_b, gla_norm,
                w_out, w1, w3, w2, final_w, final_norm):
    b, l, d = h.shape
    lctx = ctx.shape[1]
    tm = TAIL_ROWS
    sh1, sc1, g1, sh2, sc2, g2 = mods["lat"]
    csh1, csc1 = mods["ctx"][0], mods["ctx"][1]
    xe = jnp.concatenate([ctx, h], axis=1)
    shift = jnp.stack([jnp.broadcast_to(csh1, (b, d)), sh1], axis=1)[:, :, None, :]
    scale = jnp.stack([jnp.broadcast_to(csc1, (b, d)), sc1], axis=1)[:, :, None, :]
    gw = jnp.zeros((LANE, 2 * GLA_QK), F32)
    for dd in range(2):
        gw = gw.at[SM_LR + dd * GLA_GATE_RANK:SM_LR + (dd + 1) * GLA_GATE_RANK,
                   dd * GLA_QK:(dd + 1) * GLA_QK].set(gate_w[dd])
    q, k, v, sm, loga, rest = _ab_front(xe, norm1, shift, scale, _regroup_ab_weights(w_in).astype(BF16), conv_w,
                                        _lane_row(a_log, SM_G), _lane_row(dt_bias, SM_G), gw,
                                        gate_b.reshape(1, 2 * GLA_QK), AB_FRONT_ROWS, lctx)
    nctx = lctx // CHUNK
    u0, cq, akd, ggl = _gdn_pre(q, k, v, sm, 2)
    m, qd, lgl, ol_intra = _gla_pre(rest, loga, 2)
    og_f, og_b, ol_f, ol_b = _scan(u0, cq, akd, ggl, m, qd, lgl, nctx, 4)
    def lat(width):
        return pl.BlockSpec((None, tm, width), lambda bi, i: (bi, i, 0))

    def ext(width, colblk):
        return pl.BlockSpec((pl.Element(1), pl.Element(tm), pl.Element(width)),
                            lambda bi, i: (bi, SUBLANE * (lctx // SUBLANE + i * (tm // SUBLANE)), colblk * width))

    mix_args = (og_f, og_b, ol_f, ol_b, ol_intra, rest, rest, gdn_norm.reshape(1, GDN_DV), gla_norm.reshape(1, GLA_DV))
    mix_specs = [lat(GDN_V), lat(GDN_V), lat(GLA_V), lat(GLA_V), ext(GLA_V, 0), ext(GDN_V, 0),
                 ext(GLA_V, (COL_LG - COL_GZ) // GLA_V), _resident((1, GDN_DV), lambda bi, i: (0, 0)),
                 _resident((1, GLA_DV), lambda bi, i: (0, 0))]
    return _tail(mix_args, mix_specs, _ab_mix, w_out.astype(BF16), h, g1[:, None, :], norm2, sh2[:, None, :],
                 sc2[:, None, :], g2[:, None, :], w1.astype(BF16), w3.astype(BF16), w2.astype(BF16), final_w, tm,
                 final_norm)


def _hyena_layer(h, mods, norm1, norm2, w_in, conv_w, pw1, pb1, pw2, pb2, pw3, pb3, freq, filt_out, skip, w_out,
                 w1, w3, w2, final_w, final_norm):
    b, l, d = h.shape
    tm = TAIL_ROWS
    sh1, sc1, g1, sh2, sc2, g2 = mods["lat"]
    u3 = _hy_front(h, norm1, sh1[:, None, :], sc1[:, None, :], w_in.astype(BF16), conv_w, HY_FRONT_ROWS)

    n2 = 2 * l
    r = math.isqrt(n2)
    assert r * r == n2 and b % 2 == 0
    orders = skip.shape[0]
    hdn = _hy_hidden(pw1, pb1, pw2, pb2, pw3, pb3, freq, l, 512)
    fo = jnp.transpose(filt_out.reshape(filt_out.shape[0], 2, orders, d), (1, 2, 0, 3))
    deltas = jnp.abs(jnp.linspace(HY_MIN_DECAY, HY_MAX_DECAY, d, dtype=F32))[None, :]

    f1_data, f1_real, f3, g, gh = _fft_matrices(r)
    pk = _fft_filter_stage1(f1_real, hdn.reshape(r, r, hdn.shape[1]), fo, deltas, l)

    u5 = u3.reshape(3, b, r // 2, r, d)
    z_arr, z_sel = u5, (0,)
    p = _fft_stage1(f1_data, z_arr, z_sel, 2)
    for n in range(orders):
        q = _fft_stage2(g, gh, pk, n, p)
        chain = n + 1 < orders
        res = _fft_stage3(f3, f1_data, q, z_arr, z_sel, u5, (n + 1,), skip[n][None, :], chain)
        z_arr, z_sel = res[0], ()
        if chain:
            p = res[1]
    mix_specs = [pl.BlockSpec((None, tm, d), lambda bi, i: (bi, i, 0))]
    return _tail((z_arr.reshape(b, l, d),), mix_specs, _hy_mix, w_out.astype(BF16), h, g1[:, None, :], norm2,
                 sh2[:, None, :], sc2[:, None, :], g2[:, None, :], w1.astype(BF16), w3.astype(BF16), w2.astype(BF16),
                 final_w, tm, final_norm)


def kernel(x, c, ctx, c_ctx, mod_w, mod_b, norm1_w, norm2_w, ab_w_in, ab_conv_w, gdn_a_log, gdn_dt_bias, gdn_norm_w, gla_gate_w, gla_gate_b, gla_norm_w, ab_w_out, hy_w_in, hy_conv_w, hy_pos_w1, hy_pos_b1, hy_pos_w2, hy_pos_b2, hy_pos_w3, hy_pos_b3, hy_freq, hy_filt_out, hy_skip, hy_w_out, ffn_w1, ffn_w3, ffn_w2, final_norm_w):
    b, l, d = x.shape
    depth = mod_w.shape[0]
    assert b + 1 <= SUBLANE
    cvec = jnp.concatenate([c, c_ctx[None, :], jnp.zeros((SUBLANE - b - 1, d), F32)], axis=0)
    mod = _modulation(cvec, mod_w, mod_b)
    h = x
    for i in range(depth):
        mods = {"lat": [mod[i, :b, j * d:(j + 1) * d] for j in range(6)],
                "ctx": [mod[i, b, j * d:(j + 1) * d] for j in range(6)]}
        last = i == depth - 1
        if i % 2 == 0:
            assert not any(j % 2 == 0 for j in range(i + 1, depth))
            e = i // 2
            h = _even_layer(h, ctx, mods, norm1_w[i], norm2_w[i], ab_w_in[e], ab_conv_w[e], gdn_a_log[e],
                            gdn_dt_bias[e], gdn_norm_w[e], gla_gate_w[e], gla_gate_b[e], gla_norm_w[e], ab_w_out[e],
                            ffn_w1[i], ffn_w3[i], ffn_w2[i], final_norm_w, last)
        else:
            o = i // 2
            h = _hyena_layer(h, mods, norm1_w[i], norm2_w[i], hy_w_in[o], hy_conv_w[o], hy_pos_w1[o], hy_pos_b1[o],
                             hy_pos_w2[o], hy_pos_b2[o], hy_pos_w3[o], hy_pos_b3[o], hy_freq[o], hy_filt_out[o],
                             hy_skip[o], hy_w_out[o], ffn_w1[i], ffn_w3[i], ffn_w2[i], final_norm_w, last)
    return h
```

```python
import functools
import math

import jax
import jax.numpy as jnp
from jax import lax
from jax.experimental import pallas as pl
from jax.experimental.pallas import tpu as pltpu

F32 = jnp.float32
BF16 = jnp.bfloat16
HI = lax.Precision.HIGHEST

EPS = 1e-6
CHUNK = 64
GDN_HEADS, GDN_DK, GDN_DV = 4, 128, 128
GLA_HEADS, GLA_DK, GLA_DV = 4, 64, 128
GLA_GATE_RANK = 16
GLA_GATE_TAU = 16.0
GDN_QK = GDN_HEADS * GDN_DK
GDN_V = GDN_HEADS * GDN_DV
GLA_QK = GLA_HEADS * GLA_DK
GLA_V = GLA_HEADS * GLA_DV
HY_EMB = 33
HY_BANDS = (HY_EMB - 1) // 2
HY_MIN_DECAY = math.log(1e-2) / 1.5
HY_MAX_DECAY = math.log(1e-2) / 0.3
HY_SHIFT = 0.05

LANE = 128
SUBLANE = 8
VMEM_LIMIT_BYTES = 56 * 1024 * 1024

COL_GQ, COL_GK, COL_GV, COL_GZ = 0, 512, 1024, 1536
COL_LQ, COL_LK, COL_LV, COL_LG = 2048, 2304, 2560, 3072
COL_SMALL = 3584
SM_G, SM_B, SM_LR = 0, 8, 16


def _params(*sem):
    return pltpu.CompilerParams(dimension_semantics=sem, vmem_limit_bytes=VMEM_LIMIT_BYTES)


def _resident(shape, index_map):
    return pl.BlockSpec(shape, index_map, pipeline_mode=pl.Buffered(1))


def _dot(a, b):
    return jnp.dot(a.astype(BF16), b.astype(BF16), preferred_element_type=F32)


def _dot_nt(a, b):
    return lax.dot_general(a.astype(BF16), b.astype(BF16), (((1,), (1,)), ((), ())),
                           preferred_element_type=F32)


def _dot_hi(a, b):
    return jnp.dot(a, b, preferred_element_type=F32, precision=HI)


def _sigmoid(x):
    return 1.0 / (1.0 + jnp.exp(-x))


def _silu(x):
    return x * _sigmoid(x)


def _softplus(x):
    return jnp.maximum(x, 0.0) + jnp.log(1.0 + jnp.exp(-jnp.abs(x)))


def _mod_kernel(c_ref, w_ref, b_ref, o_ref):
    o_ref[...] = _dot_hi(_silu(c_ref[...]), w_ref[...]) + b_ref[...]


def _modulation(cvec, mod_w, mod_b):
    depth, d, n = mod_w.shape
    tn = n // 4
    return pl.pallas_call(
        _mod_kernel,
        grid=(depth, n // tn),
        in_specs=[pl.BlockSpec((SUBLANE, d), lambda i, j: (0, 0)),
                  pl.BlockSpec((None, d, tn), lambda i, j: (i, 0, j)),
                  pl.BlockSpec((None, 1, tn), lambda i, j: (i, 0, j))],
        out_specs=pl.BlockSpec((None, SUBLANE, tn), lambda i, j: (i, 0, j)),
        out_shape=jax.ShapeDtypeStruct((depth, SUBLANE, n), F32),
        compiler_params=_params("parallel", "parallel"),
        name="modulation",
    )(cvec, mod_w, mod_b.reshape(depth, 1, n))


CONV_HALO = SUBLANE


def _modulated(x, gain, shift, scale):
    ms = jnp.mean(x * x, axis=-1, keepdims=True)
    return x * lax.rsqrt(ms + EPS) * gain * (1.0 + scale) + shift


def _window_start(i, tm, win, seq_len):
    return SUBLANE * jnp.clip(i * (tm // SUBLANE) - CONV_HALO // SUBLANE, 0, (seq_len - win) // SUBLANE)


def _window_spec(tm, win, seq_len, d):
    return pl.BlockSpec((pl.Element(1), pl.Element(win), pl.Element(d)),
                        lambda bi, i: (bi, _window_start(i, tm, win, seq_len), 0))


def _conv3_rows(u, w, gidx=None, starts=(), ends=()):
    rows = u.shape[0]
    prev = pltpu.roll(u, 1, 0)
    nxt = pltpu.roll(u, rows - 1, 0)
    if starts:
        prev = jnp.where(functools.reduce(jnp.logical_or, [gidx == r for r in starts]), 0.0, prev)
    if ends:
        nxt = jnp.where(functools.reduce(jnp.logical_or, [gidx == r for r in ends]), 0.0, nxt)
    return prev * w[0:1, :] + u * w[1:2, :] + nxt * w[2:3, :]


def _hy_front_kernel(x_ref, g_ref, sh_ref, sc_ref, w_ref, cw_ref, o_ref, scr, *, tm, seq_len):
    i = pl.program_id(1)
    win = x_ref.shape[1]
    ws = _window_start(i, tm, win, seq_len)
    off = pl.multiple_of(i * tm - ws, SUBLANE)
    a = _modulated(x_ref[0], g_ref[...], sh_ref[...], sc_ref[...])
    u = jnp.dot(a.astype(BF16), w_ref[...], preferred_element_type=F32)
    cw = cw_ref[...]
    scr[...] = _conv3_rows(u, cw)

    @pl.when(i == 0)
    def _():
        scr[0:1, :] = scr[0:1, :] - u[win - 1:win, :] * cw[0:1, :]

    @pl.when(i == pl.num_programs(1) - 1)
    def _():
        scr[win - 1:win, :] = scr[win - 1:win, :] - u[0:1, :] * cw[2:3, :]

    d = o_ref.shape[-1]
    for g in range(o_ref.shape[0]):
        o_ref[g] = scr[pl.ds(off, tm), g * d:(g + 1) * d]


def _hy_front(x, gain, shift, scale, w, conv_w, tm):
    b, l, d = x.shape
    n = w.shape[1]
    groups = n // d
    win = tm + 2 * CONV_HALO

    def vec(bi, i):
        return (bi, 0, 0)

    return pl.pallas_call(
        functools.partial(_hy_front_kernel, tm=tm, seq_len=l),
        grid=(b, l // tm),
        in_specs=[_window_spec(tm, win, l, d), _resident((1, d), lambda bi, i: (0, 0)),
                  pl.BlockSpec((None, 1, d), vec), pl.BlockSpec((None, 1, d), vec),
                  _resident((d, n), lambda bi, i: (0, 0)), _resident((3, n), lambda bi, i: (0, 0))],
        out_specs=pl.BlockSpec((groups, None, tm, d), lambda bi, i: (0, bi, i, 0)),
        out_shape=jax.ShapeDtypeStruct((groups, b, l, d), F32),
        scratch_shapes=[pltpu.VMEM((win, n), F32)],
        compiler_params=_params("parallel", "parallel"),
        name="hy_front",
    )(x, gain.reshape(1, d), shift, scale, w, conv_w)


def _ab_front_kernel(x_ref, g_ref, sh_ref, sc_ref, w_ref, cw_ref, alog_ref, dtb_ref, gw_ref, gb_ref,
                     q_ref, k_ref, v_ref, so_ref, la_ref, rest_ref, scr, *, tm, lctx, lext):
    i = pl.program_id(1)
    win = x_ref.shape[1]
    ws = _window_start(i, tm, win, lext)
    off = pl.multiple_of(i * tm - ws, SUBLANE)
    gidx = ws + lax.broadcasted_iota(jnp.int32, (win, 1), 0)
    is_ctx = gidx < lctx
    shift = jnp.where(is_ctx, sh_ref[0], sh_ref[1])
    scale = jnp.where(is_ctx, sc_ref[0], sc_ref[1])
    a = _modulated(x_ref[0], g_ref[...], shift, scale)
    u = jnp.dot(a.astype(BF16), w_ref[...], preferred_element_type=F32)
    ncv = cw_ref.shape[1]
    scr[:, 0:ncv] = _conv3_rows(u[:, 0:ncv], cw_ref[...], gidx, (0, lctx), (lctx - 1, lext - 1))
    scr[:, ncv:] = u[:, ncv:]
    rows = pl.ds(off, tm)
    for h in range(GDN_HEADS):
        sl = slice(h * GDN_DK, (h + 1) * GDN_DK)
        qh = _silu(scr[rows, COL_GQ + h * GDN_DK:COL_GQ + (h + 1) * GDN_DK])
        kh = _silu(scr[rows, COL_GK + h * GDN_DK:COL_GK + (h + 1) * GDN_DK])
        q_ref[:, sl] = qh * lax.rsqrt(jnp.sum(qh * qh, axis=-1, keepdims=True) + EPS) * (GDN_DK ** -0.5)
        k_ref[:, sl] = kh * lax.rsqrt(jnp.sum(kh * kh, axis=-1, keepdims=True) + EPS)
    v_ref[...] = _silu(scr[rows, COL_GV:COL_GV + GDN_V])
    rest_ref[...] = scr[rows, COL_GZ:COL_SMALL]
    sm = scr[rows, COL_SMALL:COL_SMALL + LANE]
    g = -jnp.exp(alog_ref[...]) * _softplus(sm + dtb_ref[...])
    beta = _sigmoid(sm)
    lanes = lax.broadcasted_iota(jnp.int32, sm.shape, 1)
    so_ref[...] = jnp.where(lanes < SM_B, g, beta)
    gate = _dot(sm, gw_ref[...]) + gb_ref[...]
    la_ref[...] = (jnp.minimum(gate, 0.0) - jnp.log(1.0 + jnp.exp(-jnp.abs(gate)))) * (1.0 / GLA_GATE_TAU)


def _ab_front(xe, gain, shift, scale, w, conv_w, alog_row, dtb_row, gate_w, gate_b, tm, lctx):
    b, lext, d = xe.shape
    n = w.shape[1]
    win = tm + 2 * CONV_HALO
    nrest = COL_SMALL - COL_GZ

    def const(bi, i):
        return (0, 0)

    def tok(width):
        return pl.BlockSpec((None, tm, width), lambda bi, i: (bi, i, 0))

    sds = jax.ShapeDtypeStruct
    return pl.pallas_call(
        functools.partial(_ab_front_kernel, tm=tm, lctx=lctx, lext=lext),
        grid=(b, lext // tm),
        in_specs=[_window_spec(tm, win, lext, d), _resident((1, d), const),
                  pl.BlockSpec((None, 2, 1, d), lambda bi, i: (bi, 0, 0, 0)),
                  pl.BlockSpec((None, 2, 1, d), lambda bi, i: (bi, 0, 0, 0)),
                  _resident((d, n), const), _resident(conv_w.shape, const),
                  _resident((1, LANE), const), _resident((1, LANE), const),
                  _resident((LANE, 2 * GLA_QK), const), _resident((1, 2 * GLA_QK), const)],
        out_specs=[tok(GDN_QK), tok(GDN_QK), tok(GDN_V), tok(LANE), tok(2 * GLA_QK), tok(nrest)],
        out_shape=[sds((b, lext, GDN_QK), F32), sds((b, lext, GDN_QK), F32), sds((b, lext, GDN_V), F32),
                   sds((b, lext, LANE), F32), sds((b, lext, 2 * GLA_QK), F32), sds((b, lext, nrest), F32)],
        scratch_shapes=[pltpu.VMEM((win, n), F32)],
        compiler_params=_params("parallel", "parallel"),
        name="ab_front",
    )(xe, gain.reshape(1, d), shift, scale, w, conv_w, alog_row, dtb_row, gate_w, gate_b)


def _ext_chunk(d, s, nctx, ntot):
    if d == 0:
        return s
    return jnp.where(s < nctx, nctx - 1 - s, ntot + nctx - 1 - s)


def _lat_chunk(d, s, nctx, ntot):
    nlat = ntot - nctx
    if d == 0:
        return jnp.maximum(s - nctx, 0)
    return jnp.minimum(ntot - 1 - s, nlat - 1)


def _tri_masks(d):
    ii = lax.broadcasted_iota(jnp.int32, (CHUNK, CHUNK), 0)
    jj = lax.broadcasted_iota(jnp.int32, (CHUNK, CHUNK), 1)
    if d == 0:
        return ii >= jj, ii > jj, ii == jj
    return ii <= jj, ii < jj, ii == jj


def _cumsum_rows(incl, x):
    m = incl.astype(BF16)
    x1 = x.astype(BF16)
    r1 = x - x1.astype(F32)
    x2 = r1.astype(BF16)
    x3 = (r1 - x2.astype(F32)).astype(BF16)
    dot = functools.partial(jnp.dot, preferred_element_type=F32)
    return dot(m, x1) + dot(m, x2) + dot(m, x3)


def _merge_masks(d):
    ii = lax.broadcasted_iota(jnp.int32, (CHUNK, CHUNK), 0)
    jj = lax.broadcasted_iota(jnp.int32, (CHUNK, CHUNK), 1)
    lo, hi = (jj, ii) if d == 0 else (ii, jj)
    masks = {}
    s = 1
    while s < CHUNK:
        masks[s] = jnp.logical_and(ii // (2 * s) == jj // (2 * s),
                                   jnp.logical_and((hi // s) % 2 == 1, (lo // s) % 2 == 0))
        s *= 2
    return masks


def _gdn_pre_kernel(q_ref, k_ref, v_ref, sm_ref, u0_ref, cq_ref, akd_ref, gl_ref, *, cpb):
    tri = [_tri_masks(d) for d in range(2)]
    merge = [_merge_masks(d) for d in range(2)]
    insts = [(d, h) for d in range(2) for h in range(GDN_HEADS)]
    for c in range(cpb):
        rows = slice(c * CHUNK, (c + 1) * CHUNK)
        q = q_ref[rows, :]
        k = k_ref[rows, :]
        v = v_ref[rows, :]
        sm = sm_ref[rows, :]
        pref = _cumsum_rows(tri[0][0], sm)
        tot = jnp.sum(sm, axis=0, keepdims=True)
        lanes = lax.broadcasted_iota(jnp.int32, sm.shape, 1)
        bwd_lane = jnp.logical_and(lanes >= SM_G + GDN_HEADS, lanes < SM_G + 2 * GDN_HEADS)
        gc = jnp.where(bwd_lane, tot - pref + sm, pref)
        gct = gc.T
        totc = jnp.sum(sm.T, axis=1, keepdims=True)
        gl_ref[c] = jnp.exp(jnp.broadcast_to(totc[0:SUBLANE, :], (SUBLANE, LANE)))
        egc = jnp.exp(gc)
        etg = jnp.exp(tot - gc)
        kh = [k[:, h * GDN_DK:(h + 1) * GDN_DK] for h in range(GDN_HEADS)]
        qh = [q[:, h * GDN_DK:(h + 1) * GDN_DK] for h in range(GDN_HEADS)]
        vh = [v[:, h * GDN_DV:(h + 1) * GDN_DV] for h in range(GDN_HEADS)]
        gq = [_dot_nt(jnp.concatenate([kh[h], qh[h]], axis=0), kh[h]) for h in range(GDN_HEADS)]
        a, t, beta, ecol = {}, {}, {}, {}
        for d, h in insts:
            incl, strict, diag = tri[d]
            col = SM_G + d * GDN_HEADS + h
            decay = jnp.exp(jnp.where(incl, gc[:, col:col + 1] - gct[col:col + 1, :], -jnp.inf))
            beta[d, h] = sm[:, SM_B + col:SM_B + col + 1]
            ecol[d, h] = egc[:, col:col + 1]
            a[d, h] = jnp.where(strict, gq[h][0:CHUNK] * decay, 0.0) * beta[d, h]
            hl = slice((h % 2) * CHUNK, (h % 2 + 1) * CHUNK)
            akd_ref[c, d, h // 2, 0:CHUNK, hl] = (gq[h][CHUNK:2 * CHUNK] * decay).astype(BF16)
            akd_ref[c, d, h // 2, CHUNK:CHUNK + GDN_DK, hl] = (kh[h] * etg[:, col:col + 1]).T.astype(BF16)
            t[d, h] = diag.astype(F32) - jnp.where(merge[d][1], a[d, h], 0.0)
        s = 2
        while s < CHUNK:
            x = {i: _dot(jnp.where(merge[i[0]][s], a[i], 0.0), t[i]) for i in insts}
            t = {i: t[i] - _dot(t[i], x[i]) for i in insts}
            s *= 2
        for d, h in insts:
            rhs = jnp.concatenate([vh[h] * beta[d, h], kh[h] * (beta[d, h] * ecol[d, h])], axis=1)
            sol = _dot(t[d, h], rhs)
            u0_ref[c, d, h] = sol[:, 0:GDN_DV].astype(BF16)
            cq_ref[c, d, h, 0:CHUNK, :] = sol[:, GDN_DV:GDN_DV + GDN_DK].astype(BF16)
            cq_ref[c, d, h, CHUNK:2 * CHUNK, :] = (qh[h] * ecol[d, h]).astype(BF16)


def _gdn_pre(q, k, v, sm, cpb):
    b, lext, _ = q.shape
    ntot = lext // CHUNK
    rows = cpb * CHUNK
    lead = (b, ntot, 2, GDN_HEADS)

    def tok(bi, i):
        return (bi, i, 0)

    def blk(*tail, heads=GDN_HEADS):
        return pl.BlockSpec((None, cpb, 2, heads) + tail, lambda bi, i: (bi, i, 0, 0, 0, 0))

    sds = jax.ShapeDtypeStruct
    pairs = GDN_HEADS // 2
    return pl.pallas_call(
        functools.partial(_gdn_pre_kernel, cpb=cpb),
        grid=(b, ntot // cpb),
        in_specs=[pl.BlockSpec((None, rows, GDN_QK), tok), pl.BlockSpec((None, rows, GDN_QK), tok),
                  pl.BlockSpec((None, rows, GDN_V), tok), pl.BlockSpec((None, rows, LANE), tok)],
        out_specs=[blk(CHUNK, GDN_DV), blk(2 * CHUNK, GDN_DK), blk(CHUNK + GDN_DK, 2 * CHUNK, heads=pairs),
                   pl.BlockSpec((None, cpb, SUBLANE, LANE), lambda bi, i: (bi, i, 0, 0))],
        out_shape=[sds(lead + (CHUNK, GDN_DV), BF16), sds(lead + (2 * CHUNK, GDN_DK), BF16),
                   sds((b, ntot, 2, pairs, CHUNK + GDN_DK, 2 * CHUNK), BF16), sds((b, ntot, SUBLANE, LANE), F32)],
        compiler_params=_params("parallel", "parallel"),
        name="gdn_pre",
    )(q, k, v, sm)


def _scan_kernel(u0f, cqf, akf, ggf, mf, qdf, lgf, u0b, cqb, akb, ggb, mb, qdb, lgb,
                 gof_ref, gob_ref, lof_ref, lob_ref, sg_ref, sl_ref, *, nctx):
    step = pl.program_id(1)

    @pl.when(step == 0)
    def _():
        sg_ref[...] = jnp.zeros(sg_ref.shape, F32)
        sl_ref[...] = jnp.zeros(sl_ref.shape, F32)

    refs = ((u0f, cqf, akf, ggf, mf, qdf, lgf, gof_ref, lof_ref), (u0b, cqb, akb, ggb, mb, qdb, lgb, gob_ref, lob_ref))
    insts = [(d, h) for d in range(2) for h in range(GDN_HEADS)]
    dot = functools.partial(jnp.dot, preferred_element_type=F32)
    cps = u0f.shape[0]
    s = {i: sg_ref[i[0], i[1]] for i in insts}
    sl = {i: sl_ref[i[0], i[1]] for i in insts}
    def pair_rhs(xa, xb):
        z = jnp.zeros(xa.shape, xa.dtype)
        return jnp.concatenate([jnp.concatenate([xa, z], axis=1), jnp.concatenate([z, xb], axis=1)], axis=0)

    pairs = [(d, p) for d in range(2) for p in range(GDN_HEADS // 2)]
    for ci in range(cps):
        cd = (ci, cps - 1 - ci)
        cqs = {(d, h): dot(refs[d][1][cd[d], h], s[d, h].astype(BF16)) for d, h in insts}
        ol, r = {}, {}
        for d, p in pairs:
            olp = dot(refs[d][5][cd[d], p], pair_rhs(sl[d, 2 * p].astype(BF16), sl[d, 2 * p + 1].astype(BF16)))
            ol[d, 2 * p], ol[d, 2 * p + 1] = olp[:, 0:GLA_DV], olp[:, GLA_DV:2 * GLA_DV]
        u = {(d, h): refs[d][0][cd[d], h].astype(F32) - cqs[d, h][0:CHUNK] for d, h in insts}
        for d, p in pairs:
            rp = dot(refs[d][2][cd[d], p], pair_rhs(u[d, 2 * p].astype(BF16), u[d, 2 * p + 1].astype(BF16)))
            r[d, 2 * p], r[d, 2 * p + 1] = rp[:, 0:GDN_DV], rp[:, GDN_DV:2 * GDN_DV]
        for d, h in insts:
            lane = d * GDN_HEADS + h
            s[d, h] = s[d, h] * refs[d][3][cd[d], lane:lane + 1, :] + r[d, h][CHUNK:CHUNK + GDN_DK]
            sl[d, h] = sl[d, h] * refs[d][6][cd[d], :, lane:lane + 1] + refs[d][4][cd[d], h].astype(F32)

        @pl.when(step >= nctx)
        def _():
            for d, h in insts:
                rows = slice(cd[d] * CHUNK, (cd[d] + 1) * CHUNK)
                og = cqs[d, h][CHUNK:2 * CHUNK] + r[d, h][0:CHUNK]
                refs[d][7][rows, h * GDN_DV:(h + 1) * GDN_DV] = og.astype(BF16)
                refs[d][8][rows, h * GLA_DV:(h + 1) * GLA_DV] = ol[d, h].astype(BF16)

    for d, h in insts:
        sg_ref[d, h] = s[d, h]
        sl_ref[d, h] = sl[d, h]


def _scan(u0, cq, akd, ggl, m, qd, lgl, nctx, cps):
    b, nchunks = u0.shape[:2]
    assert nctx % cps == 0 and nchunks % cps == 0
    nlat = nchunks - nctx
    ntot, nctx = nchunks // cps, nctx // cps

    def specs(d):
        def chunk(bi, s):
            return (bi, _ext_chunk(d, s, nctx, ntot))

        def big(*tail, heads=GDN_HEADS):
            return pl.BlockSpec((None, cps, None, heads) + tail, lambda bi, s: chunk(bi, s) + (d, 0, 0, 0))

        def small(*tail):
            return pl.BlockSpec((None, cps) + tail, lambda bi, s: chunk(bi, s) + (0, 0))

        half = GDN_HEADS // 2
        return [big(CHUNK, GDN_DV), big(2 * CHUNK, GDN_DK), big(CHUNK + GDN_DK, 2 * CHUNK, heads=half),
                small(SUBLANE, LANE), big(GLA_DK, GLA_DV), big(CHUNK, 2 * GLA_DK, heads=half), small(GLA_DK, LANE)]

    def out_spec(d, width):
        return pl.BlockSpec((None, cps * CHUNK, width), lambda bi, s: (bi, _lat_chunk(d, s, nctx, ntot), 0))

    sds = jax.ShapeDtypeStruct
    return pl.pallas_call(
        functools.partial(_scan_kernel, nctx=nctx),
        grid=(b, ntot),
        in_specs=specs(0) + specs(1),
        out_specs=[out_spec(0, GDN_V), out_spec(1, GDN_V), out_spec(0, GLA_V), out_spec(1, GLA_V)],
        out_shape=[sds((b, nlat * CHUNK, GDN_V), BF16)] * 2 + [sds((b, nlat * CHUNK, GLA_V), BF16)] * 2,
        scratch_shapes=[pltpu.VMEM((2, GDN_HEADS, GDN_DK, GDN_DV), F32),
                        pltpu.VMEM((2, GLA_HEADS, GLA_DK, GLA_DV), F32)],
        compiler_params=_params("parallel", "arbitrary"),
        name="scan",
    )(u0, cq, akd, ggl, m, qd, lgl, u0, cq, akd, ggl, m, qd, lgl)


def _gla_pre_kernel(q_ref, k_ref, v_ref, la_ref, m_ref, qd_ref, gl_ref, intra_ref, *, cpb):
    tri = [_tri_masks(d) for d in range(2)]
    for c in range(cpb):
        rows = slice(c * CHUNK, (c + 1) * CHUNK)
        q = q_ref[rows, :] * (GLA_DK ** -0.5)
        k = k_ref[rows, :]
        v = v_ref[rows, :]
        la = la_ref[rows, :]
        pref = _cumsum_rows(tri[0][0], la)
        tot = jnp.sum(la, axis=0, keepdims=True)
        lanes = lax.broadcasted_iota(jnp.int32, la.shape, 1)
        bwd_lane = lanes >= GLA_QK
        bc = jnp.where(bwd_lane, tot - pref + la, pref)
        mid = CHUNK // 2
        ref = jnp.where(bwd_lane[0:1, :], bc[CHUNK - 1 - mid:CHUNK - mid, :], bc[mid:mid + 1, :])
        q2 = jnp.concatenate([q, q], axis=1)
        k2 = jnp.concatenate([k, k], axis=1)
        qa = q2 * jnp.exp(bc - ref)
        ka = k2 * jnp.exp(ref - bc)
        qd = (q2 * jnp.exp(bc)).astype(BF16)
        kdt = (k2 * jnp.exp(tot - bc)).T
        glcol = jnp.exp(jnp.sum(la.T, axis=1, keepdims=True))
        out_lanes = lax.broadcasted_iota(jnp.int32, (GLA_DK, LANE), 1)
        gl = jnp.zeros((GLA_DK, LANE), F32)
        for h in range(GLA_HEADS):
            vh = v[:, h * GLA_DV:(h + 1) * GLA_DV]
            intra = None
            for d in range(2):
                cs = slice(d * GLA_QK + h * GLA_DK, d * GLA_QK + (h + 1) * GLA_DK)
                attn = jnp.where(tri[d][0], _dot_nt(qa[:, cs], ka[:, cs]), 0.0)
                im = _dot(jnp.concatenate([attn, kdt[cs, :]], axis=0), vh)
                intra = im[0:CHUNK] if intra is None else intra + im[0:CHUNK]
                m_ref[c, d, h] = im[CHUNK:CHUNK + GLA_DK].astype(BF16)
                gl = jnp.where(out_lanes == d * GLA_HEADS + h, glcol[cs, :], gl)
            intra_ref[rows, h * GLA_DV:(h + 1) * GLA_DV] = intra.astype(BF16)
        for d in range(2):
            for p in range(GLA_HEADS // 2):
                qd_ref[c, d, p] = qd[:, d * GLA_QK + 2 * p * GLA_DK:d * GLA_QK + 2 * (p + 1) * GLA_DK]
        gl_ref[c] = gl


def _gla_pre(rest, loga, cpb):
    b, lext, _ = rest.shape
    ntot = lext // CHUNK
    rows = cpb * CHUNK
    lead = (b, ntot, 2, GLA_HEADS)

    def col(colblk):
        return lambda bi, i: (bi, i, colblk)

    def blk(*tail, heads=GLA_HEADS):
        return pl.BlockSpec((None, cpb, 2, heads) + tail, lambda bi, i: (bi, i, 0, 0, 0, 0))

    sds = jax.ShapeDtypeStruct
    return pl.pallas_call(
        functools.partial(_gla_pre_kernel, cpb=cpb),
        grid=(b, ntot // cpb),
        in_specs=[pl.BlockSpec((None, rows, GLA_QK), col((COL_LQ - COL_GZ) // GLA_QK)),
                  pl.BlockSpec((None, rows, GLA_QK), col((COL_LK - COL_GZ) // GLA_QK)),
                  pl.BlockSpec((None, rows, GLA_V), col((COL_LV - COL_GZ) // GLA_V)),
                  pl.BlockSpec((None, rows, 2 * GLA_QK), col(0))],
        out_specs=[blk(GLA_DK, GLA_DV), blk(CHUNK, 2 * GLA_DK, heads=GLA_HEADS // 2),
                   pl.BlockSpec((None, cpb, GLA_DK, LANE), lambda bi, i: (bi, i, 0, 0)),
                   pl.BlockSpec((None, rows, GLA_V), col(0))],
        out_shape=[sds(lead + (GLA_DK, GLA_DV), BF16), sds((b, ntot, 2, GLA_HEADS // 2, CHUNK, 2 * GLA_DK), BF16),
                   sds((b, ntot, GLA_DK, LANE), F32), sds((b, lext, GLA_V), BF16)],
        compiler_params=_params("parallel", "parallel"),
        name="gla_pre",
    )(rest, rest, rest, loga)


def _head_gate(o, gain, z, heads, dv):
    parts = []
    for h in range(heads):
        sl = slice(h * dv, (h + 1) * dv)
        oh = o[:, sl]
        parts.append(oh * lax.rsqrt(jnp.mean(oh * oh, axis=-1, keepdims=True) + EPS) * gain * _silu(z[:, sl]))
    return jnp.concatenate(parts, axis=-1)


def _ab_mix(refs, w_ref):
    gf_ref, gb_ref, lf_ref, lb_ref, li_ref, gz_ref, lg_ref, gn_ref, ln_ref = refs
    og = gf_ref[...].astype(F32) + gb_ref[...].astype(F32)
    ol = lf_ref[...].astype(F32) + lb_ref[...].astype(F32) + li_ref[0].astype(F32)
    yg = _head_gate(og, gn_ref[...], gz_ref[0], GDN_HEADS, GDN_DV)
    yl = _head_gate(ol, ln_ref[...], lg_ref[0], GLA_HEADS, GLA_DV)
    return (jnp.dot(yg.astype(BF16), w_ref[0:GDN_V, :], preferred_element_type=F32)
            + jnp.dot(yl.astype(BF16), w_ref[GDN_V:GDN_V + GLA_V, :], preferred_element_type=F32))


def _hy_mix(refs, w_ref):
    return jnp.dot(refs[0][...].astype(BF16), w_ref[...], preferred_element_type=F32)


def _tail_kernel(*refs, n_mix, mix_fn, nsplit, final_norm):
    mix_refs = refs[:n_mix]
    wo_ref, h_ref, g1_ref, gn_ref, sh_ref, sc_ref, g2_ref, w1_ref, w3_ref, w2_ref, fn_ref, o_ref = refs[n_mix:]
    h = h_ref[...] + g1_ref[...] * mix_fn(mix_refs, wo_ref)
    a = _modulated(h, gn_ref[...], sh_ref[...], sc_ref[...]).astype(BF16)
    fc = w1_ref.shape[1] // nsplit
    acc = jnp.zeros(h.shape, F32)
    for j in range(nsplit):
        u1 = jnp.dot(a, w1_ref[:, j * fc:(j + 1) * fc], preferred_element_type=F32)
        u3 = jnp.dot(a, w3_ref[:, j * fc:(j + 1) * fc], preferred_element_type=F32)
        acc = acc + jnp.dot((_silu(u1) * u3).astype(BF16), w2_ref[j * fc:(j + 1) * fc, :],
                            preferred_element_type=F32)
    y = h + g2_ref[...] * acc
    if final_norm:
        y = y * lax.rsqrt(jnp.mean(y * y, axis=-1, keepdims=True) + EPS) * fn_ref[...]
    o_ref[...] = y


def _tail(mix_args, mix_specs, mix_fn, w_out, h, g1, gain, shift, scale, g2, w1, w3, w2, final_w, tm, final_norm):
    b, l, d = h.shape
    dff = w1.shape[1]

    def tok(bi, i):
        return (bi, i, 0)

    def vec(bi, i):
        return (bi, 0, 0)

    def const(bi, i):
        return (0, 0)

    return pl.pallas_call(
        functools.partial(_tail_kernel, n_mix=len(mix_args), mix_fn=mix_fn, nsplit=2, final_norm=final_norm),
        grid=(b, l // tm),
        in_specs=list(mix_specs) + [
            _resident(w_out.shape, const), pl.BlockSpec((None, tm, d), tok), pl.BlockSpec((None, 1, d), vec),
            _resident((1, d), const), pl.BlockSpec((None, 1, d), vec), pl.BlockSpec((None, 1, d), vec),
            pl.BlockSpec((None, 1, d), vec), _resident((d, dff), const), _resident((d, dff), const),
            _resident((dff, d), const), _resident((1, d), const)],
        out_specs=pl.BlockSpec((None, tm, d), tok),
        out_shape=jax.ShapeDtypeStruct((b, l, d), F32),
        compiler_params=_params("parallel", "parallel"),
        name="mix_out_ffn",
    )(*mix_args, w_out, h, g1, gain.reshape(1, d), shift, scale, g2, w1, w3, w2, final_w.reshape(1, d))


def _hy_hidden_kernel(fb_ref, w1t_ref, w1c_ref, w1s_ref, b1_ref, w2_ref, b2_ref, w3_ref, b3_ref, fr_ref, o_ref,
                      *, tile, seq_len):
    half = tile // 2
    hid = o_ref.shape[1]
    lane_hi = lax.broadcasted_iota(jnp.int32, (half, 2 * hid), 1) >= hid
    rows = (pl.program_id(0) * tile + lax.broadcasted_iota(jnp.int32, (half, 2 * hid), 0)
            + jnp.where(lane_hi, half, 0))
    pos = jnp.where(rows < seq_len, rows, 2 * seq_len - rows).astype(F32)
    t = pos * (1.0 / (seq_len - 1))
    ang = (pos * (2.0 * math.pi / seq_len)) * fb_ref[...]
    fr = fr_ref[...]
    pre = t * w1t_ref[...] + _dot_hi(jnp.cos(ang), w1c_ref[...]) - _dot_hi(jnp.sin(ang), w1s_ref[...])
    hdn = jnp.sin(fr * (pre + b1_ref[...]))
    hdn = jnp.sin(fr * (_dot_hi(hdn, w2_ref[...]) + b2_ref[...]))
    hdn = jnp.sin(fr * (_dot_hi(hdn, w3_ref[...]) + b3_ref[...]))
    o_ref[0:half, :] = hdn[:, 0:hid]
    o_ref[half:tile, :] = hdn[:, hid:2 * hid]


def _hy_hidden(w1, b1, w2, b2, w3, b3, freq, seq_len, tile):
    hid = w1.shape[1]
    assert 2 * hid == LANE and HY_BANDS <= hid

    def twice(v):
        return jnp.tile(v.reshape(1, hid), (1, 2))

    def block_diag(w):
        k = w.shape[0]
        return jnp.zeros((2 * hid, 2 * hid), F32).at[:k, :hid].set(w).at[hid:hid + k, hid:].set(w)

    bands = jnp.linspace(1e-4, HY_BANDS - 1, HY_BANDS, dtype=F32)
    fb = jnp.zeros((1, 2 * hid), F32).at[0, :HY_BANDS].set(bands).at[0, hid:hid + HY_BANDS].set(bands)
    args = (fb, twice(w1[0]), block_diag(w1[1:1 + HY_BANDS]), block_diag(w1[1 + HY_BANDS:1 + 2 * HY_BANDS]),
            twice(b1), block_diag(w2), twice(b2), block_diag(w3), twice(b3), twice(freq))
    return pl.pallas_call(
        functools.partial(_hy_hidden_kernel, tile=tile, seq_len=seq_len),
        grid=(2 * seq_len // tile,),
        in_specs=[pl.BlockSpec(a.shape, lambda i: (0, 0)) for a in args],
        out_specs=pl.BlockSpec((tile, hid), lambda i: (i, 0)),
        out_shape=jax.ShapeDtypeStruct((2 * seq_len, hid), F32),
        compiler_params=_params("parallel"),
        name="hy_hidden",
    )(*args)


def _fft_matrices(r):
    n = r * r
    idx = jnp.arange(r, dtype=jnp.int32)

    def cis(m, period):
        ang = (2.0 * math.pi / period) * m.astype(F32)
        return jnp.cos(ang), -jnp.sin(ang)

    fr, fi = cis((idx[:, None] * idx[None, :]) % r, r)
    half = r // 2
    f1_data = jnp.block([[fr[:, :half], -fi[:, :half]], [fi[:, :half], fr[:, :half]]])
    f1_real = jnp.concatenate([fr, fi], axis=0)
    f3 = jnp.block([[fr[:half, :], fi[:half, :]], [-fi[:half, :], fr[:half, :]]]) * (1.0 / n)
    tr, ti = cis((idx[:, None] * idx[None, :]) % n, n)
    gr = tr[:, None, :] * fr[None, :, :] - ti[:, None, :] * fi[None, :, :]
    gi = tr[:, None, :] * fi[None, :, :] + ti[:, None, :] * fr[None, :, :]
    g = jnp.concatenate([jnp.concatenate([gr, -gi], axis=2), jnp.concatenate([gi, gr], axis=2)], axis=1)
    gh = jnp.swapaxes(g, 1, 2)
    return f1_data.astype(BF16), f1_real.astype(BF16), f3.astype(BF16), g.astype(BF16), gh.astype(BF16)


FFT_BG = 16
FFT_DT = 256


def _slab(ref, j):
    return jnp.concatenate([ref[i, :, j, :] for i in range(ref.shape[0])], axis=0)


def _store_interleaved(o_ref, vals):
    x = jnp.swapaxes(jnp.stack([v.astype(o_ref.dtype) for v in vals], axis=0), 0, 1)
    o_ref[...] = x.reshape(o_ref.shape)


def _time_block(arr, sel, n):
    lead = arr.ndim - 4
    return pl.BlockSpec((None,) * lead + (n, arr.shape[-3], FFT_BG, FFT_DT),
                        lambda p, g, dd: tuple(sel) + (p, 0, g, dd))


def _fft_stage1_kernel(f_ref, x_ref, o_ref):
    _store_interleaved(o_ref, [jnp.dot(f_ref[...], _slab(x_ref, j).astype(BF16), preferred_element_type=F32)
                               for j in range(x_ref.shape[2])])


def _fft_stage1(f1, x5, sel, n):
    r2, _ = f1.shape
    r = r2 // 2
    items, _, rb, d = x5.shape[-4:]
    groups = items // n
    return pl.pallas_call(
        _fft_stage1_kernel,
        grid=(groups, rb // FFT_BG, d // FFT_DT),
        in_specs=[_resident(f1.shape, lambda p, g, dd: (0, 0)), _time_block(x5, sel, n)],
        out_specs=pl.BlockSpec((None, 2, r, FFT_BG, FFT_DT), lambda p, g, dd: (p, 0, 0, g, dd)),
        out_shape=jax.ShapeDtypeStruct((groups, 2, r, rb, d), BF16),
        compiler_params=_params("parallel", "parallel", "parallel"),
        name="fft_stage1",
    )(f1, x5)


def _fft_filter_stage1_kernel(f_ref, hdn_ref, fo_ref, dl_ref, o_ref, *, seq_len):
    r, bg, _ = hdn_ref.shape
    orders = fo_ref.shape[1]
    half = r // 2
    a_idx = lax.broadcasted_iota(jnp.int32, (r, 1), 0)
    ps = [[] for _ in range(orders)]
    for j in range(bg):
        rows = a_idx * r + (pl.program_id(0) * bg + j)
        pos = jnp.where(rows < seq_len, rows, 2 * seq_len - rows).astype(F32)
        t = pos * (1.0 / (seq_len - 1))
        window = jnp.where(rows != seq_len, jnp.exp(-t * dl_ref[...]) + HY_SHIFT, 0.0)
        hj = hdn_ref[:, j, :]
        for n in range(orders):
            taps = jnp.concatenate([_dot(hj[:half], fo_ref[0, n]), _dot(hj[half:], fo_ref[1, n])], axis=0) * window
            ps[n].append(jnp.dot(f_ref[...], taps.astype(BF16), preferred_element_type=F32))
    for n in range(orders):
        _store_interleaved(o_ref.at[n], ps[n])


def _fft_filter_stage1(f1, hdn3, fo, deltas, seq_len):
    r2, r = f1.shape
    hid = hdn3.shape[-1]
    _, orders, _, d = fo.shape
    return pl.pallas_call(
        functools.partial(_fft_filter_stage1_kernel, seq_len=seq_len),
        grid=(r // FFT_BG, d // FFT_DT),
        in_specs=[_resident(f1.shape, lambda g, dd: (0, 0)),
                  pl.BlockSpec((r, FFT_BG, hid), lambda g, dd: (0, g, 0)),
                  pl.BlockSpec((2, orders, hid, FFT_DT), lambda g, dd: (0, 0, 0, dd)),
                  pl.BlockSpec((1, FFT_DT), lambda g, dd: (0, dd))],
        out_specs=pl.BlockSpec((orders, 2, r, FFT_BG, FFT_DT), lambda g, dd: (0, 0, 0, g, dd)),
        out_shape=jax.ShapeDtypeStruct((orders, 2, r, r, d), BF16),
        compiler_params=_params("parallel", "parallel"),
        name="fft_filter_stage1",
    )(f1, hdn3, fo, deltas)


FFT_KLO = 4


def _fft_stage2_kernel(g_ref, gh_ref, k_ref, p_ref, o_ref):
    r = p_ref.shape[3]
    for kk in range(p_ref.shape[2]):
        kq = k_ref[:, kk]
        h = jnp.dot(g_ref[kk], kq.reshape(2 * r, kq.shape[2]), preferred_element_type=F32)
        hr, hi = h[:r], h[r:]
        for n in range(p_ref.shape[0]):
            p = p_ref[n, :, kk]
            x = jnp.dot(g_ref[kk], p.reshape(2 * r, p.shape[2]), preferred_element_type=F32)
            xr, xi = x[:r], x[r:]
            y = jnp.concatenate([xr * hr - xi * hi, xr * hi + xi * hr], axis=0)
            q = jnp.dot(gh_ref[kk], y.astype(BF16), preferred_element_type=F32)
            o_ref[n, :, kk] = q.reshape(2, r, q.shape[1]).astype(o_ref.dtype)


def _fft_stage2(g, gh, pk, order, p5):
    pairs, _, r, _, d = p5.shape
    return pl.pallas_call(
        _fft_stage2_kernel,
        grid=(r // FFT_KLO,),
        in_specs=[pl.BlockSpec((FFT_KLO, 2 * r, 2 * r), lambda kk: (kk, 0, 0)),
                  pl.BlockSpec((FFT_KLO, 2 * r, 2 * r), lambda kk: (kk, 0, 0)),
                  pl.BlockSpec((None, 2, FFT_KLO, r, d), lambda kk: (order, 0, kk, 0, 0)),
                  pl.BlockSpec((pairs, 2, FFT_KLO, r, d), lambda kk: (0, 0, kk, 0, 0))],
        out_specs=pl.BlockSpec((pairs, 2, FFT_KLO, r, d), lambda kk: (0, 0, kk, 0, 0)),
        out_shape=jax.ShapeDtypeStruct(p5.shape, BF16),
        compiler_params=_params("parallel"),
        name="fft_stage2",
    )(g, gh, pk, p5)


def _rows_by_b(ref):
    bg = ref.shape[2]
    if ref.dtype == F32:
        return [_slab(ref, j) for j in range(bg)]
    x = ref[...]
    x = jnp.swapaxes(x.reshape(x.shape[0] * x.shape[1], bg, x.shape[3]), 0, 1)
    return [x[j] for j in range(bg)]


def _fft_stage3_kernel(f3_ref, f1_ref, q_ref, z_ref, gate_ref, skip_ref, zo_ref, po_ref=None):
    q, z, gate = _rows_by_b(q_ref), _rows_by_b(z_ref), _rows_by_b(gate_ref)
    skip = skip_ref[...]
    zs, ps = [], []
    for j in range(len(q)):
        y = jnp.dot(f3_ref[...], q[j], preferred_element_type=F32)
        zn = gate[j] * (y + z[j] * skip)
        zs.append(zn)
        if po_ref is not None:
            ps.append(jnp.dot(f1_ref[...], zn.astype(BF16), preferred_element_type=F32))
    _store_interleaved(zo_ref, zs)
    if po_ref is not None:
        _store_interleaved(po_ref, ps)


def _fft_stage3(f3, f1, q5, z5, z_sel, g5, g_sel, skip, chain):
    pairs, _, r, rb, d = q5.shape
    a = z5.shape[-3]
    sds = jax.ShapeDtypeStruct
    out_specs = [pl.BlockSpec((2, a, FFT_BG, FFT_DT), lambda p, g, dd: (p, 0, g, dd))]
    out_shape = [sds((2 * pairs, a, rb, d), BF16)]
    if chain:
        out_specs.append(pl.BlockSpec((None, 2, r, FFT_BG, FFT_DT), lambda p, g, dd: (p, 0, 0, g, dd)))
        out_shape.append(sds(q5.shape, BF16))
    return pl.pallas_call(
        _fft_stage3_kernel,
        grid=(pairs, rb // FFT_BG, d // FFT_DT),
        in_specs=[_resident(f3.shape, lambda p, g, dd: (0, 0)), _resident(f1.shape, lambda p, g, dd: (0, 0)),
                  pl.BlockSpec((None, 2, r, FFT_BG, FFT_DT), lambda p, g, dd: (p, 0, 0, g, dd)),
                  _time_block(z5, z_sel, 2), _time_block(g5, g_sel, 2),
                  pl.BlockSpec((1, FFT_DT), lambda p, g, dd: (0, dd))],
        out_specs=out_specs,
        out_shape=out_shape,
        compiler_params=_params("parallel", "parallel", "parallel"),
        name="fft_stage3",
    )(f3, f1, q5, z5, g5, skip)


TAIL_ROWS = 512
HY_FRONT_ROWS = 512
AB_FRONT_ROWS = 384


def _regroup_ab_weights(w_in):
    d = w_in.shape[0]
    sizes = (GDN_QK, GDN_QK, GDN_V, GDN_V, 2 * GDN_HEADS, 2 * GDN_HEADS, GLA_QK, GLA_QK, GLA_V, GLA_V,
             2 * GLA_GATE_RANK)
    offs = [0]
    for s in sizes:
        offs.append(offs[-1] + s)
    gq, gk, gv, gz, ga, gb, lq, lk, lv, lg, llr = (w_in[:, offs[i]:offs[i + 1]] for i in range(len(sizes)))
    pad = jnp.zeros((d, LANE - (ga.shape[1] + gb.shape[1] + llr.shape[1])), w_in.dtype)
    return jnp.concatenate([gq, gk, gv, gz, lq, lk, lv, lg, ga, gb, llr, pad], axis=1)


def _lane_row(vals, offset):
    flat = vals.reshape(-1).astype(F32)
    return jnp.zeros((1, LANE), F32).at[0, offset:offset + flat.shape[0]].set(flat)


def _even_layer(h, ctx, mods, norm1, norm2, w_in, conv_w, a_log, dt_bias, gdn_norm, gate_w, gate_b, gla_norm,
                w_out, w1, w3, w2, final_w, final_norm):
    b, l, d = h.shape
    lctx = ctx.shape[1]
    tm = TAIL_ROWS
    sh1, sc1, g1, sh2, sc2, g2 = mods["lat"]
    csh1, csc1 = mods["ctx"][0], mods["ctx"][1]
    xe = jnp.concatenate([ctx, h], axis=1)
    shift = jnp.stack([jnp.broadcast_to(csh1, (b, d)), sh1], axis=1)[:, :, None, :]
    scale = jnp.stack([jnp.broadcast_to(csc1, (b, d)), sc1], axis=1)[:, :, None, :]
    gw = jnp.zeros((LANE, 2 * GLA_QK), F32)
    for dd in range(2):
        gw = gw.at[SM_LR + dd * GLA_GATE_RANK:SM_LR + (dd + 1) * GLA_GATE_RANK,
                   dd * GLA_QK:(dd + 1) * GLA_QK].set(gate_w[dd])
    q, k, v, sm, loga, rest = _ab_front(xe, norm1, shift, scale, _regroup_ab_weights(w_in).astype(BF16), conv_w,
                                        _lane_row(a_log, SM_G), _lane_row(dt_bias, SM_G), gw,
                                        gate_b.reshape(1, 2 * GLA_QK), AB_FRONT_ROWS, lctx)
    nctx = lctx // CHUNK
    u0, cq, akd, ggl = _gdn_pre(q, k, v, sm, 2)
    m, qd, lgl, ol_intra = _gla_pre(rest, loga, 2)
    og_f, og_b, ol_f, ol_b = _scan(u0, cq, akd, ggl, m, qd, lgl, nctx, 4)
    def lat(width):
        return pl.BlockSpec((None, tm, width), lambda bi, i: (bi, i, 0))

    def ext(width, colblk):
        return pl.BlockSpec((pl.Element(1), pl.Element(tm), pl.Element(width)),
                            lambda bi, i: (bi, SUBLANE * (lctx // SUBLANE + i * (tm // SUBLANE)), colblk * width))

    mix_args = (og_f, og_b, ol_f, ol_b, ol_intra, rest, rest, gdn_norm.reshape(1, GDN_DV), gla_norm.reshape(1, GLA_DV))
    mix_specs = [lat(GDN_V), lat(GDN_V), lat(GLA_V), lat(GLA_V), ext(GLA_V, 0), ext(GDN_V, 0),
                 ext(GLA_V, (COL_LG - COL_GZ) // GLA_V), _resident((1, GDN_DV), lambda bi, i: (0, 0)),
                 _resident((1, GLA_DV), lambda bi, i: (0, 0))]
    return _tail(mix_args, mix_specs, _ab_mix, w_out.astype(BF16), h, g1[:, None, :], norm2, sh2[:, None, :],
                 sc2[:, None, :], g2[:, None, :], w1.astype(BF16), w3.astype(BF16), w2.astype(BF16), final_w, tm,
                 final_norm)


def _hyena_layer(h, mods, norm1, norm2, w_in, conv_w, pw1, pb1, pw2, pb2, pw3, pb3, freq, filt_out, skip, w_out,
                 w1, w3, w2, final_w, final_norm):
    b, l, d = h.shape
    tm = TAIL_ROWS
    sh1, sc1, g1, sh2, sc2, g2 = mods["lat"]
    u3 = _hy_front(h, norm1, sh1[:, None, :], sc1[:, None, :], w_in.astype(BF16), conv_w, HY_FRONT_ROWS)

    n2 = 2 * l
    r = math.isqrt(n2)
    assert r * r == n2 and b % 2 == 0
    orders = skip.shape[0]
    hdn = _hy_hidden(pw1, pb1, pw2, pb2, pw3, pb3, freq, l, 512)
    fo = jnp.transpose(filt_out.reshape(filt_out.shape[0], 2, orders, d), (1, 2, 0, 3))
    deltas = jnp.abs(jnp.linspace(HY_MIN_DECAY, HY_MAX_DECAY, d, dtype=F32))[None, :]

    f1_data, f1_real, f3, g, gh = _fft_matrices(r)
    pk = _fft_filter_stage1(f1_real, hdn.reshape(r, r, hdn.shape[1]), fo, deltas, l)

    u5 = u3.reshape(3, b, r // 2, r, d)
    z_arr, z_sel = u5, (0,)
    p = _fft_stage1(f1_data, z_arr, z_sel, 2)
    for n in range(orders):
        q = _fft_stage2(g, gh, pk, n, p)
        chain = n + 1 < orders
        res = _fft_stage3(f3, f1_data, q, z_arr, z_sel, u5, (n + 1,), skip[n][None, :], chain)
        z_arr, z_sel = res[0], ()
        if chain:
            p = res[1]
    mix_specs = [pl.BlockSpec((None, tm, d), lambda bi, i: (bi, i, 0))]
    return _tail((z_arr.reshape(b, l, d),), mix_specs, _hy_mix, w_out.astype(BF16), h, g1[:, None, :], norm2,
                 sh2[:, None, :], sc2[:, None, :], g2[:, None, :], w1.astype(BF16), w3.astype(BF16), w2.astype(BF16),
                 final_w, tm, final_norm)


def kernel(x, c, ctx, c_ctx, mod_w, mod_b, norm1_w, norm2_w, ab_w_in, ab_conv_w, gdn_a_log, gdn_dt_bias, gdn_norm_w, gla_gate_w, gla_gate_b, gla_norm_w, ab_w_out, hy_w_in, hy_conv_w, hy_pos_w1, hy_pos_b1, hy_pos_w2, hy_pos_b2, hy_pos_w3, hy_pos_b3, hy_freq, hy_filt_out, hy_skip, hy_w_out, ffn_w1, ffn_w3, ffn_w2, final_norm_w):
    b, l, d = x.shape
    depth = mod_w.shape[0]
    assert b + 1 <= SUBLANE
    cvec = jnp.concatenate([c, c_ctx[None, :], jnp.zeros((SUBLANE - b - 1, d), F32)], axis=0)
    mod = _modulation(cvec, mod_w, mod_b)
    h = x
    for i in range(depth):
        mods = {"lat": [mod[i, :b, j * d:(j + 1) * d] for j in range(6)],
                "ctx": [mod[i, b, j * d:(j + 1) * d] for j in range(6)]}
        last = i == depth - 1
        if i % 2 == 0:
            assert not any(j % 2 == 0 for j in range(i + 1, depth))
            e = i // 2
            h = _even_layer(h, ctx, mods, norm1_w[i], norm2_w[i], ab_w_in[e], ab_conv_w[e], gdn_a_log[e],
                            gdn_dt_bias[e], gdn_norm_w[e], gla_gate_w[e], gla_gate_b[e], gla_norm_w[e], ab_w_out[e],
                            ffn_w1[i], ffn_w3[i], ffn_w2[i], final_norm_w, last)
        else:
            o = i // 2
            h = _hyena_layer(h, mods, norm1_w[i], norm2_w[i], hy_w_in[o], hy_conv_w[o], hy_pos_w1[o], hy_pos_b1[o],
                             hy_pos_w2[o], hy_pos_b2[o], hy_pos_w3[o], hy_pos_b3[o], hy_freq[o], hy_filt_out[o],
                             hy_skip[o], hy_w_out[o], ffn_w1[i], ffn_w3[i], ffn_w2[i], final_norm_w, last)
    return h
```

```python
import functools
import math

import jax
import jax.numpy as jnp
from jax import lax
from jax.experimental import pallas as pl
from jax.experimental.pallas import tpu as pltpu

F32 = jnp.float32
BF16 = jnp.bfloat16
HI = lax.Precision.HIGHEST

EPS = 1e-6
CHUNK = 64
GDN_HEADS, GDN_DK, GDN_DV = 4, 128, 128
GLA_HEADS, GLA_DK, GLA_DV = 4, 64, 128
GLA_GATE_RANK = 16
GLA_GATE_TAU = 16.0
GDN_QK = GDN_HEADS * GDN_DK
GDN_V = GDN_HEADS * GDN_DV
GLA_QK = GLA_HEADS * GLA_DK
GLA_V = GLA_HEADS * GLA_DV
HY_EMB = 33
HY_BANDS = (HY_EMB - 1) // 2
HY_MIN_DECAY = math.log(1e-2) / 1.5
HY_MAX_DECAY = math.log(1e-2) / 0.3
HY_SHIFT = 0.05

LANE = 128
SUBLANE = 8
VMEM_LIMIT_BYTES = 56 * 1024 * 1024

COL_GQ, COL_GK, COL_GV, COL_GZ = 0, 512, 1024, 1536
COL_LQ, COL_LK, COL_LV, COL_LG = 2048, 2304, 2560, 3072
COL_SMALL = 3584
SM_G, SM_B, SM_LR = 0, 8, 16


def _params(*sem):
    return pltpu.CompilerParams(dimension_semantics=sem, vmem_limit_bytes=VMEM_LIMIT_BYTES)


def _resident(shape, index_map):
    return pl.BlockSpec(shape, index_map, pipeline_mode=pl.Buffered(1))


def _dot(a, b):
    return jnp.dot(a.astype(BF16), b.astype(BF16), preferred_element_type=F32)


def _dot_nt(a, b):
    return lax.dot_general(a.astype(BF16), b.astype(BF16), (((1,), (1,)), ((), ())),
                           preferred_element_type=F32)


def _dot_hi(a, b):
    return jnp.dot(a, b, preferred_element_type=F32, precision=HI)


def _sigmoid(x):
    return 1.0 / (1.0 + jnp.exp(-x))


def _silu(x):
    return x * _sigmoid(x)


def _softplus(x):
    return jnp.maximum(x, 0.0) + jnp.log(1.0 + jnp.exp(-jnp.abs(x)))


def _mod_kernel(c_ref, w_ref, b_ref, o_ref):
    o_ref[...] = _dot_hi(_silu(c_ref[...]), w_ref[...]) + b_ref[...]


def _modulation(cvec, mod_w, mod_b):
    depth, d, n = mod_w.shape
    tn = n // 4
    return pl.pallas_call(
        _mod_kernel,
        grid=(depth, n // tn),
        in_specs=[pl.BlockSpec((SUBLANE, d), lambda i, j: (0, 0)),
                  pl.BlockSpec((None, d, tn), lambda i, j: (i, 0, j)),
                  pl.BlockSpec((None, 1, tn), lambda i, j: (i, 0, j))],
        out_specs=pl.BlockSpec((None, SUBLANE, tn), lambda i, j: (i, 0, j)),
        out_shape=jax.ShapeDtypeStruct((depth, SUBLANE, n), F32),
        compiler_params=_params("parallel", "parallel"),
        name="modulation",
    )(cvec, mod_w, mod_b.reshape(depth, 1, n))


CONV_HALO = SUBLANE


def _modulated(x, gain, shift, scale):
    ms = jnp.mean(x * x, axis=-1, keepdims=True)
    return x * lax.rsqrt(ms + EPS) * gain * (1.0 + scale) + shift


def _window_start(i, tm, win, seq_len):
    return SUBLANE * jnp.clip(i * (tm // SUBLANE) - CONV_HALO // SUBLANE, 0, (seq_len - win) // SUBLANE)


def _window_spec(tm, win, seq_len, d):
    return pl.BlockSpec((pl.Element(1), pl.Element(win), pl.Element(d)),
                        lambda bi, i: (bi, _window_start(i, tm, win, seq_len), 0))


def _conv3_rows(u, w, gidx=None, starts=(), ends=()):
    rows = u.shape[0]
    prev = pltpu.roll(u, 1, 0)
    nxt = pltpu.roll(u, rows - 1, 0)
    if starts:
        prev = jnp.where(functools.reduce(jnp.logical_or, [gidx == r for r in starts]), 0.0, prev)
    if ends:
        nxt = jnp.where(functools.reduce(jnp.logical_or, [gidx == r for r in ends]), 0.0, nxt)
    return prev * w[0:1, :] + u * w[1:2, :] + nxt * w[2:3, :]


def _hy_front_kernel(x_ref, g_ref, sh_ref, sc_ref, w_ref, cw_ref, o_ref, scr, *, tm, seq_len):
    i = pl.program_id(1)
    win = x_ref.shape[1]
    ws = _window_start(i, tm, win, seq_len)
    off = pl.multiple_of(i * tm - ws, SUBLANE)
    a = _modulated(x_ref[0], g_ref[...], sh_ref[...], sc_ref[...])
    u = jnp.dot(a.astype(BF16), w_ref[...], preferred_element_type=F32)
    cw = cw_ref[...]
    scr[...] = _conv3_rows(u, cw)

    @pl.when(i == 0)
    def _():
        scr[0:1, :] = scr[0:1, :] - u[win - 1:win, :] * cw[0:1, :]

    @pl.when(i == pl.num_programs(1) - 1)
    def _():
        scr[win - 1:win, :] = scr[win - 1:win, :] - u[0:1, :] * cw[2:3, :]

    d = o_ref.shape[-1]
    for g in range(o_ref.shape[0]):
        o_ref[g] = scr[pl.ds(off, tm), g * d:(g + 1) * d]


def _hy_front(x, gain, shift, scale, w, conv_w, tm):
    b, l, d = x.shape
    n = w.shape[1]
    groups = n // d
    win = tm + 2 * CONV_HALO

    def vec(bi, i):
        return (bi, 0, 0)

    return pl.pallas_call(
        functools.partial(_hy_front_kernel, tm=tm, seq_len=l),
        grid=(b, l // tm),
        in_specs=[_window_spec(tm, win, l, d), _resident((1, d), lambda bi, i: (0, 0)),
                  pl.BlockSpec((None, 1, d), vec), pl.BlockSpec((None, 1, d), vec),
                  _resident((d, n), lambda bi, i: (0, 0)), _resident((3, n), lambda bi, i: (0, 0))],
        out_specs=pl.BlockSpec((groups, None, tm, d), lambda bi, i: (0, bi, i, 0)),
        out_shape=jax.ShapeDtypeStruct((groups, b, l, d), F32),
        scratch_shapes=[pltpu.VMEM((win, n), F32)],
        compiler_params=_params("parallel", "parallel"),
        name="hy_front",
    )(x, gain.reshape(1, d), shift, scale, w, conv_w)


def _ab_front_kernel(x_ref, g_ref, sh_ref, sc_ref, w_ref, cw_ref, alog_ref, dtb_ref, gw_ref, gb_ref,
                     q_ref, k_ref, v_ref, so_ref, la_ref, rest_ref, scr, *, tm, lctx, lext):
    i = pl.program_id(1)
    win = x_ref.shape[1]
    ws = _window_start(i, tm, win, lext)
    off = pl.multiple_of(i * tm - ws, SUBLANE)
    gidx = ws + lax.broadcasted_iota(jnp.int32, (win, 1), 0)
    is_ctx = gidx < lctx
    shift = jnp.where(is_ctx, sh_ref[0], sh_ref[1])
    scale = jnp.where(is_ctx, sc_ref[0], sc_ref[1])
    a = _modulated(x_ref[0], g_ref[...], shift, scale)
    u = jnp.dot(a.astype(BF16), w_ref[...], preferred_element_type=F32)
    ncv = cw_ref.shape[1]
    scr[:, 0:ncv] = _conv3_rows(u[:, 0:ncv], cw_ref[...], gidx, (0, lctx), (lctx - 1, lext - 1))
    scr[:, ncv:] = u[:, ncv:]
    rows = pl.ds(off, tm)
    for h in range(GDN_HEADS):
        sl = slice(h * GDN_DK, (h + 1) * GDN_DK)
        qh = _silu(scr[rows, COL_GQ + h * GDN_DK:COL_GQ + (h + 1) * GDN_DK])
        kh = _silu(scr[rows, COL_GK + h * GDN_DK:COL_GK + (h + 1) * GDN_DK])
        q_ref[:, sl] = qh * lax.rsqrt(jnp.sum(qh * qh, axis=-1, keepdims=True) + EPS) * (GDN_DK ** -0.5)
        k_ref[:, sl] = kh * lax.rsqrt(jnp.sum(kh * kh, axis=-1, keepdims=True) + EPS)
    v_ref[...] = _silu(scr[rows, COL_GV:COL_GV + GDN_V])
    rest_ref[...] = scr[rows, COL_GZ:COL_SMALL]
    sm = scr[rows, COL_SMALL:COL_SMALL + LANE]
    g = -jnp.exp(alog_ref[...]) * _softplus(sm + dtb_ref[...])
    beta = _sigmoid(sm)
    lanes = lax.broadcasted_iota(jnp.int32, sm.shape, 1)
    so_ref[...] = jnp.where(lanes < SM_B, g, beta)
    gate = _dot(sm, gw_ref[...]) + gb_ref[...]
    la_ref[...] = (jnp.minimum(gate, 0.0) - jnp.log(1.0 + jnp.exp(-jnp.abs(gate)))) * (1.0 / GLA_GATE_TAU)


def _ab_front(xe, gain, shift, scale, w, conv_w, alog_row, dtb_row, gate_w, gate_b, tm, lctx):
    b, lext, d = xe.shape
    n = w.shape[1]
    win = tm + 2 * CONV_HALO
    nrest = COL_SMALL - COL_GZ

    def const(bi, i):
        return (0, 0)

    def tok(width):
        return pl.BlockSpec((None, tm, width), lambda bi, i: (bi, i, 0))

    sds = jax.ShapeDtypeStruct
    return pl.pallas_call(
        functools.partial(_ab_front_kernel, tm=tm, lctx=lctx, lext=lext),
        grid=(b, lext // tm),
        in_specs=[_window_spec(tm, win, lext, d), _resident((1, d), const),
                  pl.BlockSpec((None, 2, 1, d), lambda bi, i: (bi, 0, 0, 0)),
                  pl.BlockSpec((None, 2, 1, d), lambda bi, i: (bi, 0, 0, 0)),
                  _resident((d, n), const), _resident(conv_w.shape, const),
                  _resident((1, LANE), const), _resident((1, LANE), const),
                  _resident((LANE, 2 * GLA_QK), const), _resident((1, 2 * GLA_QK), const)],
        out_specs=[tok(GDN_QK), tok(GDN_QK), tok(GDN_V), tok(LANE), tok(2 * GLA_QK), tok(nrest)],
        out_shape=[sds((b, lext, GDN_QK), F32), sds((b, lext, GDN_QK), F32), sds((b, lext, GDN_V), F32),
                   sds((b, lext, LANE), F32), sds((b, lext, 2 * GLA_QK), F32), sds((b, lext, nrest), F32)],
        scratch_shapes=[pltpu.VMEM((win, n), F32)],
        compiler_params=_params("parallel", "parallel"),
        name="ab_front",
    )(xe, gain.reshape(1, d), shift, scale, w, conv_w, alog_row, dtb_row, gate_w, gate_b)


def _ext_chunk(d, s, nctx, ntot):
    if d == 0:
        return s
    return jnp.where(s < nctx, nctx - 1 - s, ntot + nctx - 1 - s)


def _lat_chunk(d, s, nctx, ntot):
    nlat = ntot - nctx
    if d == 0:
        return jnp.maximum(s - nctx, 0)
    return jnp.minimum(ntot - 1 - s, nlat - 1)


def _tri_masks(d):
    ii = lax.broadcasted_iota(jnp.int32, (CHUNK, CHUNK), 0)
    jj = lax.broadcasted_iota(jnp.int32, (CHUNK, CHUNK), 1)
    if d == 0:
        return ii >= jj, ii > jj, ii == jj
    return ii <= jj, ii < jj, ii == jj


def _cumsum_rows(incl, x):
    m = incl.astype(BF16)
    x1 = x.astype(BF16)
    r1 = x - x1.astype(F32)
    x2 = r1.astype(BF16)
    x3 = (r1 - x2.astype(F32)).astype(BF16)
    dot = functools.partial(jnp.dot, preferred_element_type=F32)
    return dot(m, x1) + dot(m, x2) + dot(m, x3)


def _merge_masks(d):
    ii = lax.broadcasted_iota(jnp.int32, (CHUNK, CHUNK), 0)
    jj = lax.broadcasted_iota(jnp.int32, (CHUNK, CHUNK), 1)
    lo, hi = (jj, ii) if d == 0 else (ii, jj)
    masks = {}
    s = 1
    while s < CHUNK:
        masks[s] = jnp.logical_and(ii // (2 * s) == jj // (2 * s),
                                   jnp.logical_and((hi // s) % 2 == 1, (lo // s) % 2 == 0))
        s *= 2
    return masks


def _gdn_pre_kernel(q_ref, k_ref, v_ref, sm_ref, u0_ref, cq_ref, akd_ref, gl_ref, *, cpb):
    tri = [_tri_masks(d) for d in range(2)]
    merge = [_merge_masks(d) for d in range(2)]
    insts = [(d, h) for d in range(2) for h in range(GDN_HEADS)]
    for c in range(cpb):
        rows = slice(c * CHUNK, (c + 1) * CHUNK)
        q = q_ref[rows, :]
        k = k_ref[rows, :]
        v = v_ref[rows, :]
        sm = sm_ref[rows, :]
        pref = _cumsum_rows(tri[0][0], sm)
        tot = jnp.sum(sm, axis=0, keepdims=True)
        lanes = lax.broadcasted_iota(jnp.int32, sm.shape, 1)
        bwd_lane = jnp.logical_and(lanes >= SM_G + GDN_HEADS, lanes < SM_G + 2 * GDN_HEADS)
        gc = jnp.where(bwd_lane, tot - pref + sm, pref)
        gct = gc.T
        totc = jnp.sum(sm.T, axis=1, keepdims=True)
        gl_ref[c] = jnp.exp(jnp.broadcast_to(totc[0:SUBLANE, :], (SUBLANE, LANE)))
        egc = jnp.exp(gc)
        etg = jnp.exp(tot - gc)
        kh = [k[:, h * GDN_DK:(h + 1) * GDN_DK] for h in range(GDN_HEADS)]
        qh = [q[:, h * GDN_DK:(h + 1) * GDN_DK] for h in range(GDN_HEADS)]
        vh = [v[:, h * GDN_DV:(h + 1) * GDN_DV] for h in range(GDN_HEADS)]
        gq = [_dot_nt(jnp.concatenate([kh[h], qh[h]], axis=0), kh[h]) for h in range(GDN_HEADS)]
        a, t, beta, ecol = {}, {}, {}, {}
        for d, h in insts:
            incl, strict, diag = tri[d]
            col = SM_G + d * GDN_HEADS + h
            decay = jnp.exp(jnp.where(incl, gc[:, col:col + 1] - gct[col:col + 1, :], -jnp.inf))
            beta[d, h] = sm[:, SM_B + col:SM_B + col + 1]
            ecol[d, h] = egc[:, col:col + 1]
            a[d, h] = jnp.where(strict, gq[h][0:CHUNK] * decay, 0.0) * beta[d, h]
            hl = slice((h % 2) * CHUNK, (h % 2 + 1) * CHUNK)
            akd_ref[c, d, h // 2, 0:CHUNK, hl] = (gq[h][CHUNK:2 * CHUNK] * decay).astype(BF16)
            akd_ref[c, d, h // 2, CHUNK:CHUNK + GDN_DK, hl] = (kh[h] * etg[:, col:col + 1]).T.astype(BF16)
            t[d, h] = diag.astype(F32) - jnp.where(merge[d][1], a[d, h], 0.0)
        s = 2
        while s < CHUNK:
            x = {i: _dot(jnp.where(merge[i[0]][s], a[i], 0.0), t[i]) for i in insts}
            t = {i: t[i] - _dot(t[i], x[i]) for i in insts}
            s *= 2
        for d, h in insts:
            rhs = jnp.concatenate([vh[h] * beta[d, h], kh[h] * (beta[d, h] * ecol[d, h])], axis=1)
            sol = _dot(t[d, h], rhs)
            u0_ref[c, d, h] = sol[:, 0:GDN_DV].astype(BF16)
            cq_ref[c, d, h, 0:CHUNK, :] = sol[:, GDN_DV:GDN_DV + GDN_DK].astype(BF16)
            cq_ref[c, d, h, CHUNK:2 * CHUNK, :] = (qh[h] * ecol[d, h]).astype(BF16)


def _gdn_pre(q, k, v, sm, cpb):
    b, lext, _ = q.shape
    ntot = lext // CHUNK
    rows = cpb * CHUNK
    lead = (b, ntot, 2, GDN_HEADS)

    def tok(bi, i):
        return (bi, i, 0)

    def blk(*tail, heads=GDN_HEADS):
        return pl.BlockSpec((None, cpb, 2, heads) + tail, lambda bi, i: (bi, i, 0, 0, 0, 0))

    sds = jax.ShapeDtypeStruct
    pairs = GDN_HEADS // 2
    return pl.pallas_call(
        functools.partial(_gdn_pre_kernel, cpb=cpb),
        grid=(b, ntot // cpb),
        in_specs=[pl.BlockSpec((None, rows, GDN_QK), tok), pl.BlockSpec((None, rows, GDN_QK), tok),
                  pl.BlockSpec((None, rows, GDN_V), tok), pl.BlockSpec((None, rows, LANE), tok)],
        out_specs=[blk(CHUNK, GDN_DV), blk(2 * CHUNK, GDN_DK), blk(CHUNK + GDN_DK, 2 * CHUNK, heads=pairs),
                   pl.BlockSpec((None, cpb, SUBLANE, LANE), lambda bi, i: (bi, i, 0, 0))],
        out_shape=[sds(lead + (CHUNK, GDN_DV), BF16), sds(lead + (2 * CHUNK, GDN_DK), BF16),
                   sds((b, ntot, 2, pairs, CHUNK + GDN_DK, 2 * CHUNK), BF16), sds((b, ntot, SUBLANE, LANE), F32)],
        compiler_params=_params("parallel", "parallel"),
        name="gdn_pre",
    )(q, k, v, sm)


def _scan_kernel(u0f, cqf, akf, ggf, mf, qdf, lgf, u0b, cqb, akb, ggb, mb, qdb, lgb,
                 gof_ref, gob_ref, lof_ref, lob_ref, sg_ref, sl_ref, *, nctx):
    step = pl.program_id(1)

    @pl.when(step == 0)
    def _():
        sg_ref[...] = jnp.zeros(sg_ref.shape, F32)
        sl_ref[...] = jnp.zeros(sl_ref.shape, F32)

    refs = ((u0f, cqf, akf, ggf, mf, qdf, lgf, gof_ref, lof_ref), (u0b, cqb, akb, ggb, mb, qdb, lgb, gob_ref, lob_ref))
    insts = [(d, h) for d in range(2) for h in range(GDN_HEADS)]
    dot = functools.partial(jnp.dot, preferred_element_type=F32)
    cps = u0f.shape[0]
    s = {i: sg_ref[i[0], i[1]] for i in insts}
    sl = {i: sl_ref[i[0], i[1]] for i in insts}
    def pair_rhs(xa, xb):
        z = jnp.zeros(xa.shape, xa.dtype)
        return jnp.concatenate([jnp.concatenate([xa, z], axis=1), jnp.concatenate([z, xb], axis=1)], axis=0)

    pairs = [(d, p) for d in range(2) for p in range(GDN_HEADS // 2)]
    for ci in range(cps):
        cd = (ci, cps - 1 - ci)
        cqs = {(d, h): dot(refs[d][1][cd[d], h], s[d, h].astype(BF16)) for d, h in insts}
        ol, r = {}, {}
        for d, p in pairs:
            olp = dot(refs[d][5][cd[d], p], pair_rhs(sl[d, 2 * p].astype(BF16), sl[d, 2 * p + 1].astype(BF16)))
            ol[d, 2 * p], ol[d, 2 * p + 1] = olp[:, 0:GLA_DV], olp[:, GLA_DV:2 * GLA_DV]
        u = {(d, h): refs[d][0][cd[d], h].astype(F32) - cqs[d, h][0:CHUNK] for d, h in insts}
        for d, p in pairs:
            rp = dot(refs[d][2][cd[d], p], pair_rhs(u[d, 2 * p].astype(BF16), u[d, 2 * p + 1].astype(BF16)))
            r[d, 2 * p], r[d, 2 * p + 1] = rp[:, 0:GDN_DV], rp[:, GDN_DV:2 * GDN_DV]
        for d, h in insts:
            lane = d * GDN_HEADS + h
            s[d, h] = s[d, h] * refs[d][3][cd[d], lane:lane + 1, :] + r[d, h][CHUNK:CHUNK + GDN_DK]
            sl[d, h] = sl[d, h] * refs[d][6][cd[d], :, lane:lane + 1] + refs[d][4][cd[d], h].astype(F32)

        @pl.when(step >= nctx)
        def _():
            for d, h in insts:
                rows = slice(cd[d] * CHUNK, (cd[d] + 1) * CHUNK)
                og = cqs[d, h][CHUNK:2 * CHUNK] + r[d, h][0:CHUNK]
                refs[d][7][rows, h * GDN_DV:(h + 1) * GDN_DV] = og.astype(BF16)
                refs[d][8][rows, h * GLA_DV:(h + 1) * GLA_DV] = ol[d, h].astype(BF16)

    for d, h in insts:
        sg_ref[d, h] = s[d, h]
        sl_ref[d, h] = sl[d, h]


def _scan(u0, cq, akd, ggl, m, qd, lgl, nctx, cps):
    b, nchunks = u0.shape[:2]
    assert nctx % cps == 0 and nchunks % cps == 0
    nlat = nchunks - nctx
    ntot, nctx = nchunks // cps, nctx // cps

    def specs(d):
        def chunk(bi, s):
            return (bi, _ext_chunk(d, s, nctx, ntot))

        def big(*tail, heads=GDN_HEADS):
            return pl.BlockSpec((None, cps, None, heads) + tail, lambda bi, s: chunk(bi, s) + (d, 0, 0, 0))

        def small(*tail):
            return pl.BlockSpec((None, cps) + tail, lambda bi, s: chunk(bi, s) + (0, 0))

        half = GDN_HEADS // 2
        return [big(CHUNK, GDN_DV), big(2 * CHUNK, GDN_DK), big(CHUNK + GDN_DK, 2 * CHUNK, heads=half),
                small(SUBLANE, LANE), big(GLA_DK, GLA_DV), big(CHUNK, 2 * GLA_DK, heads=half), small(GLA_DK, LANE)]

    def out_spec(d, width):
        return pl.BlockSpec((None, cps * CHUNK, width), lambda bi, s: (bi, _lat_chunk(d, s, nctx, ntot), 0))

    sds = jax.ShapeDtypeStruct
    return pl.pallas_call(
        functools.partial(_scan_kernel, nctx=nctx),
        grid=(b, ntot),
        in_specs=specs(0) + specs(1),
        out_specs=[out_spec(0, GDN_V), out_spec(1, GDN_V), out_spec(0, GLA_V), out_spec(1, GLA_V)],
        out_shape=[sds((b, nlat * CHUNK, GDN_V), BF16)] * 2 + [sds((b, nlat * CHUNK, GLA_V), BF16)] * 2,
        scratch_shapes=[pltpu.VMEM((2, GDN_HEADS, GDN_DK, GDN_DV), F32),
                        pltpu.VMEM((2, GLA_HEADS, GLA_DK, GLA_DV), F32)],
        compiler_params=_params("parallel", "arbitrary"),
        name="scan",
    )(u0, cq, akd, ggl, m, qd, lgl, u0, cq, akd, ggl, m, qd, lgl)


def _gla_pre_kernel(q_ref, k_ref, v_ref, la_ref, m_ref, qd_ref, gl_ref, intra_ref, *, cpb):
    tri = [_tri_masks(d) for d in range(2)]
    for c in range(cpb):
        rows = slice(c * CHUNK, (c + 1) * CHUNK)
        q = q_ref[rows, :] * (GLA_DK ** -0.5)
        k = k_ref[rows, :]
        v = v_ref[rows, :]
        la = la_ref[rows, :]
        pref = _cumsum_rows(tri[0][0], la)
        tot = jnp.sum(la, axis=0, keepdims=True)
        lanes = lax.broadcasted_iota(jnp.int32, la.shape, 1)
        bwd_lane = lanes >= GLA_QK
        bc = jnp.where(bwd_lane, tot - pref + la, pref)
        mid = CHUNK // 2
        ref = jnp.where(bwd_lane[0:1, :], bc[CHUNK - 1 - mid:CHUNK - mid, :], bc[mid:mid + 1, :])
        q2 = jnp.concatenate([q, q], axis=1)
        k2 = jnp.concatenate([k, k], axis=1)
        qa = q2 * jnp.exp(bc - ref)
        ka = k2 * jnp.exp(ref - bc)
        qd = (q2 * jnp.exp(bc)).astype(BF16)
        kdt = (k2 * jnp.exp(tot - bc)).T
        glcol = jnp.exp(jnp.sum(la.T, axis=1, keepdims=True))
        out_lanes = lax.broadcasted_iota(jnp.int32, (GLA_DK, LANE), 1)
        gl = jnp.zeros((GLA_DK, LANE), F32)
        for h in range(GLA_HEADS):
            vh = v[:, h * GLA_DV:(h + 1) * GLA_DV]
            intra = None
            for d in range(2):
                cs = slice(d * GLA_QK + h * GLA_DK, d * GLA_QK + (h + 1) * GLA_DK)
                attn = jnp.where(tri[d][0], _dot_nt(qa[:, cs], ka[:, cs]), 0.0)
                im = _dot(jnp.concatenate([attn, kdt[cs, :]], axis=0), vh)
                intra = im[0:CHUNK] if intra is None else intra + im[0:CHUNK]
                m_ref[c, d, h] = im[CHUNK:CHUNK + GLA_DK].astype(BF16)
                gl = jnp.where(out_lanes == d * GLA_HEADS + h, glcol[cs, :], gl)
            intra_ref[rows, h * GLA_DV:(h + 1) * GLA_DV] = intra.astype(BF16)
        for d in range(2):
            for p in range(GLA_HEADS // 2):
                qd_ref[c, d, p] = qd[:, d * GLA_QK + 2 * p * GLA_DK:d * GLA_QK + 2 * (p + 1) * GLA_DK]
        gl_ref[c] = gl


def _gla_pre(rest, loga, cpb):
    b, lext, _ = rest.shape
    ntot = lext // CHUNK
    rows = cpb * CHUNK
    lead = (b, ntot, 2, GLA_HEADS)

    def col(colblk):
        return lambda bi, i: (bi, i, colblk)

    def blk(*tail, heads=GLA_HEADS):
        return pl.BlockSpec((None, cpb, 2, heads) + tail, lambda bi, i: (bi, i, 0, 0, 0, 0))

    sds = jax.ShapeDtypeStruct
    return pl.pallas_call(
        functools.partial(_gla_pre_kernel, cpb=cpb),
        grid=(b, ntot // cpb),
        in_specs=[pl.BlockSpec((None, rows, GLA_QK), col((COL_LQ - COL_GZ) // GLA_QK)),
                  pl.BlockSpec((None, rows, GLA_QK), col((COL_LK - COL_GZ) // GLA_QK)),
                  pl.BlockSpec((None, rows, GLA_V), col((COL_LV - COL_GZ) // GLA_V)),
                  pl.BlockSpec((None, rows, 2 * GLA_QK), col(0))],
        out_specs=[blk(GLA_DK, GLA_DV), blk(CHUNK, 2 * GLA_DK, heads=GLA_HEADS // 2),
                   pl.BlockSpec((None, cpb, GLA_DK, LANE), lambda bi, i: (bi, i, 0, 0)),
                   pl.BlockSpec((None, rows, GLA_V), col(0))],
        out_shape=[sds(lead + (GLA_DK, GLA_DV), BF16), sds((b, ntot, 2, GLA_HEADS // 2, CHUNK, 2 * GLA_DK), BF16),
                   sds((b, ntot, GLA_DK, LANE), F32), sds((b, lext, GLA_V), BF16)],
        compiler_params=_params("parallel", "parallel"),
        name="gla_pre",
    )(rest, rest, rest, loga)


def _head_gate(o, gain, z, heads, dv):
    parts = []
    for h in range(heads):
        sl = slice(h * dv, (h + 1) * dv)
        oh = o[:, sl]
        parts.append(oh * lax.rsqrt(jnp.mean(oh * oh, axis=-1, keepdims=True) + EPS) * gain * _silu(z[:, sl]))
    return jnp.concatenate(parts, axis=-1)


def _ab_mix(refs, w_ref):
    gf_ref, gb_ref, lf_ref, lb_ref, li_ref, gz_ref, lg_ref, gn_ref, ln_ref = refs
    og = gf_ref[...].astype(F32) + gb_ref[...].astype(F32)
    ol = lf_ref[...].astype(F32) + lb_ref[...].astype(F32) + li_ref[0].astype(F32)
    yg = _head_gate(og, gn_ref[...], gz_ref[0], GDN_HEADS, GDN_DV)
    yl = _head_gate(ol, ln_ref[...], lg_ref[0], GLA_HEADS, GLA_DV)
    return (jnp.dot(yg.astype(BF16), w_ref[0:GDN_V, :], preferred_element_type=F32)
            + jnp.dot(yl.astype(BF16), w_ref[GDN_V:GDN_V + GLA_V, :], preferred_element_type=F32))


def _hy_mix(refs, w_ref):
    return jnp.dot(refs[0][...].astype(BF16), w_ref[...], preferred_element_type=F32)


def _tail_kernel(*refs, n_mix, mix_fn, nsplit, final_norm):
    mix_refs = refs[:n_mix]
    wo_ref, h_ref, g1_ref, gn_ref, sh_ref, sc_ref, g2_ref, w1_ref, w3_ref, w2_ref, fn_ref, o_ref = refs[n_mix:]
    h = h_ref[...] + g1_ref[...] * mix_fn(mix_refs, wo_ref)
    a = _modulated(h, gn_ref[...], sh_ref[...], sc_ref[...]).astype(BF16)
    fc = w1_ref.shape[1] // nsplit
    acc = jnp.zeros(h.shape, F32)
    for j in range(nsplit):
        u1 = jnp.dot(a, w1_ref[:, j * fc:(j + 1) * fc], preferred_element_type=F32)
        u3 = jnp.dot(a, w3_ref[:, j * fc:(j + 1) * fc], preferred_element_type=F32)
        acc = acc + jnp.dot((_silu(u1) * u3).astype(BF16), w2_ref[j * fc:(j + 1) * fc, :],
                            preferred_element_type=F32)
    y = h + g2_ref[...] * acc
    if final_norm:
        y = y * lax.rsqrt(jnp.mean(y * y, axis=-1, keepdims=True) + EPS) * fn_ref[...]
    o_ref[...] = y


def _tail(mix_args, mix_specs, mix_fn, w_out, h, g1, gain, shift, scale, g2, w1, w3, w2, final_w, tm, final_norm):
    b, l, d = h.shape
    dff = w1.shape[1]

    def tok(bi, i):
        return (bi, i, 0)

    def vec(bi, i):
        return (bi, 0, 0)

    def const(bi, i):
        return (0, 0)

    return pl.pallas_call(
        functools.partial(_tail_kernel, n_mix=len(mix_args), mix_fn=mix_fn, nsplit=2, final_norm=final_norm),
        grid=(b, l // tm),
        in_specs=list(mix_specs) + [
            _resident(w_out.shape, const), pl.BlockSpec((None, tm, d), tok), pl.BlockSpec((None, 1, d), vec),
            _resident((1, d), const), pl.BlockSpec((None, 1, d), vec), pl.BlockSpec((None, 1, d), vec),
            pl.BlockSpec((None, 1, d), vec), _resident((d, dff), const), _resident((d, dff), const),
            _resident((dff, d), const), _resident((1, d), const)],
        out_specs=pl.BlockSpec((None, tm, d), tok),
        out_shape=jax.ShapeDtypeStruct((b, l, d), F32),
        compiler_params=_params("parallel", "parallel"),
        name="mix_out_ffn",
    )(*mix_args, w_out, h, g1, gain.reshape(1, d), shift, scale, g2, w1, w3, w2, final_w.reshape(1, d))


def _hy_hidden_kernel(fb_ref, w1t_ref, w1c_ref, w1s_ref, b1_ref, w2_ref, b2_ref, w3_ref, b3_ref, fr_ref, o_ref,
                      *, tile, seq_len):
    half = tile // 2
    hid = o_ref.shape[1]
    lane_hi = lax.broadcasted_iota(jnp.int32, (half, 2 * hid), 1) >= hid
    rows = (pl.program_id(0) * tile + lax.broadcasted_iota(jnp.int32, (half, 2 * hid), 0)
            + jnp.where(lane_hi, half, 0))
    pos = jnp.where(rows < seq_len, rows, 2 * seq_len - rows).astype(F32)
    t = pos * (1.0 / (seq_len - 1))
    ang = (pos * (2.0 * math.pi / seq_len)) * fb_ref[...]
    fr = fr_ref[...]
    pre = t * w1t_ref[...] + _dot_hi(jnp.cos(ang), w1c_ref[...]) - _dot_hi(jnp.sin(ang), w1s_ref[...])
    hdn = jnp.sin(fr * (pre + b1_ref[...]))
    hdn = jnp.sin(fr * (_dot_hi(hdn, w2_ref[...]) + b2_ref[...]))
    hdn = jnp.sin(fr * (_dot_hi(hdn, w3_ref[...]) + b3_ref[...]))
    o_ref[0:half, :] = hdn[:, 0:hid]
    o_ref[half:tile, :] = hdn[:, hid:2 * hid]


def _hy_hidden(w1, b1, w2, b2, w3, b3, freq, seq_len, tile):
    hid = w1.shape[1]
    assert 2 * hid == LANE and HY_BANDS <= hid

    def twice(v):
        return jnp.tile(v.reshape(1, hid), (1, 2))

    def block_diag(w):
        k = w.shape[0]
        return jnp.zeros((2 * hid, 2 * hid), F32).at[:k, :hid].set(w).at[hid:hid + k, hid:].set(w)

    bands = jnp.linspace(1e-4, HY_BANDS - 1, HY_BANDS, dtype=F32)
    fb = jnp.zeros((1, 2 * hid), F32).at[0, :HY_BANDS].set(bands).at[0, hid:hid + HY_BANDS].set(bands)
    args = (fb, twice(w1[0]), block_diag(w1[1:1 + HY_BANDS]), block_diag(w1[1 + HY_BANDS:1 + 2 * HY_BANDS]),
            twice(b1), block_diag(w2), twice(b2), block_diag(w3), twice(b3), twice(freq))
    return pl.pallas_call(
        functools.partial(_hy_hidden_kernel, tile=tile, seq_len=seq_len),
        grid=(2 * seq_len // tile,),
        in_specs=[pl.BlockSpec(a.shape, lambda i: (0, 0)) for a in args],
        out_specs=pl.BlockSpec((tile, hid), lambda i: (i, 0)),
        out_shape=jax.ShapeDtypeStruct((2 * seq_len, hid), F32),
        compiler_params=_params("parallel"),
        name="hy_hidden",
    )(*args)


def _fft_matrices(r):
    n = r * r
    idx = jnp.arange(r, dtype=jnp.int32)

    def cis(m, period):
        ang = (2.0 * math.pi / period) * m.astype(F32)
        return jnp.cos(ang), -jnp.sin(ang)

    fr, fi = cis((idx[:, None] * idx[None, :]) % r, r)
    half = r // 2
    f1_data = jnp.block([[fr[:, :half], -fi[:, :half]], [fi[:, :half], fr[:, :half]]])
    f1_real = jnp.concatenate([fr, fi], axis=0)
    f3 = jnp.block([[fr[:half, :], fi[:half, :]], [-fi[:half, :], fr[:half, :]]]) * (1.0 / n)
    tr, ti = cis((idx[:, None] * idx[None, :]) % n, n)
    gr = tr[:, None, :] * fr[None, :, :] - ti[:, None, :] * fi[None, :, :]
    gi = tr[:, None, :] * fi[None, :, :] + ti[:, None, :] * fr[None, :, :]
    g = jnp.concatenate([jnp.concatenate([gr, -gi], axis=2), jnp.concatenate([gi, gr], axis=2)], axis=1)
    gh = jnp.swapaxes(g, 1, 2)
    return f1_data.astype(BF16), f1_real.astype(BF16), f3.astype(BF16), g.astype(BF16), gh.astype(BF16)


FFT_BG = 16
FFT_DT = 256


def _slab(ref, j):
    return jnp.concatenate([ref[i, :, j, :] for i in range(ref.shape[0])], axis=0)


def _store_interleaved(o_ref, vals):
    x = jnp.swapaxes(jnp.stack([v.astype(o_ref.dtype) for v in vals], axis=0), 0, 1)
    o_ref[...] = x.reshape(o_ref.shape)


def _time_block(arr, sel, n):
    lead = arr.ndim - 4
    return pl.BlockSpec((None,) * lead + (n, arr.shape[-3], FFT_BG, FFT_DT),
                        lambda p, g, dd: tuple(sel) + (p, 0, g, dd))


def _fft_stage1_kernel(f_ref, x_ref, o_ref):
    _store_interleaved(o_ref, [jnp.dot(f_ref[...], _slab(x_ref, j).astype(BF16), preferred_element_type=F32)
                               for j in range(x_ref.shape[2])])


def _fft_stage1(f1, x5, sel, n):
    r2, _ = f1.shape
    r = r2 // 2
    items, _, rb, d = x5.shape[-4:]
    groups = items // n
    return pl.pallas_call(
        _fft_stage1_kernel,
        grid=(groups, rb // FFT_BG, d // FFT_DT),
        in_specs=[_resident(f1.shape, lambda p, g, dd: (0, 0)), _time_block(x5, sel, n)],
        out_specs=pl.BlockSpec((None, 2, r, FFT_BG, FFT_DT), lambda p, g, dd: (p, 0, 0, g, dd)),
        out_shape=jax.ShapeDtypeStruct((groups, 2, r, rb, d), BF16),
        compiler_params=_params("parallel", "parallel", "parallel"),
        name="fft_stage1",
    )(f1, x5)


def _fft_filter_stage1_kernel(f_ref, hdn_ref, fo_ref, dl_ref, o_ref, *, seq_len):
    r, bg, _ = hdn_ref.shape
    orders = fo_ref.shape[1]
    half = r // 2
    a_idx = lax.broadcasted_iota(jnp.int32, (r, 1), 0)
    ps = [[] for _ in range(orders)]
    for j in range(bg):
        rows = a_idx * r + (pl.program_id(0) * bg + j)
        pos = jnp.where(rows < seq_len, rows, 2 * seq_len - rows).astype(F32)
        t = pos * (1.0 / (seq_len - 1))
        window = jnp.where(rows != seq_len, jnp.exp(-t * dl_ref[...]) + HY_SHIFT, 0.0)
        hj = hdn_ref[:, j, :]
        for n in range(orders):
            taps = jnp.concatenate([_dot(hj[:half], fo_ref[0, n]), _dot(hj[half:], fo_ref[1, n])], axis=0) * window
            ps[n].append(jnp.dot(f_ref[...], taps.astype(BF16), preferred_element_type=F32))
    for n in range(orders):
        _store_interleaved(o_ref.at[n], ps[n])


def _fft_filter_stage1(f1, hdn3, fo, deltas, seq_len):
    r2, r = f1.shape
    hid = hdn3.shape[-1]
    _, orders, _, d = fo.shape
    return pl.pallas_call(
        functools.partial(_fft_filter_stage1_kernel, seq_len=seq_len),
        grid=(r // FFT_BG, d // FFT_DT),
        in_specs=[_resident(f1.shape, lambda g, dd: (0, 0)),
                  pl.BlockSpec((r, FFT_BG, hid), lambda g, dd: (0, g, 0)),
                  pl.BlockSpec((2, orders, hid, FFT_DT), lambda g, dd: (0, 0, 0, dd)),
                  pl.BlockSpec((1, FFT_DT), lambda g, dd: (0, dd))],
        out_specs=pl.BlockSpec((orders, 2, r, FFT_BG, FFT_DT), lambda g, dd: (0, 0, 0, g, dd)),
        out_shape=jax.ShapeDtypeStruct((orders, 2, r, r, d), BF16),
        compiler_params=_params("parallel", "parallel"),
        name="fft_filter_stage1",
    )(f1, hdn3, fo, deltas)


FFT_KLO = 4


def _fft_stage2_kernel(g_ref, gh_ref, k_ref, p_ref, o_ref):
    r = p_ref.shape[3]
    for kk in range(p_ref.shape[2]):
        kq = k_ref[:, kk]
        h = jnp.dot(g_ref[kk], kq.reshape(2 * r, kq.shape[2]), preferred_element_type=F32)
        hr, hi = h[:r], h[r:]
        for n in range(p_ref.shape[0]):
            p = p_ref[n, :, kk]
            x = jnp.dot(g_ref[kk], p.reshape(2 * r, p.shape[2]), preferred_element_type=F32)
            xr, xi = x[:r], x[r:]
            y = jnp.concatenate([xr * hr - xi * hi, xr * hi + xi * hr], axis=0)
            q = jnp.dot(gh_ref[kk], y.astype(BF16), preferred_element_type=F32)
            o_ref[n, :, kk] = q.reshape(2, r, q.shape[1]).astype(o_ref.dtype)


def _fft_stage2(g, gh, pk, order, p5):
    pairs, _, r, _, d = p5.shape
    return pl.pallas_call(
        _fft_stage2_kernel,
        grid=(r // FFT_KLO,),
        in_specs=[pl.BlockSpec((FFT_KLO, 2 * r, 2 * r), lambda kk: (kk, 0, 0)),
                  pl.BlockSpec((FFT_KLO, 2 * r, 2 * r), lambda kk: (kk, 0, 0)),
                  pl.BlockSpec((None, 2, FFT_KLO, r, d), lambda kk: (order, 0, kk, 0, 0)),
                  pl.BlockSpec((pairs, 2, FFT_KLO, r, d), lambda kk: (0, 0, kk, 0, 0))],
        out_specs=pl.BlockSpec((pairs, 2, FFT_KLO, r, d), lambda kk: (0, 0, kk, 0, 0)),
        out_shape=jax.ShapeDtypeStruct(p5.shape, BF16),
        compiler_params=_params("parallel"),
        name="fft_stage2",
    )(g, gh, pk, p5)


def _rows_by_b(ref):
    bg = ref.shape[2]
    if ref.dtype == F32:
        return [_slab(ref, j) for j in range(bg)]
    x = ref[...]
    x = jnp.swapaxes(x.reshape(x.shape[0] * x.shape[1], bg, x.shape[3]), 0, 1)
    return [x[j] for j in range(bg)]


def _fft_stage3_kernel(f3_ref, f1_ref, q_ref, z_ref, gate_ref, skip_ref, zo_ref, po_ref=None):
    q, z, gate = _rows_by_b(q_ref), _rows_by_b(z_ref), _rows_by_b(gate_ref)
    skip = skip_ref[...]
    zs, ps = [], []
    for j in range(len(q)):
        y = jnp.dot(f3_ref[...], q[j], preferred_element_type=F32)
        zn = gate[j] * (y + z[j] * skip)
        zs.append(zn)
        if po_ref is not None:
            ps.append(jnp.dot(f1_ref[...], zn.astype(BF16), preferred_element_type=F32))
    _store_interleaved(zo_ref, zs)
    if po_ref is not None:
        _store_interleaved(po_ref, ps)


def _fft_stage3(f3, f1, q5, z5, z_sel, g5, g_sel, skip, chain):
    pairs, _, r, rb, d = q5.shape
    a = z5.shape[-3]
    sds = jax.ShapeDtypeStruct
    out_specs = [pl.BlockSpec((2, a, FFT_BG, FFT_DT), lambda p, g, dd: (p, 0, g, dd))]
    out_shape = [sds((2 * pairs, a, rb, d), BF16)]
    if chain:
        out_specs.append(pl.BlockSpec((None, 2, r, FFT_BG, FFT_DT), lambda p, g, dd: (p, 0, 0, g, dd)))
        out_shape.append(sds(q5.shape, BF16))
    return pl.pallas_call(
        _fft_stage3_kernel,
        grid=(pairs, rb // FFT_BG, d // FFT_DT),
        in_specs=[_resident(f3.shape, lambda p, g, dd: (0, 0)), _resident(f1.shape, lambda p, g, dd: (0, 0)),
                  pl.BlockSpec((None, 2, r, FFT_BG, FFT_DT), lambda p, g, dd: (p, 0, 0, g, dd)),
                  _time_block(z5, z_sel, 2), _time_block(g5, g_sel, 2),
                  pl.BlockSpec((1, FFT_DT), lambda p, g, dd: (0, dd))],
        out_specs=out_specs,
        out_shape=out_shape,
        compiler_params=_params("parallel", "parallel", "parallel"),
        name="fft_stage3",
    )(f3, f1, q5, z5, g5, skip)


TAIL_ROWS = 512
HY_FRONT_ROWS = 512
AB_FRONT_ROWS = 384
PRE_CHUNKS = 4


def _regroup_ab_weights(w_in):
    d = w_in.shape[0]
    sizes = (GDN_QK, GDN_QK, GDN_V, GDN_V, 2 * GDN_HEADS, 2 * GDN_HEADS, GLA_QK, GLA_QK, GLA_V, GLA_V,
             2 * GLA_GATE_RANK)
    offs = [0]
    for s in sizes:
        offs.append(offs[-1] + s)
    gq, gk, gv, gz, ga, gb, lq, lk, lv, lg, llr = (w_in[:, offs[i]:offs[i + 1]] for i in range(len(sizes)))
    pad = jnp.zeros((d, LANE - (ga.shape[1] + gb.shape[1] + llr.shape[1])), w_in.dtype)
    return jnp.concatenate([gq, gk, gv, gz, lq, lk, lv, lg, ga, gb, llr, pad], axis=1)


def _lane_row(vals, offset):
    flat = vals.reshape(-1).astype(F32)
    return jnp.zeros((1, LANE), F32).at[0, offset:offset + flat.shape[0]].set(flat)


def _even_layer(h, ctx, mods, norm1, norm2, w_in, conv_w, a_log, dt_bias, gdn_norm, gate_w, gate_b, gla_norm,
                w_out, w1, w3, w2, final_w, final_norm):
    b, l, d = h.shape
    lctx = ctx.shape[1]
    tm = TAIL_ROWS
    sh1, sc1, g1, sh2, sc2, g2 = mods["lat"]
    csh1, csc1 = mods["ctx"][0], mods["ctx"][1]
    xe = jnp.concatenate([ctx, h], axis=1)
    shift = jnp.stack([jnp.broadcast_to(csh1, (b, d)), sh1], axis=1)[:, :, None, :]
    scale = jnp.stack([jnp.broadcast_to(csc1, (b, d)), sc1], axis=1)[:, :, None, :]
    gw = jnp.zeros((LANE, 2 * GLA_QK), F32)
    for dd in range(2):
        gw = gw.at[SM_LR + dd * GLA_GATE_RANK:SM_LR + (dd + 1) * GLA_GATE_RANK,
                   dd * GLA_QK:(dd + 1) * GLA_QK].set(gate_w[dd])
    q, k, v, sm, loga, rest = _ab_front(xe, norm1, shift, scale, _regroup_ab_weights(w_in).astype(BF16), conv_w,
                                        _lane_row(a_log, SM_G), _lane_row(dt_bias, SM_G), gw,
                                        gate_b.reshape(1, 2 * GLA_QK), AB_FRONT_ROWS, lctx)
    nctx = lctx // CHUNK
    u0, cq, akd, ggl = _gdn_pre(q, k, v, sm, PRE_CHUNKS)
    m, qd, lgl, ol_intra = _gla_pre(rest, loga, PRE_CHUNKS)
    og_f, og_b, ol_f, ol_b = _scan(u0, cq, akd, ggl, m, qd, lgl, nctx, 4)
    def lat(width):
        return pl.BlockSpec((None, tm, width), lambda bi, i: (bi, i, 0))

    def ext(width, colblk):
        return pl.BlockSpec((pl.Element(1), pl.Element(tm), pl.Element(width)),
                            lambda bi, i: (bi, SUBLANE * (lctx // SUBLANE + i * (tm // SUBLANE)), colblk * width))

    mix_args = (og_f, og_b, ol_f, ol_b, ol_intra, rest, rest, gdn_norm.reshape(1, GDN_DV), gla_norm.reshape(1, GLA_DV))
    mix_specs = [lat(GDN_V), lat(GDN_V), lat(GLA_V), lat(GLA_V), ext(GLA_V, 0), ext(GDN_V, 0),
                 ext(GLA_V, (COL_LG - COL_GZ) // GLA_V), _resident((1, GDN_DV), lambda bi, i: (0, 0)),
                 _resident((1, GLA_DV), lambda bi, i: (0, 0))]
    return _tail(mix_args, mix_specs, _ab_mix, w_out.astype(BF16), h, g1[:, None, :], norm2, sh2[:, None, :],
                 sc2[:, None, :], g2[:, None, :], w1.astype(BF16), w3.astype(BF16), w2.astype(BF16), final_w, tm,
                 final_norm)


def _hyena_layer(h, mods, norm1, norm2, w_in, conv_w, pw1, pb1, pw2, pb2, pw3, pb3, freq, filt_out, skip, w_out,
                 w1, w3, w2, final_w, final_norm):
    b, l, d = h.shape
    tm = TAIL_ROWS
    sh1, sc1, g1, sh2, sc2, g2 = mods["lat"]
    u3 = _hy_front(h, norm1, sh1[:, None, :], sc1[:, None, :], w_in.astype(BF16), conv_w, HY_FRONT_ROWS)

    n2 = 2 * l
    r = math.isqrt(n2)
    assert r * r == n2 and b % 2 == 0
    orders = skip.shape[0]
    hdn = _hy_hidden(pw1, pb1, pw2, pb2, pw3, pb3, freq, l, 512)
    fo = jnp.transpose(filt_out.reshape(filt_out.shape[0], 2, orders, d), (1, 2, 0, 3))
    deltas = jnp.abs(jnp.linspace(HY_MIN_DECAY, HY_MAX_DECAY, d, dtype=F32))[None, :]

    f1_data, f1_real, f3, g, gh = _fft_matrices(r)
    pk = _fft_filter_stage1(f1_real, hdn.reshape(r, r, hdn.shape[1]), fo, deltas, l)

    u5 = u3.reshape(3, b, r // 2, r, d)
    z_arr, z_sel = u5, (0,)
    p = _fft_stage1(f1_data, z_arr, z_sel, 2)
    for n in range(orders):
        q = _fft_stage2(g, gh, pk, n, p)
        chain = n + 1 < orders
        res = _fft_stage3(f3, f1_data, q, z_arr, z_sel, u5, (n + 1,), skip[n][None, :], chain)
        z_arr, z_sel = res[0], ()
        if chain:
            p = res[1]
    mix_specs = [pl.BlockSpec((None, tm, d), lambda bi, i: (bi, i, 0))]
    return _tail((z_arr.reshape(b, l, d),), mix_specs, _hy_mix, w_out.astype(BF16), h, g1[:, None, :], norm2,
                 sh2[:, None, :], sc2[:, None, :], g2[:, None, :], w1.astype(BF16), w3.astype(BF16), w2.astype(BF16),
                 final_w, tm, final_norm)


def kernel(x, c, ctx, c_ctx, mod_w, mod_b, norm1_w, norm2_w, ab_w_in, ab_conv_w, gdn_a_log, gdn_dt_bias, gdn_norm_w, gla_gate_w, gla_gate_b, gla_norm_w, ab_w_out, hy_w_in, hy_conv_w, hy_pos_w1, hy_pos_b1, hy_pos_w2, hy_pos_b2, hy_pos_w3, hy_pos_b3, hy_freq, hy_filt_out, hy_skip, hy_w_out, ffn_w1, ffn_w3, ffn_w2, final_norm_w):
    b, l, d = x.shape
    depth = mod_w.shape[0]
    assert b + 1 <= SUBLANE
    cvec = jnp.concatenate([c, c_ctx[None, :], jnp.zeros((SUBLANE - b - 1, d), F32)], axis=0)
    mod = _modulation(cvec, mod_w, mod_b)
    h = x
    for i in range(depth):
        mods = {"lat": [mod[i, :b, j * d:(j + 1) * d] for j in range(6)],
                "ctx": [mod[i, b, j * d:(j + 1) * d] for j in range(6)]}
        last = i == depth - 1
        if i % 2 == 0:
            assert not any(j % 2 == 0 for j in range(i + 1, depth))
            e = i // 2
            h = _even_layer(h, ctx, mods, norm1_w[i], norm2_w[i], ab_w_in[e], ab_conv_w[e], gdn_a_log[e],
                            gdn_dt_bias[e], gdn_norm_w[e], gla_gate_w[e], gla_gate_b[e], gla_norm_w[e], ab_w_out[e],
                            ffn_w1[i], ffn_w3[i], ffn_w2[i], final_norm_w, last)
        else:
            o = i // 2
            h = _hyena_layer(h, mods, norm1_w[i], norm2_w[i], hy_w_in[o], hy_conv_w[o], hy_pos_w1[o], hy_pos_b1[o],
                             hy_pos_w2[o], hy_pos_b2[o], hy_pos_w3[o], hy_pos_b3[o], hy_freq[o], hy_filt_out[o],
                             hy_skip[o], hy_w_out[o], ffn_w1[i], ffn_w3[i], ffn_w2[i], final_norm_w, last)
    return h
```
